```python
import math
import jax, jax.numpy as jnp
from jax import lax
import numpy as np

D_MODEL = 2048
BATCH = 2
SEQ = 8192
DEPTH = 2

A_HEADS = 8
A_HALF = 64
A_VDIM = 2 * A_HALF
LRU_WIDTH = 1024
LRU_BLOCKS = 8
LRU_BDIM = LRU_WIDTH // LRU_BLOCKS
CONV_WIDTH = 4
LRU_C = 8.0
C_HEADS = 8
C_HDIM = 128
N_BRANCH = 3
BRANCH_WIDTH = 1024
REL_BUCKETS = 32
REL_MAX_DIST = 128
Q_BLOCK = 128
PEER_HEADS = 8
PEER_NKEYS = 128
PEER_EXPERTS = PEER_NKEYS * PEER_NKEYS
PEER_QDIM = 128
PEER_HALF = PEER_QDIM // 2
PEER_TOPK = 16
PEER_CHUNK = 128
PLE_DIM = 256
DEEPNORM_ALPHA = (2 * DEPTH) ** 0.25
DEEPNORM_BETA = (8 * DEPTH) ** -0.25
LN_EPS = 1e-5

IN_SIZES = (
    A_HEADS * 2 * A_HALF,
    A_HEADS * 2 * A_HALF,
    A_HEADS * A_VDIM,
    LRU_WIDTH,
    LRU_WIDTH,
    C_HEADS * C_HDIM,
    C_HEADS * C_HDIM,
    C_HEADS * C_HDIM,
    C_HEADS,
    N_BRANCH * D_MODEL,
)
IN_TOTAL = sum(IN_SIZES)

kernel_name = "hybrid_diffattn_rglru_fox_peer_deepnorm"


def _split(t, sizes):
    outs, start = [], 0
    for n in sizes:
        outs.append(t[..., start:start + n])
        start += n
    return outs


def _layer_norm(x, g, b):
    xf = x.astype(jnp.float32)
    mu = jnp.mean(xf, axis=-1, keepdims=True)
    var = jnp.mean(jnp.square(xf - mu), axis=-1, keepdims=True)
    y = (xf - mu) * lax.rsqrt(var + LN_EPS)
    return (y * g.astype(jnp.float32) + b.astype(jnp.float32)).astype(x.dtype)


def _rms_norm(x, g):
    xf = x.astype(jnp.float32)
    y = xf * lax.rsqrt(jnp.mean(jnp.square(xf), axis=-1, keepdims=True) + LN_EPS)
    return y * g.astype(jnp.float32)


def _rel_bucket(rel):
    n = jnp.maximum(rel, 0)
    max_exact = REL_BUCKETS // 2
    nf = jnp.maximum(n, 1).astype(jnp.float32)
    large = max_exact + (jnp.log(nf / max_exact) / math.log(REL_MAX_DIST / max_exact)
                         * (REL_BUCKETS - max_exact)).astype(jnp.int32)
    large = jnp.minimum(large, REL_BUCKETS - 1)
    return jnp.where(n < max_exact, n, large)


def _diff_attention(q, k, v, rel_table, lam):
    B, S = q.shape[0], q.shape[1]
    nb = S // Q_BLOCK
    scale = A_HALF ** -0.5
    kpos = jnp.arange(S)

    def block(i):
        start = i * Q_BLOCK
        qb = lax.dynamic_slice_in_dim(q, start, Q_BLOCK, axis=1)
        qpos = start + jnp.arange(Q_BLOCK)
        rel = qpos[:, None] - kpos[None, :]
        bias = jnp.transpose(rel_table[_rel_bucket(rel)], (2, 0, 1)).astype(jnp.float32)
        logits = jnp.einsum('bqhcd,bkhcd->bhcqk', qb, k).astype(jnp.float32) * scale
        logits = logits + bias[None, :, None]
        logits = jnp.where(rel >= 0, logits, -jnp.inf)
        probs = jax.nn.softmax(logits, axis=-1)
        attn = probs[:, :, 0] - lam * probs[:, :, 1]
        return jnp.einsum('bhqk,bkhd->bqhd', attn.astype(v.dtype), v)

    out = lax.map(block, jnp.arange(nb))
    return jnp.transpose(out, (1, 0, 2, 3, 4)).reshape(B, S, A_HEADS, A_VDIM)


def _forgetting_attention(q, k, v, logf):
    B, S = q.shape[0], q.shape[1]
    nb = S // Q_BLOCK
    scale = C_HDIM ** -0.5
    cum = jnp.transpose(jnp.cumsum(logf, axis=1), (0, 2, 1))
    kpos = jnp.arange(S)

    def block(i):
        start = i * Q_BLOCK
        qb = lax.dynamic_slice_in_dim(q, start, Q_BLOCK, axis=1)
        cq = lax.dynamic_slice_in_dim(cum, start, Q_BLOCK, axis=2)
        qpos = start + jnp.arange(Q_BLOCK)
        causal = qpos[:, None] >= kpos[None, :]
        logits = jnp.einsum('bqhd,bkhd->bhqk', qb, k).astype(jnp.float32) * scale
        logits = logits + cq[..., None] - cum[:, :, None, :]
        logits = jnp.where(causal, logits, -jnp.inf)
        probs = jax.nn.softmax(logits, axis=-1)
        return jnp.einsum('bhqk,bkhd->bqhd', probs.astype(v.dtype), v)

    out = lax.map(block, jnp.arange(nb))
    return jnp.transpose(out, (1, 0, 2, 3, 4)).reshape(B, S, C_HEADS * C_HDIM)


def _causal_conv(x, w, b):
    S = x.shape[1]
    xp = jnp.pad(x, ((0, 0), (CONV_WIDTH - 1, 0), (0, 0)))
    y = b
    for tap in range(CONV_WIDTH):
        y = y + xp[:, tap:tap + S] * w[tap]
    return y


def _rg_lru(x, w_a, b_a, w_x, b_x, lam):
    B, S, W = x.shape
    xb = x.reshape(B, S, LRU_BLOCKS, LRU_BDIM)
    r = jax.nn.sigmoid(jnp.einsum('bsgi,gij->bsgj', xb, w_a).reshape(B, S, W) + b_a)
    gi = jax.nn.sigmoid(jnp.einsum('bsgi,gij->bsgj', xb, w_x).reshape(B, S, W) + b_x)
    log_a = LRU_C * r.astype(jnp.float32) * jax.nn.log_sigmoid(lam.astype(jnp.float32))
    a = jnp.exp(log_a)
    u = jnp.sqrt(-jnp.expm1(2.0 * log_a)) * (gi * x).astype(jnp.float32)

    def combine(left, right):
        a1, b1 = left
        a2, b2 = right
        return a1 * a2, a2 * b1 + b2

    _, h = lax.associative_scan(combine, (a, u), axis=1)
    return h.astype(x.dtype)


def _peer(x, w_q, sub_keys, u_tab, v_tab):
    B, S, D = x.shape
    xt = x.reshape((B * S) // PEER_CHUNK, PEER_CHUNK, D)

    def chunk(xc):
        q = (xc @ w_q).reshape(PEER_CHUNK, PEER_HEADS, 2, PEER_HALF)
        s = jnp.einsum('chpd,hpnd->chpn', q, sub_keys).astype(jnp.float32)
        s1, i1 = lax.top_k(s[:, :, 0], PEER_TOPK)
        s2, i2 = lax.top_k(s[:, :, 1], PEER_TOPK)
        cand = (s1[..., :, None] + s2[..., None, :]).reshape(PEER_CHUNK, PEER_HEADS, PEER_TOPK * PEER_TOPK)
        cidx = (i1[..., :, None] * PEER_NKEYS + i2[..., None, :]).reshape(PEER_CHUNK, PEER_HEADS, PEER_TOPK * PEER_TOPK)
        sc, pos = lax.top_k(cand, PEER_TOPK)
        idx = jnp.take_along_axis(cidx, pos, axis=-1)
        g = jax.nn.softmax(sc, axis=-1)
        ue = u_tab[idx]
        ve = v_tab[idx]
        act = jax.nn.gelu(jnp.einsum('chkd,cd->chk', ue, xc).astype(jnp.float32), approximate=False)
        return jnp.einsum('chk,chkd->cd', (g * act).astype(x.dtype), ve)

    return lax.map(chunk, xt).reshape(B, S, D)


def setup_inputs(seed: int = 0) -> dict:
    key = jax.random.key(seed)
    ks = jax.random.split(key, 32)
    f32 = jnp.float32

    def nrm(k, shape, scale):
        return jax.random.normal(k, shape, f32) * scale

    s = jax.random.uniform(ks[13], (DEPTH, LRU_WIDTH), f32, minval=0.9, maxval=0.999) ** (1.0 / LRU_C)
    return {
        "x": nrm(ks[0], (BATCH, SEQ, D_MODEL), 1.0),
        "p": nrm(ks[1], (DEPTH, BATCH, SEQ, PLE_DIM), 1.0),
        "w_in": nrm(ks[2], (DEPTH, D_MODEL, IN_TOTAL), D_MODEL ** -0.5),
        "b_in": nrm(ks[3], (DEPTH, IN_TOTAL), 0.02),
        "diff_lambda": nrm(ks[4], (DEPTH, 4, A_HALF), 0.1),
        "diff_subln": 1.0 + nrm(ks[5], (DEPTH, A_VDIM), 0.02),
        "rel_bias": nrm(ks[6], (REL_BUCKETS, A_HEADS), 0.5),
        "conv_w": nrm(ks[7], (DEPTH, CONV_WIDTH, LRU_WIDTH), CONV_WIDTH ** -0.5),
        "conv_b": nrm(ks[8], (DEPTH, LRU_WIDTH), 0.02),
        "lru_wa": nrm(ks[9], (DEPTH, LRU_BLOCKS, LRU_BDIM, LRU_BDIM), LRU_BDIM ** -0.5),
        "lru_ba": nrm(ks[10], (DEPTH, LRU_WIDTH), 0.02),
        "lru_wx": nrm(ks[11], (DEPTH, LRU_BLOCKS, LRU_BDIM, LRU_BDIM), LRU_BDIM ** -0.5),
        "lru_bx": nrm(ks[12], (DEPTH, LRU_WIDTH), 0.02),
        "lru_lambda": jnp.log(s) - jnp.log1p(-s),
        "w_branch": nrm(ks[14], (DEPTH, N_BRANCH, BRANCH_WIDTH, D_MODEL), DEEPNORM_BETA * BRANCH_WIDTH ** -0.5),
        "w_o": nrm(ks[15], (DEPTH, D_MODEL, D_MODEL), DEEPNORM_BETA * D_MODEL ** -0.5),
        "ln1_g": 1.0 + nrm(ks[16], (DEPTH, D_MODEL), 0.02),
        "ln1_b": nrm(ks[17], (DEPTH, D_MODEL), 0.02),
        "peer_wq": nrm(ks[18], (DEPTH, D_MODEL, PEER_HEADS * PEER_QDIM), D_MODEL ** -0.5),
        "peer_keys": nrm(ks[19], (DEPTH, PEER_HEADS, 2, PEER_NKEYS, PEER_HALF), PEER_HALF ** -0.5),
        "peer_u": nrm(ks[20], (DEPTH, PEER_EXPERTS, D_MODEL), D_MODEL ** -0.5),
        "peer_v": nrm(ks[21], (DEPTH, PEER_EXPERTS, D_MODEL), DEEPNORM_BETA),
        "w_ple": nrm(ks[22], (DEPTH, PLE_DIM, D_MODEL), DEEPNORM_BETA * PLE_DIM ** -0.5),
        "w_ple_gate": nrm(ks[23], (DEPTH, D_MODEL, D_MODEL), D_MODEL ** -0.5),
        "b_ple_gate": nrm(ks[24], (DEPTH, D_MODEL), 0.02),
        "ln2_g": 1.0 + nrm(ks[25], (DEPTH, D_MODEL), 0.02),
        "ln2_b": nrm(ks[26], (DEPTH, D_MODEL), 0.02),
    }


def reference(x, p, w_in, b_in, diff_lambda, diff_subln, rel_bias, conv_w, conv_b,
              lru_wa, lru_ba, lru_wx, lru_bx, lru_lambda, w_branch, w_o, ln1_g, ln1_b,
              peer_wq, peer_keys, peer_u, peer_v, w_ple, w_ple_gate, b_ple_gate,
              ln2_g, ln2_b):
    B, S, D = x.shape
    for i in range(DEPTH):
        h = x @ w_in[i] + b_in[i]
        qa, ka, va, bx, bg, qc, kc, vc, fc, gt = _split(h, IN_SIZES)

        lam_init = 0.8 - 0.6 * math.exp(-0.3 * i)
        lv = diff_lambda[i].astype(jnp.float32)
        lam = jnp.exp(jnp.sum(lv[0] * lv[1])) - jnp.exp(jnp.sum(lv[2] * lv[3])) + lam_init
        ya = _diff_attention(qa.reshape(B, S, A_HEADS, 2, A_HALF),
                             ka.reshape(B, S, A_HEADS, 2, A_HALF),
                             va.reshape(B, S, A_HEADS, A_VDIM), rel_bias, lam)
        ya = (_rms_norm(ya, diff_subln[i]) * (1.0 - lam_init)).astype(x.dtype).reshape(B, S, BRANCH_WIDTH)

        xc = _causal_conv(bx, conv_w[i], conv_b[i])
        yb = jax.nn.gelu(bg) * _rg_lru(xc, lru_wa[i], lru_ba[i], lru_wx[i], lru_bx[i], lru_lambda[i])

        logf = jax.nn.log_sigmoid(fc.astype(jnp.float32))
        yc = _forgetting_attention(qc.reshape(B, S, C_HEADS, C_HDIM),
                                   kc.reshape(B, S, C_HEADS, C_HDIM),
                                   vc.reshape(B, S, C_HEADS, C_HDIM), logf)

        gate = jax.nn.sigmoid(gt.reshape(B, S, N_BRANCH, D))
        merged = (gate[:, :, 0] * (ya @ w_branch[i, 0])
                  + gate[:, :, 1] * (yb @ w_branch[i, 1])
                  + gate[:, :, 2] * (yc @ w_branch[i, 2]))
        x = _layer_norm(DEEPNORM_ALPHA * x + merged @ w_o[i], ln1_g[i], ln1_b[i])

        ple = jax.nn.sigmoid(x @ w_ple_gate[i] + b_ple_gate[i]) * (p[i] @ w_ple[i])
        yf = _peer(x, peer_wq[i], peer_keys[i], peer_u[i], peer_v[i])
        x = _layer_norm(DEEPNORM_ALPHA * x + yf + ple, ln2_g[i], ln2_b[i])
    return x
```

```python
import functools
import math

import jax
import jax.numpy as jnp
import numpy as np
from jax import lax
from jax.experimental import pallas as pl
from jax.experimental.pallas import tpu as pltpu

F32, BF16, I32 = jnp.float32, jnp.bfloat16, jnp.int32

V7X_VMEM_BYTES = 64 * 2**20
V7X_LANES = 128
V7X_SUBLANES = 8
VMEM_HEADROOM_BYTES = 8 * 2**20

A_HEADS = 8
A_HALF = 64
A_VDIM = 2 * A_HALF
LRU_BLOCKS = 8
CONV_WIDTH = 4
LRU_C = 8.0
C_HEADS = 8
C_HDIM = 128
N_BRANCH = 3
REL_BUCKETS = 32
REL_MAX_DIST = 128
PEER_HEADS = 8
PEER_NKEYS = 128
PEER_HALF = 64
PEER_TOPK = 16
LN_EPS = 1e-5

LOG2E = 1.4426950408889634
MASKED = -1e30


def _vmem_limit(*block_bytes, scratch_bytes=0):
    need = 2 * sum(block_bytes) + scratch_bytes + VMEM_HEADROOM_BYTES
    return int(min(max(need, 32 * 2**20), V7X_VMEM_BYTES - 4 * 2**20))


def _nbytes(shape, dtype):
    return int(np.prod(shape)) * jnp.dtype(dtype).itemsize


def _log_sigmoid(z):
    return jnp.minimum(z, 0.0) - jnp.log1p(jnp.exp(-jnp.abs(z)))


def _layer_norm_rows(y, g, b):
    mu = jnp.mean(y, axis=1, keepdims=True)
    yc = y - mu
    var = jnp.mean(yc * yc, axis=1, keepdims=True)
    return yc * lax.rsqrt(var + LN_EPS) * g + b


def _proj_kernel(x_ref, w_ref, b_ref, s_ref, o_ref, *, act):
    y = jnp.dot(x_ref[...], w_ref[...], preferred_element_type=F32)
    y = (y + b_ref[...]) * s_ref[...]
    if act == "sigmoid":
        y = jax.nn.sigmoid(y)
    o_ref[...] = y.astype(o_ref.dtype)


def _proj(x, w, b, s, *, out_dtype, tm, tn, act=None, name):
    m, k = x.shape
    n = w.shape[1]
    grid = (n // tn, m // tm)
    return pl.pallas_call(
        functools.partial(_proj_kernel, act=act),
        out_shape=jax.ShapeDtypeStruct((m, n), out_dtype),
        grid=grid,
        in_specs=[
            pl.BlockSpec((tm, k), lambda j, i: (i, 0)),
            pl.BlockSpec((k, tn), lambda j, i: (0, j)),
            pl.BlockSpec((1, tn), lambda j, i: (0, j)),
            pl.BlockSpec((1, tn), lambda j, i: (0, j)),
        ],
        out_specs=pl.BlockSpec((tm, tn), lambda j, i: (i, j)),
        compiler_params=pltpu.CompilerParams(
            dimension_semantics=("parallel", "parallel"),
            vmem_limit_bytes=_vmem_limit(_nbytes((tm, k), x.dtype), _nbytes((k, tn), w.dtype),
                                         _nbytes((tm, tn), out_dtype)),
        ),
        name=name,
    )(x, w, b, s)


def _fgate_kernel(x_ref, w_ref, b_ref, o_ref, carry_ref, *, ts):
    @pl.when(pl.program_id(1) == 0)
    def _():
        carry_ref[...] = jnp.zeros_like(carry_ref)

    z = lax.dot_general(w_ref[...], x_ref[...], (((1,), (1,)), ((), ())),
                        preferred_element_type=F32) + b_ref[...]
    c = _log_sigmoid(z)
    lane = lax.broadcasted_iota(I32, c.shape, 1)
    d = 1
    while d < ts:
        c = c + jnp.where(lane >= d, pltpu.roll(c, d, 1), 0.0)
        d *= 2
    c = c + carry_ref[:, 0:1]
    o_ref[...] = c * LOG2E
    carry_ref[...] = jnp.broadcast_to(c[:, ts - 1:ts], carry_ref.shape)


def _fgate_cumsum(x3, wf_t, bf, *, ts):
    bsz, seq, dm = x3.shape
    return pl.pallas_call(
        functools.partial(_fgate_kernel, ts=ts),
        out_shape=jax.ShapeDtypeStruct((bsz, C_HEADS, seq), F32),
        grid=(bsz, seq // ts),
        in_specs=[
            pl.BlockSpec((None, ts, dm), lambda b, i: (b, i, 0)),
            pl.BlockSpec((C_HEADS, dm), lambda b, i: (0, 0)),
            pl.BlockSpec((C_HEADS, 1), lambda b, i: (0, 0)),
        ],
        out_specs=pl.BlockSpec((None, C_HEADS, ts), lambda b, i: (b, 0, i)),
        scratch_shapes=[pltpu.VMEM((C_HEADS, V7X_LANES), F32)],
        compiler_params=pltpu.CompilerParams(dimension_semantics=("parallel", "arbitrary")),
        name="fgate_cumsum",
    )(x3, wf_t, bf)


def _rel_bucket_tiles(t):
    q = np.arange(t)[:, None]
    k = np.arange(t)[None, :]
    out = []
    for off in (0, t):
        rel = q - k + off
        n = np.maximum(rel, 0)
        max_exact = REL_BUCKETS // 2
        nf = np.maximum(n, 1).astype(np.float32)
        large = max_exact + (np.log(nf / np.float32(max_exact)) / np.float32(math.log(REL_MAX_DIST / max_exact))
                             * np.float32(REL_BUCKETS - max_exact)).astype(np.int32)
        large = np.minimum(large, REL_BUCKETS - 1)
        bkt = np.where(n < max_exact, n, large)
        out.append(np.where(rel >= 0, bkt, -1))
    return np.stack(out).astype(np.int32)


def _bias_kernel(rel_ref, bkt_ref, o_ref):
    h = pl.program_id(0)
    bkt = bkt_ref[...]
    far = rel_ref[REL_BUCKETS - 1, h]
    acc = jnp.zeros(bkt.shape, F32)
    for b in range(REL_BUCKETS):
        acc = jnp.where(bkt == b, rel_ref[b, h] - far, acc)
    o_ref[...] = jnp.where(bkt < 0, MASKED, acc * LOG2E)


def _bias_tiles(rel_bias, t):
    bkt = jnp.asarray(_rel_bucket_tiles(t))
    return pl.pallas_call(
        _bias_kernel,
        out_shape=jax.ShapeDtypeStruct((A_HEADS, 2, t, t), F32),
        grid=(A_HEADS,),
        in_specs=[
            pl.BlockSpec(memory_space=pltpu.SMEM),
            pl.BlockSpec((2, t, t), lambda h: (0, 0, 0)),
        ],
        out_specs=pl.BlockSpec((None, 2, t, t), lambda h: (h, 0, 0, 0)),
        name="rel_bias_tiles",
    )(rel_bias, bkt)


def _online_softmax_step(s, v, m_sc, l_sc, acc_sc):
    m_prev = m_sc[...]
    m_new = jnp.maximum(m_prev, jnp.max(s, axis=1, keepdims=True))
    alpha = jnp.exp2(m_prev - m_new)
    p = jnp.exp2(s - m_new)
    l_sc[...] = alpha * l_sc[...] + jnp.sum(p, axis=1, keepdims=True)
    acc_sc[...] = alpha * acc_sc[...] + jnp.dot(p.astype(BF16), v, preferred_element_type=F32)
    m_sc[...] = m_new


def _diff_attn_kernel(lam_ref, subln_ref, q_ref, k_ref, v_ref, bias_ref, o_ref,
                      m_sc, l_sc, acc_sc, *, t, lam_init):
    qi = pl.program_id(2)
    m_sc[...] = jnp.full_like(m_sc, MASKED)
    l_sc[...] = jnp.zeros_like(l_sc)
    acc_sc[...] = jnp.zeros_like(acc_sc)

    q = q_ref[...]
    lane = lax.broadcasted_iota(I32, q.shape, 1)
    zero = jnp.zeros_like(q)
    qq = jnp.concatenate([jnp.where(lane < A_HALF, q, zero), jnp.where(lane >= A_HALF, q, zero)], axis=0)

    def tile(kj, bias):
        start = pl.multiple_of(kj * t, t)
        k = k_ref[pl.ds(start, t), :]
        v = v_ref[pl.ds(start, t), :]
        s = lax.dot_general(qq, k, (((1,), (1,)), ((), ())), preferred_element_type=F32)
        if bias is not None:
            s = s + jnp.concatenate([bias, bias], axis=0)
        _online_softmax_step(s, v, m_sc, l_sc, acc_sc)

    def far_body(kj, carry):
        tile(kj, None)
        return carry

    lax.fori_loop(0, jnp.maximum(qi - 1, 0), far_body, 0)

    @pl.when(qi >= 1)
    def _():
        tile(qi - 1, bias_ref[1])

    tile(qi, bias_ref[0])

    lv = lam_ref[...]
    lam = (jnp.exp(jnp.sum(lv[0:1] * lv[1:2], axis=1, keepdims=True))
           - jnp.exp(jnp.sum(lv[2:3] * lv[3:4], axis=1, keepdims=True)) + lam_init)
    acc = acc_sc[...]
    l = l_sc[...]
    o = acc[:t] / l[:t] - lam * (acc[t:] / l[t:])
    ms = jnp.mean(o * o, axis=1, keepdims=True)
    y = o * lax.rsqrt(ms + LN_EPS) * subln_ref[...] * (1.0 - lam_init)
    o_ref[...] = y.astype(o_ref.dtype)


def _diff_attention(att3, lam4, subln, bias, *, t, lam_init):
    bsz, seq, _ = att3.shape
    nq = seq // t
    return pl.pallas_call(
        functools.partial(_diff_attn_kernel, t=t, lam_init=lam_init),
        out_shape=jax.ShapeDtypeStruct((bsz, seq, A_HEADS * A_VDIM), BF16),
        grid=(bsz, A_HEADS, nq),
        in_specs=[
            pl.BlockSpec((4, A_HALF), lambda b, h, i: (0, 0)),
            pl.BlockSpec((1, A_VDIM), lambda b, h, i: (0, 0)),
            pl.BlockSpec((None, t, A_VDIM), lambda b, h, i: (b, i, h)),
            pl.BlockSpec((None, seq, A_VDIM), lambda b, h, i: (b, 0, A_HEADS + h)),
            pl.BlockSpec((None, seq, A_VDIM), lambda b, h, i: (b, 0, 2 * A_HEADS + h)),
            pl.BlockSpec((None, 2, t, t), lambda b, h, i: (h, 0, 0, 0)),
        ],
        out_specs=pl.BlockSpec((None, t, A_VDIM), lambda b, h, i: (b, i, h)),
        scratch_shapes=[
            pltpu.VMEM((2 * t, 1), F32),
            pltpu.VMEM((2 * t, 1), F32),
            pltpu.VMEM((2 * t, A_VDIM), F32),
        ],
        compiler_params=pltpu.CompilerParams(
            dimension_semantics=("parallel", "parallel", "arbitrary"),
            vmem_limit_bytes=_vmem_limit(2 * _nbytes((seq, A_VDIM), BF16), _nbytes((2, t, t), F32),
                                         scratch_bytes=3 * _nbytes((2 * t, V7X_LANES), F32)),
        ),
        name="diff_attention",
    )(lam4, subln, att3, att3, att3, bias)


def _fox_attn_kernel(q_ref, k_ref, v_ref, ck_ref, o_ref, m_sc, l_sc, acc_sc, *, t):
    qi = pl.program_id(2)
    m_sc[...] = jnp.full_like(m_sc, MASKED)
    l_sc[...] = jnp.zeros_like(l_sc)
    acc_sc[...] = jnp.zeros_like(acc_sc)
    q = q_ref[...]

    def tile(kj, diagonal):
        start = pl.multiple_of(kj * t, t)
        k = k_ref[pl.ds(start, t), :]
        v = v_ref[pl.ds(start, t), :]
        s = lax.dot_general(q, k, (((1,), (1,)), ((), ())), preferred_element_type=F32)
        s = s - ck_ref[pl.ds(kj, 1), :]
        if diagonal:
            row = lax.broadcasted_iota(I32, s.shape, 0)
            col = lax.broadcasted_iota(I32, s.shape, 1)
            s = jnp.where(row >= col, s, MASKED)
        _online_softmax_step(s, v, m_sc, l_sc, acc_sc)

    def body(kj, carry):
        tile(kj, False)
        return carry

    lax.fori_loop(0, qi, body, 0)
    tile(qi, True)
    o_ref[...] = (acc_sc[...] / l_sc[...]).astype(o_ref.dtype)


def _fox_attention(att3, ck4, *, t):
    bsz, seq, _ = att3.shape
    nq = seq // t
    base = 3 * A_HEADS
    return pl.pallas_call(
        functools.partial(_fox_attn_kernel, t=t),
        out_shape=jax.ShapeDtypeStruct((bsz, seq, C_HEADS * C_HDIM), BF16),
        grid=(bsz, C_HEADS, nq),
        in_specs=[
            pl.BlockSpec((None, t, C_HDIM), lambda b, h, i: (b, i, base + h)),
            pl.BlockSpec((None, seq, C_HDIM), lambda b, h, i: (b, 0, base + C_HEADS + h)),
            pl.BlockSpec((None, seq, C_HDIM), lambda b, h, i: (b, 0, base + 2 * C_HEADS + h)),
            pl.BlockSpec((None, None, nq, t), lambda b, h, i: (b, h, 0, 0)),
        ],
        out_specs=pl.BlockSpec((None, t, C_HDIM), lambda b, h, i: (b, i, h)),
        scratch_shapes=[
            pltpu.VMEM((t, 1), F32),
            pltpu.VMEM((t, 1), F32),
            pltpu.VMEM((t, C_HDIM), F32),
        ],
        compiler_params=pltpu.CompilerParams(
            dimension_semantics=("parallel", "parallel", "arbitrary"),
            vmem_limit_bytes=_vmem_limit(2 * _nbytes((seq, C_HDIM), BF16),
                                         scratch_bytes=3 * _nbytes((t, V7X_LANES), F32)),
        ),
        name="fox_attention",
    )(att3, att3, att3, ck4)


def _lru_kernel(bx_ref, bg_ref, cw_ref, cb_ref, wax_ref, bax_ref, lam_ref, o_ref, xbuf, hprev, *, t):
    pad = V7X_SUBLANES

    @pl.when(pl.program_id(1) == 0)
    def _():
        xbuf[0:pad, :] = jnp.zeros((pad, xbuf.shape[1]), F32)
        hprev[...] = jnp.zeros_like(hprev)

    xbuf[pad:pad + t, :] = bx_ref[...]
    xc = cb_ref[...]
    for tap in range(CONV_WIDTH):
        xc = xc + xbuf[pl.ds(pad - (CONV_WIDTH - 1) + tap, t), :] * cw_ref[tap:tap + 1, :]
    xbuf[0:pad, :] = bx_ref[t - pad:t, :]

    bw = xc.shape[1] // LRU_BLOCKS
    row = lax.broadcasted_iota(I32, (t, bw), 0)
    for g in range(LRU_BLOCKS):
        cols = slice(g * bw, (g + 1) * bw)
        xg = xc[:, cols]
        z = jnp.dot(xg.astype(BF16), wax_ref[g], preferred_element_type=F32) + bax_ref[g]
        r = jax.nn.sigmoid(z[:, :bw])
        gi = jax.nn.sigmoid(z[:, bw:])
        log_a = (LRU_C * r) * _log_sigmoid(lam_ref[:, cols])
        a = jnp.exp(log_a)
        u = jnp.sqrt(1.0 - jnp.exp(2.0 * log_a)) * (gi * xg)
        d = 1
        while d < t:
            keep = row >= d
            a_sh = jnp.where(keep, pltpu.roll(a, d, 0), 1.0)
            u_sh = jnp.where(keep, pltpu.roll(u, d, 0), 0.0)
            u = a * u_sh + u
            a = a * a_sh
            d *= 2
        h = a * hprev[0:1, cols] + u
        hprev[0:1, cols] = h[t - 1:t, :]
        o_ref[:, cols] = (jax.nn.gelu(bg_ref[:, cols], approximate=True) * h).astype(o_ref.dtype)


def _lru_branch(bxg, conv_w, conv_b, wax, bax, lam, *, bsz, t):
    n, two_w = bxg.shape
    w = two_w // 2
    nt = n // bsz // t
    bw = w // LRU_BLOCKS
    return pl.pallas_call(
        functools.partial(_lru_kernel, t=t),
        out_shape=jax.ShapeDtypeStruct((n, w), BF16),
        grid=(bsz, nt),
        in_specs=[
            pl.BlockSpec((t, w), lambda b, i: (b * nt + i, 0)),
            pl.BlockSpec((t, w), lambda b, i: (b * nt + i, 1)),
            pl.BlockSpec((CONV_WIDTH, w), lambda b, i: (0, 0)),
            pl.BlockSpec((1, w), lambda b, i: (0, 0)),
            pl.BlockSpec((LRU_BLOCKS, bw, 2 * bw), lambda b, i: (0, 0, 0)),
            pl.BlockSpec((LRU_BLOCKS, 1, 2 * bw), lambda b, i: (0, 0, 0)),
            pl.BlockSpec((1, w), lambda b, i: (0, 0)),
        ],
        out_specs=pl.BlockSpec((t, w), lambda b, i: (b * nt + i, 0)),
        scratch_shapes=[
            pltpu.VMEM((t + V7X_SUBLANES, w), F32),
            pltpu.VMEM((V7X_SUBLANES, w), F32),
        ],
        compiler_params=pltpu.CompilerParams(dimension_semantics=("parallel", "arbitrary")),
        name="conv_rglru",
    )(bxg, bxg, conv_w, conv_b, wax, bax, lam)


def _merge_kernel(ya_ref, yb_ref, yc_ref, w_ref, g0_ref, g1_ref, g2_ref, o_ref):
    acc = g0_ref[...].astype(F32) * jnp.dot(ya_ref[...], w_ref[0], preferred_element_type=F32)
    acc = acc + g1_ref[...].astype(F32) * jnp.dot(yb_ref[...], w_ref[1], preferred_element_type=F32)
    acc = acc + g2_ref[...].astype(F32) * jnp.dot(yc_ref[...], w_ref[2], preferred_element_type=F32)
    o_ref[...] = acc.astype(o_ref.dtype)


def _merge(ya, yb, yc, wb, gates, *, tm, tn):
    n, bwid = ya.shape
    dm = wb.shape[2]
    nc = dm // tn
    y_spec = pl.BlockSpec((tm, bwid), lambda j, i: (i, 0))
    return pl.pallas_call(
        _merge_kernel,
        out_shape=jax.ShapeDtypeStruct((n, dm), BF16),
        grid=(nc, n // tm),
        in_specs=[
            y_spec, y_spec, y_spec,
            pl.BlockSpec((N_BRANCH, bwid, tn), lambda j, i: (0, 0, j)),
            pl.BlockSpec((tm, tn), lambda j, i: (i, j)),
            pl.BlockSpec((tm, tn), lambda j, i: (i, nc + j)),
            pl.BlockSpec((tm, tn), lambda j, i: (i, 2 * nc + j)),
        ],
        out_specs=pl.BlockSpec((tm, tn), lambda j, i: (i, j)),
        compiler_params=pltpu.CompilerParams(
            dimension_semantics=("parallel", "parallel"),
            vmem_limit_bytes=_vmem_limit(3 * _nbytes((tm, bwid), BF16), _nbytes((N_BRANCH, bwid, tn), BF16),
                                         4 * _nbytes((tm, tn), BF16)),
        ),
        name="gated_merge",
    )(ya, yb, yc, wb, gates, gates, gates)


def _wo_ln_kernel(m_ref, w_ref, x_ref, g_ref, b_ref, o32_ref, o16_ref, *, alpha):
    y = jnp.dot(m_ref[...], w_ref[...], preferred_element_type=F32) + alpha * x_ref[...]
    out = _layer_norm_rows(y, g_ref[...], b_ref[...])
    o32_ref[...] = out
    o16_ref[...] = out.astype(BF16)


def _wo_ln(merged, wo, x, g, b, *, alpha, tm):
    n, dm = x.shape
    row = pl.BlockSpec((tm, dm), lambda i: (i, 0))
    vec = pl.BlockSpec((1, dm), lambda i: (0, 0))
    return pl.pallas_call(
        functools.partial(_wo_ln_kernel, alpha=alpha),
        out_shape=(jax.ShapeDtypeStruct((n, dm), F32), jax.ShapeDtypeStruct((n, dm), BF16)),
        grid=(n // tm,),
        in_specs=[row, pl.BlockSpec((dm, dm), lambda i: (0, 0)), row, vec, vec],
        out_specs=(row, row),
        compiler_params=pltpu.CompilerParams(
            dimension_semantics=("parallel",),
            vmem_limit_bytes=_vmem_limit(_nbytes((tm, dm), BF16), _nbytes((dm, dm), BF16),
                                         2 * _nbytes((tm, dm), F32), _nbytes((tm, dm), BF16)),
        ),
        name="wo_layernorm",
    )(merged, wo, x, g, b)


def _ple_kernel(x_ref, wg_ref, bg_ref, p_ref, we_ref, o_ref):
    gate = jax.nn.sigmoid(jnp.dot(x_ref[...], wg_ref[...], preferred_element_type=F32) + bg_ref[...])
    o_ref[...] = gate * jnp.dot(p_ref[...], we_ref[...], preferred_element_type=F32)


def _ple(x1b, wg, bg, pb, we, *, tm, tn):
    n, dm = x1b.shape
    pdim = pb.shape[1]
    return pl.pallas_call(
        _ple_kernel,
        out_shape=jax.ShapeDtypeStruct((n, dm), F32),
        grid=(dm // tn, n // tm),
        in_specs=[
            pl.BlockSpec((tm, dm), lambda j, i: (i, 0)),
            pl.BlockSpec((dm, tn), lambda j, i: (0, j)),
            pl.BlockSpec((1, tn), lambda j, i: (0, j)),
            pl.BlockSpec((tm, pdim), lambda j, i: (i, 0)),
            pl.BlockSpec((pdim, tn), lambda j, i: (0, j)),
        ],
        out_specs=pl.BlockSpec((tm, tn), lambda j, i: (i, j)),
        compiler_params=pltpu.CompilerParams(
            dimension_semantics=("parallel", "parallel"),
            vmem_limit_bytes=_vmem_limit(_nbytes((tm, dm), BF16), _nbytes((dm, tn), BF16),
                                         _nbytes((tm, tn), F32)),
        ),
        name="gated_ple",
    )(x1b, wg, bg, pb, we)


def _top_rows(s, payload, k):
    rows = s.shape[0]
    rid = lax.broadcasted_iota(I32, s.shape, 0)
    vals, pays = [], []
    for _ in range(k):
        m = jnp.max(s, axis=0, keepdims=True)
        pos = jnp.min(jnp.where(s == m, rid, rows), axis=0, keepdims=True)
        hit = rid == pos
        vals.append(m)
        pays.append(jnp.max(jnp.where(hit, payload, -1), axis=0, keepdims=True))
        s = jnp.where(hit, -jnp.inf, s)
    return jnp.concatenate(vals, axis=0), jnp.concatenate(pays, axis=0)


def _peer_route_kernel(x_ref, wq_ref, keys_ref, idx_ref, g_ref):
    q = jnp.dot(x_ref[...], wq_ref[...], preferred_element_type=F32).astype(BF16)
    st = lax.dot_general(keys_ref[...], q, (((1,), (1,)), ((), ())), preferred_element_type=F32)
    kid = lax.broadcasted_iota(I32, (PEER_NKEYS, st.shape[1]), 0)
    s1, i1 = _top_rows(st[:PEER_NKEYS], kid, PEER_TOPK)
    s2, i2 = _top_rows(st[PEER_NKEYS:], kid, PEER_TOPK)
    cand = jnp.concatenate([s1[a:a + 1] + s2 for a in range(PEER_TOPK)], axis=0)
    cidx = jnp.concatenate([i1[a:a + 1] * PEER_NKEYS + i2 for a in range(PEER_TOPK)], axis=0)
    sc, idx = _top_rows(cand, cidx, PEER_TOPK)
    e = jnp.exp(sc - jnp.max(sc, axis=0, keepdims=True))
    idx_ref[...] = idx
    g_ref[...] = e / jnp.sum(e, axis=0, keepdims=True)


def _peer_route(x1b, wq, keys_t, *, tm):
    n, dm = x1b.shape
    qd = 2 * PEER_HALF
    slots = PEER_HEADS * PEER_TOPK
    out_spec = pl.BlockSpec((PEER_TOPK, tm), lambda i, h: (h, i))
    return pl.pallas_call(
        _peer_route_kernel,
        out_shape=(jax.ShapeDtypeStruct((slots, n), I32), jax.ShapeDtypeStruct((slots, n), F32)),
        grid=(n // tm, PEER_HEADS),
        in_specs=[
            pl.BlockSpec((tm, dm), lambda i, h: (i, 0)),
            pl.BlockSpec((dm, qd), lambda i, h: (0, h)),
            pl.BlockSpec((None, 2 * PEER_NKEYS, qd), lambda i, h: (h, 0, 0)),
        ],
        out_specs=(out_spec, out_spec),
        compiler_params=pltpu.CompilerParams(dimension_semantics=("parallel", "parallel")),
        name="peer_route",
    )(x1b, wq, keys_t)


W_ROW_PITCH = PEER_NKEYS + V7X_SUBLANES


def _peer_gate_kernel(idx_ref, g_ref, o_ref, wbuf, idx_sc, g_sc, *, tb):
    idx_sc[...] = idx_ref[...].T
    g_sc[...] = g_ref[...].T
    rid = lax.broadcasted_iota(I32, (PEER_NKEYS, PEER_NKEYS), 0)

    def body(tok, carry):
        ir = idx_sc[pl.ds(tok, 1), :]
        gr = g_sc[pl.ds(tok, 1), :]
        r1 = jnp.where(rid == (ir >> 7), gr, 0.0).astype(BF16)
        r2t = jnp.where(rid == (ir & (PEER_NKEYS - 1)), 1.0, 0.0).astype(BF16)
        wt = lax.dot_general(r1, r2t, (((1,), (1,)), ((), ())), preferred_element_type=F32)
        wbuf[pl.ds(pl.multiple_of(tok * W_ROW_PITCH, V7X_SUBLANES), PEER_NKEYS), :] = wt
        return carry

    lax.fori_loop(0, tb, body, 0)
    for i1 in range(PEER_NKEYS):
        o_ref[:, i1 * PEER_NKEYS:(i1 + 1) * PEER_NKEYS] = (
            wbuf[pl.ds(i1, tb, stride=W_ROW_PITCH), :].astype(o_ref.dtype))


def _peer_gate_matrix(idx_t, g_t, *, tb):
    slots, n = idx_t.shape
    ne = PEER_NKEYS * PEER_NKEYS
    in_spec = pl.BlockSpec((slots, tb), lambda i: (0, i))
    return pl.pallas_call(
        functools.partial(_peer_gate_kernel, tb=tb),
        out_shape=jax.ShapeDtypeStruct((n, ne), BF16),
        grid=(n // tb,),
        in_specs=[in_spec, in_spec],
        out_specs=pl.BlockSpec((tb, ne), lambda i: (i, 0)),
        scratch_shapes=[
            pltpu.VMEM((tb * W_ROW_PITCH, PEER_NKEYS), F32),
            pltpu.VMEM((tb, slots), I32),
            pltpu.VMEM((tb, slots), F32),
        ],
        compiler_params=pltpu.CompilerParams(
            dimension_semantics=("parallel",),
            vmem_limit_bytes=_vmem_limit(_nbytes((tb, ne), BF16),
                                         scratch_bytes=_nbytes((tb * W_ROW_PITCH, PEER_NKEYS), F32)),
        ),
        name="peer_gate_matrix",
    )(idx_t, g_t)


def _peer_expert_kernel(x_ref, ut_ref, v_ref, w_ref, o_ref):
    a = jnp.dot(x_ref[...], ut_ref[...], preferred_element_type=F32)
    gelu = 0.5 * a * (1.0 + lax.erf(a * (0.5 ** 0.5)))
    g = (w_ref[...].astype(F32) * gelu).astype(BF16)
    contrib = jnp.dot(g, v_ref[...], preferred_element_type=F32)

    @pl.when(pl.program_id(1) == 0)
    def _():
        o_ref[...] = contrib

    @pl.when(pl.program_id(1) != 0)
    def _():
        o_ref[...] += contrib


def _peer_experts(x1b, ut, v, w, *, tm, ce):
    n, dm = x1b.shape
    ne = v.shape[0]
    return pl.pallas_call(
        _peer_expert_kernel,
        out_shape=jax.ShapeDtypeStruct((n, dm), F32),
        grid=(n // tm, ne // ce),
        in_specs=[
            pl.BlockSpec((tm, dm), lambda i, j: (i, 0)),
            pl.BlockSpec((dm, ce), lambda i, j: (0, j)),
            pl.BlockSpec((ce, dm), lambda i, j: (j, 0)),
            pl.BlockSpec((tm, ce), lambda i, j: (i, j)),
        ],
        out_specs=pl.BlockSpec((tm, dm), lambda i, j: (i, 0)),
        compiler_params=pltpu.CompilerParams(
            dimension_semantics=("parallel", "arbitrary"),
            vmem_limit_bytes=_vmem_limit(_nbytes((tm, dm), BF16), 2 * _nbytes((dm, ce), BF16),
                                         _nbytes((tm, ce), BF16), _nbytes((tm, dm), F32)),
        ),
        name="peer_experts",
    )(x1b, ut, v, w)


def _residual_ln_kernel(x_ref, y_ref, p_ref, g_ref, b_ref, o32_ref, o16_ref, *, alpha):
    out = _layer_norm_rows(alpha * x_ref[...] + y_ref[...] + p_ref[...], g_ref[...], b_ref[...])
    o32_ref[...] = out
    o16_ref[...] = out.astype(BF16)


def _residual_ln(x1, yf, ple, g, b, *, alpha, tm):
    n, dm = x1.shape
    row = pl.BlockSpec((tm, dm), lambda i: (i, 0))
    vec = pl.BlockSpec((1, dm), lambda i: (0, 0))
    return pl.pallas_call(
        functools.partial(_residual_ln_kernel, alpha=alpha),
        out_shape=(jax.ShapeDtypeStruct((n, dm), F32), jax.ShapeDtypeStruct((n, dm), BF16)),
        grid=(n // tm,),
        in_specs=[row, row, row, vec, vec],
        out_specs=(row, row),
        compiler_params=pltpu.CompilerParams(
            dimension_semantics=("parallel",),
            vmem_limit_bytes=_vmem_limit(4 * _nbytes((tm, dm), F32), _nbytes((tm, dm), BF16)),
        ),
        name="residual_layernorm",
    )(x1, yf, ple, g, b)


def _peer_keys_layout(keys):
    z = jnp.zeros_like(keys[:, 0])
    top = jnp.concatenate([keys[:, 0], z], axis=-1)
    bot = jnp.concatenate([z, keys[:, 1]], axis=-1)
    return jnp.concatenate([top, bot], axis=1).astype(BF16)


def kernel(x, p, w_in, b_in, diff_lambda, diff_subln, rel_bias, conv_w, conv_b, lru_wa, lru_ba, lru_wx,
           lru_bx, lru_lambda, w_branch, w_o, ln1_g, ln1_b, peer_wq, peer_keys, peer_u, peer_v, w_ple,
           w_ple_gate, b_ple_gate, ln2_g, ln2_b):
    bsz, seq, dm = x.shape
    depth = w_in.shape[0]
    n = bsz * seq
    alpha = (2 * depth) ** 0.25
    bwid = A_HEADS * A_VDIM
    t_attn = 256
    assert seq % t_attn == 0 and t_attn >= REL_MAX_DIST

    o_lru = 3 * bwid
    o_c = 5 * bwid
    o_f = 8 * bwid
    o_g = o_f + C_HEADS
    att_scale = jnp.concatenate([
        jnp.full((bwid,), A_HALF ** -0.5 * LOG2E, F32), jnp.ones((2 * bwid,), F32),
        jnp.full((bwid,), C_HDIM ** -0.5 * LOG2E, F32), jnp.ones((2 * bwid,), F32)])[None]
    ones_row = jnp.ones((1, max(2 * bwid, N_BRANCH * dm)), F32)

    bias = _bias_tiles(rel_bias, t_attn)
    x2 = x.reshape(n, dm)
    xb = x2.astype(BF16)
    for i in range(depth):
        w = w_in[i]
        b = b_in[i]
        w_att = jnp.concatenate([w[:, :o_lru], w[:, o_c:o_f]], axis=1).astype(BF16)
        b_att = jnp.concatenate([b[:o_lru], b[o_c:o_f]])[None]
        att = _proj(xb, w_att, b_att, att_scale, out_dtype=BF16, tm=1024, tn=2048, name="proj_attn")
        bxg = _proj(xb, w[:, o_lru:o_c].astype(BF16), b[None, o_lru:o_c], ones_row[:, :2 * bwid],
                    out_dtype=F32, tm=1024, tn=1024, name="proj_lru")
        gates = _proj(xb, w[:, o_g:].astype(BF16), b[None, o_g:], ones_row[:, :N_BRANCH * dm],
                      out_dtype=BF16, tm=1024, tn=2048, act="sigmoid", name="proj_gates")
        cum = _fgate_cumsum(xb.reshape(bsz, seq, dm), w[:, o_f:o_g].T.astype(BF16), b[o_f:o_g, None],
                            ts=min(seq, 1024))

        att3 = att.reshape(bsz, seq, 6 * bwid)
        lam_init = 0.8 - 0.6 * math.exp(-0.3 * i)
        ya = _diff_attention(att3, diff_lambda[i], diff_subln[i][None], bias, t=t_attn, lam_init=lam_init)
        yc = _fox_attention(att3, cum.reshape(bsz, C_HEADS, seq // t_attn, t_attn), t=t_attn)

        bw = bwid // LRU_BLOCKS
        wax = jnp.concatenate([lru_wa[i], lru_wx[i]], axis=-1).astype(BF16)
        bax = jnp.concatenate([lru_ba[i].reshape(LRU_BLOCKS, 1, bw), lru_bx[i].reshape(LRU_BLOCKS, 1, bw)], axis=-1)
        yb = _lru_branch(bxg, conv_w[i], conv_b[i][None], wax, bax, lru_lambda[i][None], bsz=bsz, t=256)

        merged = _merge(ya.reshape(n, bwid), yb, yc.reshape(n, bwid), w_branch[i].astype(BF16), gates,
                        tm=512, tn=1024)
        x1, x1b = _wo_ln(merged, w_o[i].astype(BF16), x2, ln1_g[i][None], ln1_b[i][None], alpha=alpha, tm=256)

        ple = _ple(x1b, w_ple_gate[i].astype(BF16), b_ple_gate[i][None], p[i].reshape(n, -1).astype(BF16),
                   w_ple[i].astype(BF16), tm=1024, tn=1024)
        idx_t, g_t = _peer_route(x1b, peer_wq[i].astype(BF16), _peer_keys_layout(peer_keys[i]), tm=256)
        wdense = _peer_gate_matrix(idx_t, g_t, tb=128)
        yf = _peer_experts(x1b, peer_u[i].T.astype(BF16), peer_v[i].astype(BF16), wdense, tm=512, ce=512)
        x2, xb = _residual_ln(x1, yf, ple, ln2_g[i][None], ln2_b[i][None], alpha=alpha, tm=512)
    return x2.reshape(bsz, seq, dm)
```

```python
import functools
import math

import jax
import jax.numpy as jnp
import numpy as np
from jax import lax
from jax.experimental import pallas as pl
from jax.experimental.pallas import tpu as pltpu

F32, BF16, I32 = jnp.float32, jnp.bfloat16, jnp.int32

V7X_VMEM_BYTES = 64 * 2**20
V7X_LANES = 128
V7X_SUBLANES = 8
VMEM_HEADROOM_BYTES = 8 * 2**20

A_HEADS = 8
A_HALF = 64
A_VDIM = 2 * A_HALF
LRU_BLOCKS = 8
CONV_WIDTH = 4
LRU_C = 8.0
C_HEADS = 8
C_HDIM = 128
N_BRANCH = 3
REL_BUCKETS = 32
REL_MAX_DIST = 128
PEER_HEADS = 8
PEER_NKEYS = 128
PEER_HALF = 64
PEER_TOPK = 16
LN_EPS = 1e-5

LOG2E = 1.4426950408889634
MASKED = -1e30


def _vmem_limit(*block_bytes, scratch_bytes=0):
    need = 2 * sum(block_bytes) + scratch_bytes + VMEM_HEADROOM_BYTES
    return int(min(max(need, 32 * 2**20), V7X_VMEM_BYTES - 4 * 2**20))


def _nbytes(shape, dtype):
    return int(np.prod(shape)) * jnp.dtype(dtype).itemsize


def _log_sigmoid(z):
    return jnp.minimum(z, 0.0) - jnp.log1p(jnp.exp(-jnp.abs(z)))


def _layer_norm_rows(y, g, b):
    mu = jnp.mean(y, axis=1, keepdims=True)
    yc = y - mu
    var = jnp.mean(yc * yc, axis=1, keepdims=True)
    return yc * lax.rsqrt(var + LN_EPS) * g + b


def _proj_kernel(x_ref, w_ref, b_ref, s_ref, o_ref, *, act):
    y = jnp.dot(x_ref[...], w_ref[...], preferred_element_type=F32)
    y = (y + b_ref[...]) * s_ref[...]
    if act == "sigmoid":
        y = jax.nn.sigmoid(y)
    o_ref[...] = y.astype(o_ref.dtype)


def _proj(x, w, b, s, *, out_dtype, tm, tn, act=None, name):
    m, k = x.shape
    n = w.shape[1]
    grid = (n // tn, m // tm)
    return pl.pallas_call(
        functools.partial(_proj_kernel, act=act),
        out_shape=jax.ShapeDtypeStruct((m, n), out_dtype),
        grid=grid,
        in_specs=[
            pl.BlockSpec((tm, k), lambda j, i: (i, 0)),
            pl.BlockSpec((k, tn), lambda j, i: (0, j)),
            pl.BlockSpec((1, tn), lambda j, i: (0, j)),
            pl.BlockSpec((1, tn), lambda j, i: (0, j)),
        ],
        out_specs=pl.BlockSpec((tm, tn), lambda j, i: (i, j)),
        compiler_params=pltpu.CompilerParams(
            dimension_semantics=("parallel", "parallel"),
            vmem_limit_bytes=_vmem_limit(_nbytes((tm, k), x.dtype), _nbytes((k, tn), w.dtype),
                                         _nbytes((tm, tn), out_dtype)),
        ),
        name=name,
    )(x, w, b, s)


def _proj_t_kernel(w_ref, x_ref, b_ref, o_ref):
    y = lax.dot_general(w_ref[...], x_ref[...], (((1,), (1,)), ((), ())), preferred_element_type=F32)
    o_ref[...] = (y + b_ref[...]).astype(o_ref.dtype)


def _proj_t(x, w_t, b_col, *, tm, tn, name):
    m, k = x.shape
    n = w_t.shape[0]
    return pl.pallas_call(
        _proj_t_kernel,
        out_shape=jax.ShapeDtypeStruct((m // tm, n, tm), BF16),
        grid=(n // tn, m // tm),
        in_specs=[
            pl.BlockSpec((tn, k), lambda j, i: (j, 0)),
            pl.BlockSpec((tm, k), lambda j, i: (i, 0)),
            pl.BlockSpec((tn, 1), lambda j, i: (j, 0)),
        ],
        out_specs=pl.BlockSpec((None, tn, tm), lambda j, i: (i, j, 0)),
        compiler_params=pltpu.CompilerParams(
            dimension_semantics=("parallel", "parallel"),
            vmem_limit_bytes=_vmem_limit(_nbytes((tm, k), BF16), _nbytes((tn, k), BF16), _nbytes((tn, tm), BF16)),
        ),
        name=name,
    )(w_t, x, b_col)


def _fgate_kernel(x_ref, w_ref, b_ref, o_ref, carry_ref, *, ts):
    @pl.when(pl.program_id(1) == 0)
    def _():
        carry_ref[...] = jnp.zeros_like(carry_ref)

    z = lax.dot_general(w_ref[...], x_ref[...], (((1,), (1,)), ((), ())),
                        preferred_element_type=F32) + b_ref[...]
    c = _log_sigmoid(z)
    lane = lax.broadcasted_iota(I32, c.shape, 1)
    d = 1
    while d < ts:
        c = c + jnp.where(lane >= d, pltpu.roll(c, d, 1), 0.0)
        d *= 2
    c = c + carry_ref[:, 0:1]
    carry_ref[...] = jnp.broadcast_to(c[:, ts - 1:ts], carry_ref.shape)
    c = c * LOG2E
    for h in range(C_HEADS):
        o_ref[h] = jnp.broadcast_to(c[h:h + 1, :], (V7X_LANES, ts)).T


def _fgate_cumsum(x3, wf_t, bf, *, ts):
    bsz, seq, dm = x3.shape
    return pl.pallas_call(
        functools.partial(_fgate_kernel, ts=ts),
        out_shape=jax.ShapeDtypeStruct((bsz, C_HEADS, seq, V7X_LANES), F32),
        grid=(bsz, seq // ts),
        in_specs=[
            pl.BlockSpec((None, ts, dm), lambda b, i: (b, i, 0)),
            pl.BlockSpec((C_HEADS, dm), lambda b, i: (0, 0)),
            pl.BlockSpec((C_HEADS, 1), lambda b, i: (0, 0)),
        ],
        out_specs=pl.BlockSpec((None, C_HEADS, ts, V7X_LANES), lambda b, i: (b, 0, i, 0)),
        scratch_shapes=[pltpu.VMEM((C_HEADS, V7X_LANES), F32)],
        compiler_params=pltpu.CompilerParams(dimension_semantics=("parallel", "arbitrary")),
        name="fgate_cumsum",
    )(x3, wf_t, bf)


def _rel_bucket_tiles(t):
    q = np.arange(t)[None, :]
    k = np.arange(t)[:, None]
    out = []
    for off in (0, t):
        rel = q - k + off
        n = np.maximum(rel, 0)
        max_exact = REL_BUCKETS // 2
        nf = np.maximum(n, 1).astype(np.float32)
        large = max_exact + (np.log(nf / np.float32(max_exact)) / np.float32(math.log(REL_MAX_DIST / max_exact))
                             * np.float32(REL_BUCKETS - max_exact)).astype(np.int32)
        large = np.minimum(large, REL_BUCKETS - 1)
        bkt = np.where(n < max_exact, n, large)
        out.append(np.where(rel >= 0, bkt, -1))
    return np.stack(out).astype(np.int32)


def _bias_kernel(rel_ref, bkt_ref, o_ref):
    h = pl.program_id(0)
    bkt = bkt_ref[...]
    far = rel_ref[REL_BUCKETS - 1, h]
    acc = jnp.zeros(bkt.shape, F32)
    for b in range(REL_BUCKETS):
        acc = jnp.where(bkt == b, rel_ref[b, h] - far, acc)
    o_ref[...] = jnp.where(bkt < 0, MASKED, acc * LOG2E)


def _bias_tiles(rel_bias, t):
    bkt = jnp.asarray(_rel_bucket_tiles(t))
    return pl.pallas_call(
        _bias_kernel,
        out_shape=jax.ShapeDtypeStruct((A_HEADS, 2, t, t), F32),
        grid=(A_HEADS,),
        in_specs=[
            pl.BlockSpec(memory_space=pltpu.SMEM),
            pl.BlockSpec((2, t, t), lambda h: (0, 0, 0)),
        ],
        out_specs=pl.BlockSpec((None, 2, t, t), lambda h: (h, 0, 0, 0)),
        name="rel_bias_tiles",
    )(rel_bias, bkt)


HEADS_PER_STEP = 2


def _sublane_allreduce(x, op):
    for shift in (1, 2, 4):
        x = op(x, pltpu.roll(x, shift, 0))
    return x


def _rows3(x):
    return x.reshape(x.shape[0] // V7X_SUBLANES, V7X_SUBLANES, x.shape[1])


def _attn_tile(s, vt, h, m_sc, l_sc, acc_sc):
    s3 = _rows3(s)
    m_prev = m_sc[h]
    m_new = jnp.maximum(m_prev, _sublane_allreduce(jnp.max(s3, axis=0), jnp.maximum))
    alpha = jnp.exp2(m_prev - m_new)
    p3 = jnp.exp2(s3 - m_new[None])
    l_sc[h] = alpha * l_sc[h] + jnp.sum(p3, axis=0)
    pv = jnp.dot(vt, p3.reshape(s.shape).astype(BF16), preferred_element_type=F32)
    acc_sc[h] = (_rows3(acc_sc[h]) * alpha[None]).reshape(pv.shape) + pv
    m_sc[h] = m_new


def _diff_attn_kernel(lam_ref, subln_ref, q_ref, k_ref, vt_ref, bias_ref, o_ref,
                      m_sc, l_sc, acc_sc, *, t, lam_init):
    qi = pl.program_id(2)
    m_sc[...] = jnp.full_like(m_sc, MASKED)
    l_sc[...] = jnp.zeros_like(l_sc)
    acc_sc[...] = jnp.zeros_like(acc_sc)

    qqs = []
    for h in range(HEADS_PER_STEP):
        q = q_ref[:, h * A_VDIM:(h + 1) * A_VDIM]
        lane = lax.broadcasted_iota(I32, q.shape, 1)
        zero = jnp.zeros_like(q)
        qqs.append(jnp.concatenate([jnp.where(lane < A_HALF, q, zero), jnp.where(lane >= A_HALF, q, zero)], axis=0))

    def tile(kj, near):
        start = pl.multiple_of(kj * t, t)
        for h in range(HEADS_PER_STEP):
            cols = slice(h * A_VDIM, (h + 1) * A_VDIM)
            s = lax.dot_general(k_ref[pl.ds(start, t), cols], qqs[h], (((1,), (1,)), ((), ())),
                                preferred_element_type=F32)
            if near is not None:
                bias = bias_ref[h, near]
                s = s + jnp.concatenate([bias, bias], axis=1)
            _attn_tile(s, vt_ref[kj, cols, :], h, m_sc, l_sc, acc_sc)

    def far_body(kj, carry):
        tile(kj, None)
        return carry

    lax.fori_loop(0, jnp.maximum(qi - 1, 0), far_body, 0)

    @pl.when(qi >= 1)
    def _():
        tile(qi - 1, 1)

    tile(qi, 0)

    lv = lam_ref[...]
    lam = (jnp.exp(jnp.sum(lv[0:1] * lv[1:2], axis=1, keepdims=True))
           - jnp.exp(jnp.sum(lv[2:3] * lv[3:4], axis=1, keepdims=True)) + lam_init)
    for h in range(HEADS_PER_STEP):
        l = _sublane_allreduce(l_sc[h], jnp.add)
        r = (_rows3(acc_sc[h]) / l[None]).reshape(A_VDIM, 2 * t)
        o = r[:, :t] - lam * r[:, t:]
        ms = _sublane_allreduce(jnp.sum(_rows3(o * o), axis=0), jnp.add) * (1.0 / A_VDIM)
        y = (_rows3(o) * lax.rsqrt(ms + LN_EPS)[None]).reshape(A_VDIM, t) * (subln_ref[...] * (1.0 - lam_init))
        o_ref[:, h * A_VDIM:(h + 1) * A_VDIM] = y.T.astype(o_ref.dtype)


def _diff_attention(qk3, vt4, lam4, subln_col, bias, *, t, lam_init):
    bsz, seq, _ = qk3.shape
    nq = seq // t
    hp = HEADS_PER_STEP
    wid = hp * A_VDIM
    kblk = A_HEADS // hp
    return pl.pallas_call(
        functools.partial(_diff_attn_kernel, t=t, lam_init=lam_init),
        out_shape=jax.ShapeDtypeStruct((bsz, seq, A_HEADS * A_VDIM), BF16),
        grid=(bsz, A_HEADS // hp, nq),
        in_specs=[
            pl.BlockSpec((4, A_HALF), lambda b, h, i: (0, 0)),
            pl.BlockSpec((A_VDIM, 1), lambda b, h, i: (0, 0)),
            pl.BlockSpec((None, t, wid), lambda b, h, i: (b, i, h)),
            pl.BlockSpec((None, seq, wid), lambda b, h, i: (b, 0, kblk + h)),
            pl.BlockSpec((None, nq, wid, t), lambda b, h, i: (b, 0, h, 0)),
            pl.BlockSpec((hp, 2, t, t), lambda b, h, i: (h, 0, 0, 0)),
        ],
        out_specs=pl.BlockSpec((None, t, wid), lambda b, h, i: (b, i, h)),
        scratch_shapes=[
            pltpu.VMEM((hp, V7X_SUBLANES, 2 * t), F32),
            pltpu.VMEM((hp, V7X_SUBLANES, 2 * t), F32),
            pltpu.VMEM((hp, A_VDIM, 2 * t), F32),
        ],
        compiler_params=pltpu.CompilerParams(
            dimension_semantics=("parallel", "parallel", "arbitrary"),
            vmem_limit_bytes=_vmem_limit(2 * _nbytes((seq, wid), BF16), _nbytes((hp, 2, t, t), F32),
                                         scratch_bytes=_nbytes((hp, A_VDIM + 2 * t, 2 * t), F32)),
        ),
        name="diff_attention",
    )(lam4, subln_col, qk3, qk3, vt4, bias)


def _fox_attn_kernel(q_ref, k_ref, vt_ref, ck_ref, o_ref, m_sc, l_sc, acc_sc, *, t):
    qi = pl.program_id(2)
    m_sc[...] = jnp.full_like(m_sc, MASKED)
    l_sc[...] = jnp.zeros_like(l_sc)
    acc_sc[...] = jnp.zeros_like(acc_sc)
    qs = [q_ref[:, h * C_HDIM:(h + 1) * C_HDIM] for h in range(HEADS_PER_STEP)]

    def tile(kj, diagonal):
        start = pl.multiple_of(kj * t, t)
        for h in range(HEADS_PER_STEP):
            cols = slice(h * C_HDIM, (h + 1) * C_HDIM)
            s = lax.dot_general(k_ref[pl.ds(start, t), cols], qs[h], (((1,), (1,)), ((), ())),
                                preferred_element_type=F32)
            ck = ck_ref[h, pl.ds(start, t), :]
            s = s - jnp.concatenate([ck] * (t // V7X_LANES), axis=1)
            if diagonal:
                key = lax.broadcasted_iota(I32, s.shape, 0)
                qry = lax.broadcasted_iota(I32, s.shape, 1)
                s = jnp.where(key <= qry, s, MASKED)
            _attn_tile(s, vt_ref[kj, cols, :], h, m_sc, l_sc, acc_sc)

    def body(kj, carry):
        tile(kj, False)
        return carry

    lax.fori_loop(0, qi, body, 0)
    tile(qi, True)
    for h in range(HEADS_PER_STEP):
        l = _sublane_allreduce(l_sc[h], jnp.add)
        o = (_rows3(acc_sc[h]) / l[None]).reshape(C_HDIM, t)
        o_ref[:, h * C_HDIM:(h + 1) * C_HDIM] = o.T.astype(o_ref.dtype)


def _fox_attention(qk3, vt4, ckb, *, t):
    bsz, seq, _ = qk3.shape
    nq = seq // t
    hp = HEADS_PER_STEP
    wid = hp * C_HDIM
    qblk = 2 * (A_HEADS // hp)
    kblk = qblk + C_HEADS // hp
    vblk = A_HEADS // hp
    return pl.pallas_call(
        functools.partial(_fox_attn_kernel, t=t),
        out_shape=jax.ShapeDtypeStruct((bsz, seq, C_HEADS * C_HDIM), BF16),
        grid=(bsz, C_HEADS // hp, nq),
        in_specs=[
            pl.BlockSpec((None, t, wid), lambda b, h, i: (b, i, qblk + h)),
            pl.BlockSpec((None, seq, wid), lambda b, h, i: (b, 0, kblk + h)),
            pl.BlockSpec((None, nq, wid, t), lambda b, h, i: (b, 0, vblk + h, 0)),
            pl.BlockSpec((None, hp, seq, V7X_LANES), lambda b, h, i: (b, h, 0, 0)),
        ],
        out_specs=pl.BlockSpec((None, t, wid), lambda b, h, i: (b, i, h)),
        scratch_shapes=[
            pltpu.VMEM((hp, V7X_SUBLANES, t), F32),
            pltpu.VMEM((hp, V7X_SUBLANES, t), F32),
            pltpu.VMEM((hp, C_HDIM, t), F32),
        ],
        compiler_params=pltpu.CompilerParams(
            dimension_semantics=("parallel", "parallel", "arbitrary"),
            vmem_limit_bytes=_vmem_limit(2 * _nbytes((seq, wid), BF16), _nbytes((hp, seq, V7X_LANES), F32),
                                         scratch_bytes=_nbytes((hp, C_HDIM + 2 * t, t), F32)),
        ),
        name="fox_attention",
    )(qk3, qk3, vt4, ckb)


def _lru_kernel(bx_ref, bg_ref, cw_ref, cb_ref, wax_ref, bax_ref, lam_ref, o_ref, xbuf, hprev, *, t):
    pad = V7X_SUBLANES

    @pl.when(pl.program_id(1) == 0)
    def _():
        xbuf[0:pad, :] = jnp.zeros((pad, xbuf.shape[1]), F32)
        hprev[...] = jnp.zeros_like(hprev)

    xbuf[pad:pad + t, :] = bx_ref[...]
    xc = cb_ref[...]
    for tap in range(CONV_WIDTH):
        xc = xc + xbuf[pl.ds(pad - (CONV_WIDTH - 1) + tap, t), :] * cw_ref[tap:tap + 1, :]
    xbuf[0:pad, :] = bx_ref[t - pad:t, :]

    bw = xc.shape[1] // LRU_BLOCKS
    row = lax.broadcasted_iota(I32, (t, bw), 0)
    for g in range(LRU_BLOCKS):
        cols = slice(g * bw, (g + 1) * bw)
        xg = xc[:, cols]
        z = jnp.dot(xg.astype(BF16), wax_ref[g], preferred_element_type=F32) + bax_ref[g]
        r = jax.nn.sigmoid(z[:, :bw])
        gi = jax.nn.sigmoid(z[:, bw:])
        log_a = (LRU_C * r) * _log_sigmoid(lam_ref[:, cols])
        a = jnp.exp(log_a)
        u = jnp.sqrt(1.0 - jnp.exp(2.0 * log_a)) * (gi * xg)
        d = 1
        while d < t:
            keep = row >= d
            a_sh = jnp.where(keep, pltpu.roll(a, d, 0), 1.0)
            u_sh = jnp.where(keep, pltpu.roll(u, d, 0), 0.0)
            u = a * u_sh + u
            a = a * a_sh
            d *= 2
        h = a * hprev[0:1, cols] + u
        hprev[0:1, cols] = h[t - 1:t, :]
        o_ref[:, cols] = (jax.nn.gelu(bg_ref[:, cols], approximate=True) * h).astype(o_ref.dtype)


def _lru_branch(bxg, conv_w, conv_b, wax, bax, lam, *, bsz, t):
    n, two_w = bxg.shape
    w = two_w // 2
    nt = n // bsz // t
    bw = w // LRU_BLOCKS
    return pl.pallas_call(
        functools.partial(_lru_kernel, t=t),
        out_shape=jax.ShapeDtypeStruct((n, w), BF16),
        grid=(bsz, nt),
        in_specs=[
            pl.BlockSpec((t, w), lambda b, i: (b * nt + i, 0)),
            pl.BlockSpec((t, w), lambda b, i: (b * nt + i, 1)),
            pl.BlockSpec((CONV_WIDTH, w), lambda b, i: (0, 0)),
            pl.BlockSpec((1, w), lambda b, i: (0, 0)),
            pl.BlockSpec((LRU_BLOCKS, bw, 2 * bw), lambda b, i: (0, 0, 0)),
            pl.BlockSpec((LRU_BLOCKS, 1, 2 * bw), lambda b, i: (0, 0, 0)),
            pl.BlockSpec((1, w), lambda b, i: (0, 0)),
        ],
        out_specs=pl.BlockSpec((t, w), lambda b, i: (b * nt + i, 0)),
        scratch_shapes=[
            pltpu.VMEM((t + V7X_SUBLANES, w), F32),
            pltpu.VMEM((V7X_SUBLANES, w), F32),
        ],
        compiler_params=pltpu.CompilerParams(dimension_semantics=("parallel", "arbitrary")),
        name="conv_rglru",
    )(bxg, bxg, conv_w, conv_b, wax, bax, lam)


def _merge_kernel(ya_ref, yb_ref, yc_ref, w_ref, g0_ref, g1_ref, g2_ref, o_ref):
    acc = g0_ref[...].astype(F32) * jnp.dot(ya_ref[...], w_ref[0], preferred_element_type=F32)
    acc = acc + g1_ref[...].astype(F32) * jnp.dot(yb_ref[...], w_ref[1], preferred_element_type=F32)
    acc = acc + g2_ref[...].astype(F32) * jnp.dot(yc_ref[...], w_ref[2], preferred_element_type=F32)
    o_ref[...] = acc.astype(o_ref.dtype)


def _merge(ya, yb, yc, wb, gates, *, tm, tn):
    n, bwid = ya.shape
    dm = wb.shape[2]
    nc = dm // tn
    y_spec = pl.BlockSpec((tm, bwid), lambda j, i: (i, 0))
    return pl.pallas_call(
        _merge_kernel,
        out_shape=jax.ShapeDtypeStruct((n, dm), BF16),
        grid=(nc, n // tm),
        in_specs=[
            y_spec, y_spec, y_spec,
            pl.BlockSpec((N_BRANCH, bwid, tn), lambda j, i: (0, 0, j)),
            pl.BlockSpec((tm, tn), lambda j, i: (i, j)),
            pl.BlockSpec((tm, tn), lambda j, i: (i, nc + j)),
            pl.BlockSpec((tm, tn), lambda j, i: (i, 2 * nc + j)),
        ],
        out_specs=pl.BlockSpec((tm, tn), lambda j, i: (i, j)),
        compiler_params=pltpu.CompilerParams(
            dimension_semantics=("parallel", "parallel"),
            vmem_limit_bytes=_vmem_limit(3 * _nbytes((tm, bwid), BF16), _nbytes((N_BRANCH, bwid, tn), BF16),
                                         4 * _nbytes((tm, tn), BF16)),
        ),
        name="gated_merge",
    )(ya, yb, yc, wb, gates, gates, gates)


def _wo_ln_kernel(m_ref, w_ref, x_ref, g_ref, b_ref, o32_ref, o16_ref, *, alpha):
    y = jnp.dot(m_ref[...], w_ref[...], preferred_element_type=F32) + alpha * x_ref[...]
    out = _layer_norm_rows(y, g_ref[...], b_ref[...])
    o32_ref[...] = out
    o16_ref[...] = out.astype(BF16)


def _wo_ln(merged, wo, x, g, b, *, alpha, tm):
    n, dm = x.shape
    row = pl.BlockSpec((tm, dm), lambda i: (i, 0))
    vec = pl.BlockSpec((1, dm), lambda i: (0, 0))
    return pl.pallas_call(
        functools.partial(_wo_ln_kernel, alpha=alpha),
        out_shape=(jax.ShapeDtypeStruct((n, dm), F32), jax.ShapeDtypeStruct((n, dm), BF16)),
        grid=(n // tm,),
        in_specs=[row, pl.BlockSpec((dm, dm), lambda i: (0, 0)), row, vec, vec],
        out_specs=(row, row),
        compiler_params=pltpu.CompilerParams(
            dimension_semantics=("parallel",),
            vmem_limit_bytes=_vmem_limit(_nbytes((tm, dm), BF16), _nbytes((dm, dm), BF16),
                                         2 * _nbytes((tm, dm), F32), _nbytes((tm, dm), BF16)),
        ),
        name="wo_layernorm",
    )(merged, wo, x, g, b)


def _ple_kernel(x_ref, wg_ref, bg_ref, p_ref, we_ref, o_ref):
    gate = jax.nn.sigmoid(jnp.dot(x_ref[...], wg_ref[...], preferred_element_type=F32) + bg_ref[...])
    o_ref[...] = gate * jnp.dot(p_ref[...], we_ref[...], preferred_element_type=F32)


def _ple(x1b, wg, bg, pb, we, *, tm, tn):
    n, dm = x1b.shape
    pdim = pb.shape[1]
    return pl.pallas_call(
        _ple_kernel,
        out_shape=jax.ShapeDtypeStruct((n, dm), F32),
        grid=(dm // tn, n // tm),
        in_specs=[
            pl.BlockSpec((tm, dm), lambda j, i: (i, 0)),
            pl.BlockSpec((dm, tn), lambda j, i: (0, j)),
            pl.BlockSpec((1, tn), lambda j, i: (0, j)),
            pl.BlockSpec((tm, pdim), lambda j, i: (i, 0)),
            pl.BlockSpec((pdim, tn), lambda j, i: (0, j)),
        ],
        out_specs=pl.BlockSpec((tm, tn), lambda j, i: (i, j)),
        compiler_params=pltpu.CompilerParams(
            dimension_semantics=("parallel", "parallel"),
            vmem_limit_bytes=_vmem_limit(_nbytes((tm, dm), BF16), _nbytes((dm, tn), BF16),
                                         _nbytes((tm, tn), F32)),
        ),
        name="gated_ple",
    )(x1b, wg, bg, pb, we)


def _top_rows(s, payload, k):
    rows = s.shape[0]
    rid = lax.broadcasted_iota(I32, s.shape, 0)
    vals, pays = [], []
    for _ in range(k):
        m = jnp.max(s, axis=0, keepdims=True)
        pos = jnp.min(jnp.where(s == m, rid, rows), axis=0, keepdims=True)
        hit = rid == pos
        vals.append(m)
        pays.append(jnp.max(jnp.where(hit, payload, -1), axis=0, keepdims=True))
        s = jnp.where(hit, -jnp.inf, s)
    return jnp.concatenate(vals, axis=0), jnp.concatenate(pays, axis=0)


def _peer_route_kernel(x_ref, wq_ref, keys_ref, idx_ref, g_ref):
    q = jnp.dot(x_ref[...], wq_ref[...], preferred_element_type=F32).astype(BF16)
    st = lax.dot_general(keys_ref[...], q, (((1,), (1,)), ((), ())), preferred_element_type=F32)
    kid = lax.broadcasted_iota(I32, (PEER_NKEYS, st.shape[1]), 0)
    s1, i1 = _top_rows(st[:PEER_NKEYS], kid, PEER_TOPK)
    s2, i2 = _top_rows(st[PEER_NKEYS:], kid, PEER_TOPK)
    cand = jnp.concatenate([s1[a:a + 1] + s2 for a in range(PEER_TOPK)], axis=0)
    cidx = jnp.concatenate([i1[a:a + 1] * PEER_NKEYS + i2 for a in range(PEER_TOPK)], axis=0)
    sc, idx = _top_rows(cand, cidx, PEER_TOPK)
    e = jnp.exp(sc - jnp.max(sc, axis=0, keepdims=True))
    idx_ref[...] = idx
    g_ref[...] = e / jnp.sum(e, axis=0, keepdims=True)


def _peer_route(x1b, wq, keys_t, *, tm):
    n, dm = x1b.shape
    qd = 2 * PEER_HALF
    slots = PEER_HEADS * PEER_TOPK
    out_spec = pl.BlockSpec((PEER_TOPK, tm), lambda i, h: (h, i))
    return pl.pallas_call(
        _peer_route_kernel,
        out_shape=(jax.ShapeDtypeStruct((slots, n), I32), jax.ShapeDtypeStruct((slots, n), F32)),
        grid=(n // tm, PEER_HEADS),
        in_specs=[
            pl.BlockSpec((tm, dm), lambda i, h: (i, 0)),
            pl.BlockSpec((dm, qd), lambda i, h: (0, h)),
            pl.BlockSpec((None, 2 * PEER_NKEYS, qd), lambda i, h: (h, 0, 0)),
        ],
        out_specs=(out_spec, out_spec),
        compiler_params=pltpu.CompilerParams(dimension_semantics=("parallel", "parallel")),
        name="peer_route",
    )(x1b, wq, keys_t)


W_ROW_PITCH = PEER_NKEYS + V7X_SUBLANES


def _peer_gate_kernel(idx_ref, g_ref, o_ref, wbuf, idx_sc, g_sc, *, tb):
    idx_sc[...] = idx_ref[...].T
    g_sc[...] = g_ref[...].T
    rid = lax.broadcasted_iota(I32, (PEER_NKEYS, PEER_NKEYS), 0)

    def body(tok, carry):
        ir = idx_sc[pl.ds(tok, 1), :]
        gr = g_sc[pl.ds(tok, 1), :]
        r1 = jnp.where(rid == (ir >> 7), gr, 0.0).astype(BF16)
        r2t = jnp.where(rid == (ir & (PEER_NKEYS - 1)), 1.0, 0.0).astype(BF16)
        wt = lax.dot_general(r1, r2t, (((1,), (1,)), ((), ())), preferred_element_type=F32)
        wbuf[pl.ds(pl.multiple_of(tok * W_ROW_PITCH, V7X_SUBLANES), PEER_NKEYS), :] = wt
        return carry

    lax.fori_loop(0, tb, body, 0)
    for i1 in range(PEER_NKEYS):
        o_ref[:, i1 * PEER_NKEYS:(i1 + 1) * PEER_NKEYS] = (
            wbuf[pl.ds(i1, tb, stride=W_ROW_PITCH), :].astype(o_ref.dtype))


def _peer_gate_matrix(idx_t, g_t, *, tb):
    slots, n = idx_t.shape
    ne = PEER_NKEYS * PEER_NKEYS
    in_spec = pl.BlockSpec((slots, tb), lambda i: (0, i))
    return pl.pallas_call(
        functools.partial(_peer_gate_kernel, tb=tb),
        out_shape=jax.ShapeDtypeStruct((n, ne), BF16),
        grid=(n // tb,),
        in_specs=[in_spec, in_spec],
        out_specs=pl.BlockSpec((tb, ne), lambda i: (i, 0)),
        scratch_shapes=[
            pltpu.VMEM((tb * W_ROW_PITCH, PEER_NKEYS), F32),
            pltpu.VMEM((tb, slots), I32),
            pltpu.VMEM((tb, slots), F32),
        ],
        compiler_params=pltpu.CompilerParams(
            dimension_semantics=("parallel",),
            vmem_limit_bytes=_vmem_limit(_nbytes((tb, ne), BF16),
                                         scratch_bytes=_nbytes((tb * W_ROW_PITCH, PEER_NKEYS), F32)),
        ),
        name="peer_gate_matrix",
    )(idx_t, g_t)


def _peer_expert_kernel(x_ref, ut_ref, v_ref, w_ref, o_ref):
    a = jnp.dot(x_ref[...], ut_ref[...], preferred_element_type=F32)
    gelu = 0.5 * a * (1.0 + lax.erf(a * (0.5 ** 0.5)))
    g = (w_ref[...].astype(F32) * gelu).astype(BF16)
    contrib = jnp.dot(g, v_ref[...], preferred_element_type=F32)

    @pl.when(pl.program_id(1) == 0)
    def _():
        o_ref[...] = contrib

    @pl.when(pl.program_id(1) != 0)
    def _():
        o_ref[...] += contrib


def _peer_experts(x1b, ut, v, w, *, tm, ce):
    n, dm = x1b.shape
    ne = v.shape[0]
    return pl.pallas_call(
        _peer_expert_kernel,
        out_shape=jax.ShapeDtypeStruct((n, dm), F32),
        grid=(n // tm, ne // ce),
        in_specs=[
            pl.BlockSpec((tm, dm), lambda i, j: (i, 0)),
            pl.BlockSpec((dm, ce), lambda i, j: (0, j)),
            pl.BlockSpec((ce, dm), lambda i, j: (j, 0)),
            pl.BlockSpec((tm, ce), lambda i, j: (i, j)),
        ],
        out_specs=pl.BlockSpec((tm, dm), lambda i, j: (i, 0)),
        compiler_params=pltpu.CompilerParams(
            dimension_semantics=("parallel", "arbitrary"),
            vmem_limit_bytes=_vmem_limit(_nbytes((tm, dm), BF16), 2 * _nbytes((dm, ce), BF16),
                                         _nbytes((tm, ce), BF16), _nbytes((tm, dm), F32)),
        ),
        name="peer_experts",
    )(x1b, ut, v, w)


def _residual_ln_kernel(x_ref, y_ref, p_ref, g_ref, b_ref, o32_ref, o16_ref, *, alpha):
    out = _layer_norm_rows(alpha * x_ref[...] + y_ref[...] + p_ref[...], g_ref[...], b_ref[...])
    o32_ref[...] = out
    o16_ref[...] = out.astype(BF16)


def _residual_ln(x1, yf, ple, g, b, *, alpha, tm):
    n, dm = x1.shape
    row = pl.BlockSpec((tm, dm), lambda i: (i, 0))
    vec = pl.BlockSpec((1, dm), lambda i: (0, 0))
    return pl.pallas_call(
        functools.partial(_residual_ln_kernel, alpha=alpha),
        out_shape=(jax.ShapeDtypeStruct((n, dm), F32), jax.ShapeDtypeStruct((n, dm), BF16)),
        grid=(n // tm,),
        in_specs=[row, row, row, vec, vec],
        out_specs=(row, row),
        compiler_params=pltpu.CompilerParams(
            dimension_semantics=("parallel",),
            vmem_limit_bytes=_vmem_limit(4 * _nbytes((tm, dm), F32), _nbytes((tm, dm), BF16)),
        ),
        name="residual_layernorm",
    )(x1, yf, ple, g, b)


def _peer_keys_layout(keys):
    z = jnp.zeros_like(keys[:, 0])
    top = jnp.concatenate([keys[:, 0], z], axis=-1)
    bot = jnp.concatenate([z, keys[:, 1]], axis=-1)
    return jnp.concatenate([top, bot], axis=1).astype(BF16)


def kernel(x, p, w_in, b_in, diff_lambda, diff_subln, rel_bias, conv_w, conv_b, lru_wa, lru_ba, lru_wx,
           lru_bx, lru_lambda, w_branch, w_o, ln1_g, ln1_b, peer_wq, peer_keys, peer_u, peer_v, w_ple,
           w_ple_gate, b_ple_gate, ln2_g, ln2_b):
    bsz, seq, dm = x.shape
    depth = w_in.shape[0]
    n = bsz * seq
    alpha = (2 * depth) ** 0.25
    bwid = A_HEADS * A_VDIM
    t_attn = 512
    assert seq % t_attn == 0 and t_attn >= REL_MAX_DIST

    o_ka, o_va, o_lru, o_c, o_kc, o_vc, o_f = (j * bwid for j in (1, 2, 3, 5, 6, 7, 8))
    o_g = o_f + C_HEADS
    qk_scale = jnp.concatenate([
        jnp.full((bwid,), A_HALF ** -0.5 * LOG2E, F32), jnp.ones((bwid,), F32),
        jnp.full((bwid,), C_HDIM ** -0.5 * LOG2E, F32), jnp.ones((bwid,), F32)])[None]
    ones_row = jnp.ones((1, max(2 * bwid, N_BRANCH * dm)), F32)

    bias = _bias_tiles(rel_bias, t_attn)
    x2 = x.reshape(n, dm)
    xb = x2.astype(BF16)
    for i in range(depth):
        w = w_in[i]
        b = b_in[i]
        w_qk = jnp.concatenate([w[:, :o_va], w[:, o_c:o_vc]], axis=1).astype(BF16)
        b_qk = jnp.concatenate([b[:o_va], b[o_c:o_vc]])[None]
        qk = _proj(xb, w_qk, b_qk, qk_scale, out_dtype=BF16, tm=1024, tn=2048, name="proj_qk")
        w_v = jnp.concatenate([w[:, o_va:o_lru], w[:, o_vc:o_f]], axis=1).T.astype(BF16)
        b_v = jnp.concatenate([b[o_va:o_lru], b[o_vc:o_f]])[:, None]
        vt = _proj_t(xb, w_v, b_v, tm=t_attn, tn=1024, name="proj_v_t")
        bxg = _proj(xb, w[:, o_lru:o_c].astype(BF16), b[None, o_lru:o_c], ones_row[:, :2 * bwid],
                    out_dtype=F32, tm=1024, tn=1024, name="proj_lru")
        gates = _proj(xb, w[:, o_g:].astype(BF16), b[None, o_g:], ones_row[:, :N_BRANCH * dm],
                      out_dtype=BF16, tm=1024, tn=2048, act="sigmoid", name="proj_gates")
        ckb = _fgate_cumsum(xb.reshape(bsz, seq, dm), w[:, o_f:o_g].T.astype(BF16), b[o_f:o_g, None],
                            ts=min(seq, 1024))

        qk3 = qk.reshape(bsz, seq, 4 * bwid)
        vt4 = vt.reshape(bsz, seq // t_attn, 2 * bwid, t_attn)
        lam_init = 0.8 - 0.6 * math.exp(-0.3 * i)
        ya = _diff_attention(qk3, vt4, diff_lambda[i], diff_subln[i][:, None], bias, t=t_attn, lam_init=lam_init)
        yc = _fox_attention(qk3, vt4, ckb, t=t_attn)

        bw = bwid // LRU_BLOCKS
        wax = jnp.concatenate([lru_wa[i], lru_wx[i]], axis=-1).astype(BF16)
        bax = jnp.concatenate([lru_ba[i].reshape(LRU_BLOCKS, 1, bw), lru_bx[i].reshape(LRU_BLOCKS, 1, bw)], axis=-1)
        yb = _lru_branch(bxg, conv_w[i], conv_b[i][None], wax, bax, lru_lambda[i][None], bsz=bsz, t=256)

        merged = _merge(ya.reshape(n, bwid), yb, yc.reshape(n, bwid), w_branch[i].astype(BF16), gates,
                        tm=512, tn=1024)
        x1, x1b = _wo_ln(merged, w_o[i].astype(BF16), x2, ln1_g[i][None], ln1_b[i][None], alpha=alpha, tm=256)

        ple = _ple(x1b, w_ple_gate[i].astype(BF16), b_ple_gate[i][None], p[i].reshape(n, -1).astype(BF16),
                   w_ple[i].astype(BF16), tm=1024, tn=1024)
        idx_t, g_t = _peer_route(x1b, peer_wq[i].astype(BF16), _peer_keys_layout(peer_keys[i]), tm=256)
        wdense = _peer_gate_matrix(idx_t, g_t, tb=128)
        yf = _peer_experts(x1b, peer_u[i].T.astype(BF16), peer_v[i].astype(BF16), wdense, tm=512, ce=512)
        x2, xb = _residual_ln(x1, yf, ple, ln2_g[i][None], ln2_b[i][None], alpha=alpha, tm=512)
    return x2.reshape(bsz, seq, dm)
```

```python
import functools
import math

import jax
import jax.numpy as jnp
import numpy as np
from jax import lax
from jax.experimental import pallas as pl
from jax.experimental.pallas import tpu as pltpu

F32, BF16, I32 = jnp.float32, jnp.bfloat16, jnp.int32

V7X_VMEM_BYTES = 64 * 2**20
V7X_LANES = 128
V7X_SUBLANES = 8
VMEM_HEADROOM_BYTES = 8 * 2**20

A_HEADS = 8
A_HALF = 64
A_VDIM = 2 * A_HALF
LRU_BLOCKS = 8
CONV_WIDTH = 4
LRU_C = 8.0
C_HEADS = 8
C_HDIM = 128
N_BRANCH = 3
REL_BUCKETS = 32
REL_MAX_DIST = 128
PEER_HEADS = 8
PEER_NKEYS = 128
PEER_HALF = 64
PEER_TOPK = 16
LN_EPS = 1e-5

LOG2E = 1.4426950408889634
MASKED = -1e30


def _vmem_limit(*block_bytes, scratch_bytes=0):
    need = 2 * sum(block_bytes) + scratch_bytes + VMEM_HEADROOM_BYTES
    return int(min(max(need, 32 * 2**20), V7X_VMEM_BYTES - 4 * 2**20))


def _nbytes(shape, dtype):
    return int(np.prod(shape)) * jnp.dtype(dtype).itemsize


def _log_sigmoid(z):
    return jnp.minimum(z, 0.0) - jnp.log1p(jnp.exp(-jnp.abs(z)))


def _layer_norm_rows(y, g, b):
    mu = jnp.mean(y, axis=1, keepdims=True)
    yc = y - mu
    var = jnp.mean(yc * yc, axis=1, keepdims=True)
    return yc * lax.rsqrt(var + LN_EPS) * g + b


def _proj_kernel(x_ref, w_ref, b_ref, s_ref, o_ref, *, act):
    y = jnp.dot(x_ref[...], w_ref[...], preferred_element_type=F32)
    y = (y + b_ref[...]) * s_ref[...]
    if act == "sigmoid":
        y = jax.nn.sigmoid(y)
    o_ref[...] = y.astype(o_ref.dtype)


def _proj(x, w, b, s, *, out_dtype, tm, tn, act=None, name):
    m, k = x.shape
    n = w.shape[1]
    grid = (n // tn, m // tm)
    return pl.pallas_call(
        functools.partial(_proj_kernel, act=act),
        out_shape=jax.ShapeDtypeStruct((m, n), out_dtype),
        grid=grid,
        in_specs=[
            pl.BlockSpec((tm, k), lambda j, i: (i, 0)),
            pl.BlockSpec((k, tn), lambda j, i: (0, j)),
            pl.BlockSpec((1, tn), lambda j, i: (0, j)),
            pl.BlockSpec((1, tn), lambda j, i: (0, j)),
        ],
        out_specs=pl.BlockSpec((tm, tn), lambda j, i: (i, j)),
        compiler_params=pltpu.CompilerParams(
            dimension_semantics=("parallel", "parallel"),
            vmem_limit_bytes=_vmem_limit(_nbytes((tm, k), x.dtype), _nbytes((k, tn), w.dtype),
                                         _nbytes((tm, tn), out_dtype)),
        ),
        name=name,
    )(x, w, b, s)


def _proj_t_kernel(w_ref, x_ref, b_ref, o_ref):
    y = lax.dot_general(w_ref[...], x_ref[...], (((1,), (1,)), ((), ())), preferred_element_type=F32)
    o_ref[...] = (y + b_ref[...]).astype(o_ref.dtype)


def _proj_t(x, w_t, b_col, *, tm, tn, name):
    m, k = x.shape
    n = w_t.shape[0]
    return pl.pallas_call(
        _proj_t_kernel,
        out_shape=jax.ShapeDtypeStruct((m // tm, n, tm), BF16),
        grid=(n // tn, m // tm),
        in_specs=[
            pl.BlockSpec((tn, k), lambda j, i: (j, 0)),
            pl.BlockSpec((tm, k), lambda j, i: (i, 0)),
            pl.BlockSpec((tn, 1), lambda j, i: (j, 0)),
        ],
        out_specs=pl.BlockSpec((None, tn, tm), lambda j, i: (i, j, 0)),
        compiler_params=pltpu.CompilerParams(
            dimension_semantics=("parallel", "parallel"),
            vmem_limit_bytes=_vmem_limit(_nbytes((tm, k), BF16), _nbytes((tn, k), BF16), _nbytes((tn, tm), BF16)),
        ),
        name=name,
    )(w_t, x, b_col)


def _fgate_kernel(x_ref, w_ref, b_ref, o_ref, carry_ref, *, ts):
    @pl.when(pl.program_id(1) == 0)
    def _():
        carry_ref[...] = jnp.zeros_like(carry_ref)

    z = lax.dot_general(w_ref[...], x_ref[...], (((1,), (1,)), ((), ())),
                        preferred_element_type=F32) + b_ref[...]
    c = _log_sigmoid(z)
    lane = lax.broadcasted_iota(I32, c.shape, 1)
    d = 1
    while d < ts:
        c = c + jnp.where(lane >= d, pltpu.roll(c, d, 1), 0.0)
        d *= 2
    c = c + carry_ref[:, 0:1]
    carry_ref[...] = jnp.broadcast_to(c[:, ts - 1:ts], carry_ref.shape)
    c = c * LOG2E
    for h in range(C_HEADS):
        o_ref[h] = jnp.broadcast_to(c[h:h + 1, :], (V7X_LANES, ts)).T


def _fgate_cumsum(x3, wf_t, bf, *, ts):
    bsz, seq, dm = x3.shape
    return pl.pallas_call(
        functools.partial(_fgate_kernel, ts=ts),
        out_shape=jax.ShapeDtypeStruct((bsz, C_HEADS, seq, V7X_LANES), F32),
        grid=(bsz, seq // ts),
        in_specs=[
            pl.BlockSpec((None, ts, dm), lambda b, i: (b, i, 0)),
            pl.BlockSpec((C_HEADS, dm), lambda b, i: (0, 0)),
            pl.BlockSpec((C_HEADS, 1), lambda b, i: (0, 0)),
        ],
        out_specs=pl.BlockSpec((None, C_HEADS, ts, V7X_LANES), lambda b, i: (b, 0, i, 0)),
        scratch_shapes=[pltpu.VMEM((C_HEADS, V7X_LANES), F32)],
        compiler_params=pltpu.CompilerParams(dimension_semantics=("parallel", "arbitrary")),
        name="fgate_cumsum",
    )(x3, wf_t, bf)


def _rel_bucket_tiles(t):
    q = np.arange(t)[None, :]
    k = np.arange(t)[:, None]
    out = []
    for off in (0, t):
        rel = q - k + off
        n = np.maximum(rel, 0)
        max_exact = REL_BUCKETS // 2
        nf = np.maximum(n, 1).astype(np.float32)
        large = max_exact + (np.log(nf / np.float32(max_exact)) / np.float32(math.log(REL_MAX_DIST / max_exact))
                             * np.float32(REL_BUCKETS - max_exact)).astype(np.int32)
        large = np.minimum(large, REL_BUCKETS - 1)
        bkt = np.where(n < max_exact, n, large)
        out.append(np.where(rel >= 0, bkt, -1))
    return np.stack(out).astype(np.int32)


def _bias_kernel(rel_ref, bkt_ref, o_ref):
    h = pl.program_id(0)
    bkt = bkt_ref[...]
    far = rel_ref[REL_BUCKETS - 1, h]
    acc = jnp.zeros(bkt.shape, F32)
    for b in range(REL_BUCKETS):
        acc = jnp.where(bkt == b, rel_ref[b, h] - far, acc)
    o_ref[...] = jnp.where(bkt < 0, MASKED, acc * LOG2E)


def _bias_tiles(rel_bias, t):
    bkt = jnp.asarray(_rel_bucket_tiles(t))
    return pl.pallas_call(
        _bias_kernel,
        out_shape=jax.ShapeDtypeStruct((A_HEADS, 2, t, t), F32),
        grid=(A_HEADS,),
        in_specs=[
            pl.BlockSpec(memory_space=pltpu.SMEM),
            pl.BlockSpec((2, t, t), lambda h: (0, 0, 0)),
        ],
        out_specs=pl.BlockSpec((None, 2, t, t), lambda h: (h, 0, 0, 0)),
        name="rel_bias_tiles",
    )(rel_bias, bkt)


HEADS_PER_STEP = 2


def _sublane_allreduce(x, op):
    for shift in (1, 2, 4):
        x = op(x, pltpu.roll(x, shift, 0))
    return x


def _rows3(x):
    return x.reshape(x.shape[0] // V7X_SUBLANES, V7X_SUBLANES, x.shape[1])


def _attn_tile(s, vt, h, m_sc, l_sc, acc_sc):
    s3 = _rows3(s)
    m_prev = m_sc[h]
    m_new = jnp.maximum(m_prev, _sublane_allreduce(jnp.max(s3, axis=0), jnp.maximum))
    alpha = jnp.exp2(m_prev - m_new)
    p3 = jnp.exp2(s3 - m_new[None])
    l_sc[h] = alpha * l_sc[h] + jnp.sum(p3, axis=0)
    pv = jnp.dot(vt, p3.reshape(s.shape).astype(BF16), preferred_element_type=F32)
    acc_sc[h] = (_rows3(acc_sc[h]) * alpha[None]).reshape(pv.shape) + pv
    m_sc[h] = m_new


def _diff_attn_kernel(lam_ref, subln_ref, q_ref, k_ref, vt_ref, bias_ref, o_ref,
                      m_sc, l_sc, acc_sc, *, t, lam_init):
    qi = pl.program_id(2)
    m_sc[...] = jnp.full_like(m_sc, MASKED)
    l_sc[...] = jnp.zeros_like(l_sc)
    acc_sc[...] = jnp.zeros_like(acc_sc)

    qqs = []
    for h in range(HEADS_PER_STEP):
        q = q_ref[:, h * A_VDIM:(h + 1) * A_VDIM]
        lane = lax.broadcasted_iota(I32, q.shape, 1)
        zero = jnp.zeros_like(q)
        qqs.append(jnp.concatenate([jnp.where(lane < A_HALF, q, zero), jnp.where(lane >= A_HALF, q, zero)], axis=0))

    def tile(kj, near):
        start = pl.multiple_of(kj * t, t)
        for h in range(HEADS_PER_STEP):
            cols = slice(h * A_VDIM, (h + 1) * A_VDIM)
            s = lax.dot_general(k_ref[pl.ds(start, t), cols], qqs[h], (((1,), (1,)), ((), ())),
                                preferred_element_type=F32)
            if near is not None:
                bias = bias_ref[h, near]
                s = s + jnp.concatenate([bias, bias], axis=1)
            _attn_tile(s, vt_ref[kj, cols, :], h, m_sc, l_sc, acc_sc)

    def far_body(kj, carry):
        tile(kj, None)
        return carry

    lax.fori_loop(0, jnp.maximum(qi - 1, 0), far_body, 0)

    @pl.when(qi >= 1)
    def _():
        tile(qi - 1, 1)

    tile(qi, 0)

    lv = lam_ref[...]
    lam = (jnp.exp(jnp.sum(lv[0:1] * lv[1:2], axis=1, keepdims=True))
           - jnp.exp(jnp.sum(lv[2:3] * lv[3:4], axis=1, keepdims=True)) + lam_init)
    for h in range(HEADS_PER_STEP):
        l = _sublane_allreduce(l_sc[h], jnp.add)
        r = (_rows3(acc_sc[h]) / l[None]).reshape(A_VDIM, 2 * t)
        o = r[:, :t] - lam * r[:, t:]
        ms = _sublane_allreduce(jnp.sum(_rows3(o * o), axis=0), jnp.add) * (1.0 / A_VDIM)
        y = (_rows3(o) * lax.rsqrt(ms + LN_EPS)[None]).reshape(A_VDIM, t) * (subln_ref[...] * (1.0 - lam_init))
        o_ref[:, h * A_VDIM:(h + 1) * A_VDIM] = y.T.astype(o_ref.dtype)


def _diff_attention(qk3, vt4, lam4, subln_col, bias, *, t, lam_init):
    bsz, seq, _ = qk3.shape
    nq = seq // t
    hp = HEADS_PER_STEP
    wid = hp * A_VDIM
    kblk = A_HEADS // hp
    return pl.pallas_call(
        functools.partial(_diff_attn_kernel, t=t, lam_init=lam_init),
        out_shape=jax.ShapeDtypeStruct((bsz, seq, A_HEADS * A_VDIM), BF16),
        grid=(bsz, A_HEADS // hp, nq),
        in_specs=[
            pl.BlockSpec((4, A_HALF), lambda b, h, i: (0, 0)),
            pl.BlockSpec((A_VDIM, 1), lambda b, h, i: (0, 0)),
            pl.BlockSpec((None, t, wid), lambda b, h, i: (b, i, h)),
            pl.BlockSpec((None, seq, wid), lambda b, h, i: (b, 0, kblk + h)),
            pl.BlockSpec((None, nq, wid, t), lambda b, h, i: (b, 0, h, 0)),
            pl.BlockSpec((hp, 2, t, t), lambda b, h, i: (h, 0, 0, 0)),
        ],
        out_specs=pl.BlockSpec((None, t, wid), lambda b, h, i: (b, i, h)),
        scratch_shapes=[
            pltpu.VMEM((hp, V7X_SUBLANES, 2 * t), F32),
            pltpu.VMEM((hp, V7X_SUBLANES, 2 * t), F32),
            pltpu.VMEM((hp, A_VDIM, 2 * t), F32),
        ],
        compiler_params=pltpu.CompilerParams(
            dimension_semantics=("parallel", "parallel", "arbitrary"),
            vmem_limit_bytes=_vmem_limit(2 * _nbytes((seq, wid), BF16), _nbytes((hp, 2, t, t), F32),
                                         scratch_bytes=_nbytes((hp, A_VDIM + 2 * t, 2 * t), F32)),
        ),
        name="diff_attention",
    )(lam4, subln_col, qk3, qk3, vt4, bias)


def _fox_attn_kernel(q_ref, k_ref, vt_ref, ck_ref, o_ref, m_sc, l_sc, acc_sc, *, t):
    qi = pl.program_id(2)
    m_sc[...] = jnp.full_like(m_sc, MASKED)
    l_sc[...] = jnp.zeros_like(l_sc)
    acc_sc[...] = jnp.zeros_like(acc_sc)
    qs = [q_ref[:, h * C_HDIM:(h + 1) * C_HDIM] for h in range(HEADS_PER_STEP)]

    def tile(kj, diagonal):
        start = pl.multiple_of(kj * t, t)
        for h in range(HEADS_PER_STEP):
            cols = slice(h * C_HDIM, (h + 1) * C_HDIM)
            s = lax.dot_general(k_ref[pl.ds(start, t), cols], qs[h], (((1,), (1,)), ((), ())),
                                preferred_element_type=F32)
            ck = ck_ref[h, pl.ds(start, t), :]
            s = s - jnp.concatenate([ck] * (t // V7X_LANES), axis=1)
            if diagonal:
                key = lax.broadcasted_iota(I32, s.shape, 0)
                qry = lax.broadcasted_iota(I32, s.shape, 1)
                s = jnp.where(key <= qry, s, MASKED)
            _attn_tile(s, vt_ref[kj, cols, :], h, m_sc, l_sc, acc_sc)

    def body(kj, carry):
        tile(kj, False)
        return carry

    lax.fori_loop(0, qi, body, 0)
    tile(qi, True)
    for h in range(HEADS_PER_STEP):
        l = _sublane_allreduce(l_sc[h], jnp.add)
        o = (_rows3(acc_sc[h]) / l[None]).reshape(C_HDIM, t)
        o_ref[:, h * C_HDIM:(h + 1) * C_HDIM] = o.T.astype(o_ref.dtype)


def _fox_attention(qk3, vt4, ckb, *, t):
    bsz, seq, _ = qk3.shape
    nq = seq // t
    hp = HEADS_PER_STEP
    wid = hp * C_HDIM
    qblk = 2 * (A_HEADS // hp)
    kblk = qblk + C_HEADS // hp
    vblk = A_HEADS // hp
    return pl.pallas_call(
        functools.partial(_fox_attn_kernel, t=t),
        out_shape=jax.ShapeDtypeStruct((bsz, seq, C_HEADS * C_HDIM), BF16),
        grid=(bsz, C_HEADS // hp, nq),
        in_specs=[
            pl.BlockSpec((None, t, wid), lambda b, h, i: (b, i, qblk + h)),
            pl.BlockSpec((None, seq, wid), lambda b, h, i: (b, 0, kblk + h)),
            pl.BlockSpec((None, nq, wid, t), lambda b, h, i: (b, 0, vblk + h, 0)),
            pl.BlockSpec((None, hp, seq, V7X_LANES), lambda b, h, i: (b, h, 0, 0)),
        ],
        out_specs=pl.BlockSpec((None, t, wid), lambda b, h, i: (b, i, h)),
        scratch_shapes=[
            pltpu.VMEM((hp, V7X_SUBLANES, t), F32),
            pltpu.VMEM((hp, V7X_SUBLANES, t), F32),
            pltpu.VMEM((hp, C_HDIM, t), F32),
        ],
        compiler_params=pltpu.CompilerParams(
            dimension_semantics=("parallel", "parallel", "arbitrary"),
            vmem_limit_bytes=_vmem_limit(2 * _nbytes((seq, wid), BF16), _nbytes((hp, seq, V7X_LANES), F32),
                                         scratch_bytes=_nbytes((hp, C_HDIM + 2 * t, t), F32)),
        ),
        name="fox_attention",
    )(qk3, qk3, vt4, ckb)


def _lru_kernel(bx_ref, bg_ref, cw_ref, cb_ref, wax_ref, bax_ref, lam_ref, o_ref, xbuf, hprev, *, t):
    pad = V7X_SUBLANES

    @pl.when(pl.program_id(1) == 0)
    def _():
        xbuf[0:pad, :] = jnp.zeros((pad, xbuf.shape[1]), F32)
        hprev[...] = jnp.zeros_like(hprev)

    xbuf[pad:pad + t, :] = bx_ref[...]
    xc = cb_ref[...]
    for tap in range(CONV_WIDTH):
        xc = xc + xbuf[pl.ds(pad - (CONV_WIDTH - 1) + tap, t), :] * cw_ref[tap:tap + 1, :]
    xbuf[0:pad, :] = bx_ref[t - pad:t, :]

    bw = xc.shape[1] // LRU_BLOCKS
    row = lax.broadcasted_iota(I32, (t, bw), 0)
    for g in range(LRU_BLOCKS):
        cols = slice(g * bw, (g + 1) * bw)
        xg = xc[:, cols]
        z = jnp.dot(xg.astype(BF16), wax_ref[g], preferred_element_type=F32) + bax_ref[g]
        r = jax.nn.sigmoid(z[:, :bw])
        gi = jax.nn.sigmoid(z[:, bw:])
        log_a = (LRU_C * r) * _log_sigmoid(lam_ref[:, cols])
        a = jnp.exp(log_a)
        u = jnp.sqrt(1.0 - jnp.exp(2.0 * log_a)) * (gi * xg)
        d = 1
        while d < t:
            keep = row >= d
            a_sh = jnp.where(keep, pltpu.roll(a, d, 0), 1.0)
            u_sh = jnp.where(keep, pltpu.roll(u, d, 0), 0.0)
            u = a * u_sh + u
            a = a * a_sh
            d *= 2
        h = a * hprev[0:1, cols] + u
        hprev[0:1, cols] = h[t - 1:t, :]
        o_ref[:, cols] = (jax.nn.gelu(bg_ref[:, cols], approximate=True) * h).astype(o_ref.dtype)


def _lru_branch(bxg, conv_w, conv_b, wax, bax, lam, *, bsz, t):
    n, two_w = bxg.shape
    w = two_w // 2
    nt = n // bsz // t
    bw = w // LRU_BLOCKS
    return pl.pallas_call(
        functools.partial(_lru_kernel, t=t),
        out_shape=jax.ShapeDtypeStruct((n, w), BF16),
        grid=(bsz, nt),
        in_specs=[
            pl.BlockSpec((t, w), lambda b, i: (b * nt + i, 0)),
            pl.BlockSpec((t, w), lambda b, i: (b * nt + i, 1)),
            pl.BlockSpec((CONV_WIDTH, w), lambda b, i: (0, 0)),
            pl.BlockSpec((1, w), lambda b, i: (0, 0)),
            pl.BlockSpec((LRU_BLOCKS, bw, 2 * bw), lambda b, i: (0, 0, 0)),
            pl.BlockSpec((LRU_BLOCKS, 1, 2 * bw), lambda b, i: (0, 0, 0)),
            pl.BlockSpec((1, w), lambda b, i: (0, 0)),
        ],
        out_specs=pl.BlockSpec((t, w), lambda b, i: (b * nt + i, 0)),
        scratch_shapes=[
            pltpu.VMEM((t + V7X_SUBLANES, w), F32),
            pltpu.VMEM((V7X_SUBLANES, w), F32),
        ],
        compiler_params=pltpu.CompilerParams(dimension_semantics=("parallel", "arbitrary")),
        name="conv_rglru",
    )(bxg, bxg, conv_w, conv_b, wax, bax, lam)


def _merge_kernel(ya_ref, yb_ref, yc_ref, w_ref, g0_ref, g1_ref, g2_ref, o_ref):
    acc = g0_ref[...].astype(F32) * jnp.dot(ya_ref[...], w_ref[0], preferred_element_type=F32)
    acc = acc + g1_ref[...].astype(F32) * jnp.dot(yb_ref[...], w_ref[1], preferred_element_type=F32)
    acc = acc + g2_ref[...].astype(F32) * jnp.dot(yc_ref[...], w_ref[2], preferred_element_type=F32)
    o_ref[...] = acc.astype(o_ref.dtype)


def _merge(ya, yb, yc, wb, gates, *, tm, tn):
    n, bwid = ya.shape
    dm = wb.shape[2]
    nc = dm // tn
    y_spec = pl.BlockSpec((tm, bwid), lambda j, i: (i, 0))
    return pl.pallas_call(
        _merge_kernel,
        out_shape=jax.ShapeDtypeStruct((n, dm), BF16),
        grid=(nc, n // tm),
        in_specs=[
            y_spec, y_spec, y_spec,
            pl.BlockSpec((N_BRANCH, bwid, tn), lambda j, i: (0, 0, j)),
            pl.BlockSpec((tm, tn), lambda j, i: (i, j)),
            pl.BlockSpec((tm, tn), lambda j, i: (i, nc + j)),
            pl.BlockSpec((tm, tn), lambda j, i: (i, 2 * nc + j)),
        ],
        out_specs=pl.BlockSpec((tm, tn), lambda j, i: (i, j)),
        compiler_params=pltpu.CompilerParams(
            dimension_semantics=("parallel", "parallel"),
            vmem_limit_bytes=_vmem_limit(3 * _nbytes((tm, bwid), BF16), _nbytes((N_BRANCH, bwid, tn), BF16),
                                         4 * _nbytes((tm, tn), BF16)),
        ),
        name="gated_merge",
    )(ya, yb, yc, wb, gates, gates, gates)


def _wo_ln_kernel(m_ref, w_ref, x_ref, g_ref, b_ref, o32_ref, o16_ref, *, alpha):
    y = jnp.dot(m_ref[...], w_ref[...], preferred_element_type=F32) + alpha * x_ref[...]
    out = _layer_norm_rows(y, g_ref[...], b_ref[...])
    o32_ref[...] = out
    o16_ref[...] = out.astype(BF16)


def _wo_ln(merged, wo, x, g, b, *, alpha, tm):
    n, dm = x.shape
    row = pl.BlockSpec((tm, dm), lambda i: (i, 0))
    vec = pl.BlockSpec((1, dm), lambda i: (0, 0))
    return pl.pallas_call(
        functools.partial(_wo_ln_kernel, alpha=alpha),
        out_shape=(jax.ShapeDtypeStruct((n, dm), F32), jax.ShapeDtypeStruct((n, dm), BF16)),
        grid=(n // tm,),
        in_specs=[row, pl.BlockSpec((dm, dm), lambda i: (0, 0)), row, vec, vec],
        out_specs=(row, row),
        compiler_params=pltpu.CompilerParams(
            dimension_semantics=("parallel",),
            vmem_limit_bytes=_vmem_limit(_nbytes((tm, dm), BF16), _nbytes((dm, dm), BF16),
                                         2 * _nbytes((tm, dm), F32), _nbytes((tm, dm), BF16)),
        ),
        name="wo_layernorm",
    )(merged, wo, x, g, b)


def _ple_kernel(x_ref, wg_ref, bg_ref, p_ref, we_ref, o_ref):
    gate = jax.nn.sigmoid(jnp.dot(x_ref[...], wg_ref[...], preferred_element_type=F32) + bg_ref[...])
    o_ref[...] = gate * jnp.dot(p_ref[...], we_ref[...], preferred_element_type=F32)


def _ple(x1b, wg, bg, pb, we, *, tm, tn):
    n, dm = x1b.shape
    pdim = pb.shape[1]
    return pl.pallas_call(
        _ple_kernel,
        out_shape=jax.ShapeDtypeStruct((n, dm), F32),
        grid=(dm // tn, n // tm),
        in_specs=[
            pl.BlockSpec((tm, dm), lambda j, i: (i, 0)),
            pl.BlockSpec((dm, tn), lambda j, i: (0, j)),
            pl.BlockSpec((1, tn), lambda j, i: (0, j)),
            pl.BlockSpec((tm, pdim), lambda j, i: (i, 0)),
            pl.BlockSpec((pdim, tn), lambda j, i: (0, j)),
        ],
        out_specs=pl.BlockSpec((tm, tn), lambda j, i: (i, j)),
        compiler_params=pltpu.CompilerParams(
            dimension_semantics=("parallel", "parallel"),
            vmem_limit_bytes=_vmem_limit(_nbytes((tm, dm), BF16), _nbytes((dm, tn), BF16),
                                         _nbytes((tm, tn), F32)),
        ),
        name="gated_ple",
    )(x1b, wg, bg, pb, we)


def _top_rows(s, payload, k):
    rows = s.shape[0]
    rid = lax.broadcasted_iota(I32, s.shape, 0)
    vals, pays = [], []
    for _ in range(k):
        m = jnp.max(s, axis=0, keepdims=True)
        pos = jnp.min(jnp.where(s == m, rid, rows), axis=0, keepdims=True)
        hit = rid == pos
        vals.append(m)
        pays.append(pos if payload is None else jnp.max(jnp.where(hit, payload, -1), axis=0, keepdims=True))
        s = jnp.where(hit, -jnp.inf, s)
    return jnp.concatenate(vals, axis=0), jnp.concatenate(pays, axis=0)


def _peer_route_kernel(x_ref, wq_ref, keys_ref, idx_ref, g_ref):
    q = jnp.dot(x_ref[...], wq_ref[...], preferred_element_type=F32).astype(BF16)
    st = lax.dot_general(keys_ref[...], q, (((1,), (1,)), ((), ())), preferred_element_type=F32)
    s1, i1 = _top_rows(st[:PEER_NKEYS], None, PEER_TOPK)
    s2, i2 = _top_rows(st[PEER_NKEYS:], None, PEER_TOPK)
    cand, cidx = [], []
    for a in range(PEER_TOPK):
        nb = PEER_TOPK // (a + 1)
        cand.append(s1[a:a + 1] + s2[:nb])
        cidx.append(i1[a:a + 1] * PEER_NKEYS + i2[:nb])
    npairs = sum(c.shape[0] for c in cand)
    npad = -npairs % V7X_SUBLANES
    cand.append(jnp.full((npad, st.shape[1]), -jnp.inf, F32))
    cidx.append(jnp.full((npad, st.shape[1]), -1, I32))
    sc, idx = _top_rows(jnp.concatenate(cand, axis=0), jnp.concatenate(cidx, axis=0), PEER_TOPK)
    e = jnp.exp(sc - jnp.max(sc, axis=0, keepdims=True))
    idx_ref[...] = idx
    g_ref[...] = e / jnp.sum(e, axis=0, keepdims=True)


def _peer_route(x1b, wq, keys_t, *, tm):
    n, dm = x1b.shape
    qd = 2 * PEER_HALF
    slots = PEER_HEADS * PEER_TOPK
    out_spec = pl.BlockSpec((PEER_TOPK, tm), lambda i, h: (h, i))
    return pl.pallas_call(
        _peer_route_kernel,
        out_shape=(jax.ShapeDtypeStruct((slots, n), I32), jax.ShapeDtypeStruct((slots, n), F32)),
        grid=(n // tm, PEER_HEADS),
        in_specs=[
            pl.BlockSpec((tm, dm), lambda i, h: (i, 0)),
            pl.BlockSpec((dm, qd), lambda i, h: (0, h)),
            pl.BlockSpec((None, 2 * PEER_NKEYS, qd), lambda i, h: (h, 0, 0)),
        ],
        out_specs=(out_spec, out_spec),
        compiler_params=pltpu.CompilerParams(dimension_semantics=("parallel", "parallel")),
        name="peer_route",
    )(x1b, wq, keys_t)


W_ROW_PITCH = PEER_NKEYS + V7X_SUBLANES


def _peer_gate_kernel(idx_ref, g_ref, o_ref, wbuf, idx_sc, g_sc, *, tb):
    idx_sc[...] = idx_ref[...].T
    g_sc[...] = g_ref[...].T
    rid = lax.broadcasted_iota(I32, (PEER_NKEYS, PEER_NKEYS), 0).astype(F32).astype(BF16)
    one = jnp.ones((PEER_NKEYS, PEER_NKEYS), BF16)
    zero = jnp.zeros((PEER_NKEYS, PEER_NKEYS), BF16)
    group = 2 * V7X_SUBLANES
    packed_rows = 2 * V7X_SUBLANES

    def body(j, carry):
        base = pl.multiple_of(j * group, group)
        ib = idx_sc[pl.ds(base, group), :]
        gb = g_sc[pl.ds(base, group), :]
        i1b = (ib >> 7).astype(F32)
        i2b = (ib & (PEER_NKEYS - 1)).astype(F32)
        for r in range(group):
            def rep(v):
                one_vreg = jnp.broadcast_to(v[r:r + 1, :], (packed_rows, PEER_NKEYS)).astype(BF16)
                return jnp.concatenate([one_vreg] * (PEER_NKEYS // packed_rows), axis=0)
            r1 = jnp.where(rid == rep(i1b), rep(gb), zero)
            r2t = jnp.where(rid == rep(i2b), one, zero)
            wt = lax.dot_general(r1, r2t, (((1,), (1,)), ((), ())), preferred_element_type=F32)
            wbuf[pl.ds(pl.multiple_of((base + r) * W_ROW_PITCH, V7X_SUBLANES), PEER_NKEYS), :] = wt
        return carry

    lax.fori_loop(0, tb // group, body, 0)
    for i1 in range(PEER_NKEYS):
        o_ref[:, i1 * PEER_NKEYS:(i1 + 1) * PEER_NKEYS] = (
            wbuf[pl.ds(i1, tb, stride=W_ROW_PITCH), :].astype(o_ref.dtype))


def _peer_gate_matrix(idx_t, g_t, *, tb):
    slots, n = idx_t.shape
    ne = PEER_NKEYS * PEER_NKEYS
    in_spec = pl.BlockSpec((slots, tb), lambda i: (0, i))
    return pl.pallas_call(
        functools.partial(_peer_gate_kernel, tb=tb),
        out_shape=jax.ShapeDtypeStruct((n, ne), BF16),
        grid=(n // tb,),
        in_specs=[in_spec, in_spec],
        out_specs=pl.BlockSpec((tb, ne), lambda i: (i, 0)),
        scratch_shapes=[
            pltpu.VMEM((tb * W_ROW_PITCH, PEER_NKEYS), F32),
            pltpu.VMEM((tb, slots), I32),
            pltpu.VMEM((tb, slots), F32),
        ],
        compiler_params=pltpu.CompilerParams(
            dimension_semantics=("parallel",),
            vmem_limit_bytes=_vmem_limit(_nbytes((tb, ne), BF16),
                                         scratch_bytes=_nbytes((tb * W_ROW_PITCH, PEER_NKEYS), F32)),
        ),
        name="peer_gate_matrix",
    )(idx_t, g_t)


def _peer_expert_kernel(x_ref, ut_ref, v_ref, w_ref, o_ref):
    @pl.when(pl.program_id(1) == 0)
    def _():
        o_ref[...] = jnp.zeros_like(o_ref)

    a = jnp.dot(x_ref[...], ut_ref[...], preferred_element_type=F32)
    gelu = 0.5 * a * (1.0 + lax.erf(a * (0.5 ** 0.5)))
    g = (w_ref[...].astype(F32) * gelu).astype(BF16)
    o_ref[...] += jnp.dot(g, v_ref[...], preferred_element_type=F32)


def _peer_experts(x1b, ut, v, w, *, tm, ce):
    n, dm = x1b.shape
    ne = v.shape[0]
    return pl.pallas_call(
        _peer_expert_kernel,
        out_shape=jax.ShapeDtypeStruct((n, dm), F32),
        grid=(n // tm, ne // ce),
        in_specs=[
            pl.BlockSpec((tm, dm), lambda i, j: (i, 0)),
            pl.BlockSpec((dm, ce), lambda i, j: (0, j)),
            pl.BlockSpec((ce, dm), lambda i, j: (j, 0)),
            pl.BlockSpec((tm, ce), lambda i, j: (i, j)),
        ],
        out_specs=pl.BlockSpec((tm, dm), lambda i, j: (i, 0)),
        compiler_params=pltpu.CompilerParams(
            dimension_semantics=("parallel", "arbitrary"),
            vmem_limit_bytes=_vmem_limit(_nbytes((tm, dm), BF16), 2 * _nbytes((dm, ce), BF16),
                                         _nbytes((tm, ce), BF16), _nbytes((tm, dm), F32)),
        ),
        name="peer_experts",
    )(x1b, ut, v, w)


def _residual_ln_kernel(x_ref, y_ref, p_ref, g_ref, b_ref, o32_ref, o16_ref, *, alpha):
    out = _layer_norm_rows(alpha * x_ref[...] + y_ref[...] + p_ref[...], g_ref[...], b_ref[...])
    o32_ref[...] = out
    o16_ref[...] = out.astype(BF16)


def _residual_ln(x1, yf, ple, g, b, *, alpha, tm):
    n, dm = x1.shape
    row = pl.BlockSpec((tm, dm), lambda i: (i, 0))
    vec = pl.BlockSpec((1, dm), lambda i: (0, 0))
    return pl.pallas_call(
        functools.partial(_residual_ln_kernel, alpha=alpha),
        out_shape=(jax.ShapeDtypeStruct((n, dm), F32), jax.ShapeDtypeStruct((n, dm), BF16)),
        grid=(n // tm,),
        in_specs=[row, row, row, vec, vec],
        out_specs=(row, row),
        compiler_params=pltpu.CompilerParams(
            dimension_semantics=("parallel",),
            vmem_limit_bytes=_vmem_limit(4 * _nbytes((tm, dm), F32), _nbytes((tm, dm), BF16)),
        ),
        name="residual_layernorm",
    )(x1, yf, ple, g, b)


def _peer_keys_layout(keys):
    z = jnp.zeros_like(keys[:, 0])
    top = jnp.concatenate([keys[:, 0], z], axis=-1)
    bot = jnp.concatenate([z, keys[:, 1]], axis=-1)
    return jnp.concatenate([top, bot], axis=1).astype(BF16)


def kernel(x, p, w_in, b_in, diff_lambda, diff_subln, rel_bias, conv_w, conv_b, lru_wa, lru_ba, lru_wx,
           lru_bx, lru_lambda, w_branch, w_o, ln1_g, ln1_b, peer_wq, peer_keys, peer_u, peer_v, w_ple,
           w_ple_gate, b_ple_gate, ln2_g, ln2_b):
    bsz, seq, dm = x.shape
    depth = w_in.shape[0]
    n = bsz * seq
    alpha = (2 * depth) ** 0.25
    bwid = A_HEADS * A_VDIM
    t_attn = 512
    assert seq % t_attn == 0 and t_attn >= REL_MAX_DIST

    o_ka, o_va, o_lru, o_c, o_kc, o_vc, o_f = (j * bwid for j in (1, 2, 3, 5, 6, 7, 8))
    o_g = o_f + C_HEADS
    qk_scale = jnp.concatenate([
        jnp.full((bwid,), A_HALF ** -0.5 * LOG2E, F32), jnp.ones((bwid,), F32),
        jnp.full((bwid,), C_HDIM ** -0.5 * LOG2E, F32), jnp.ones((bwid,), F32)])[None]
    ones_row = jnp.ones((1, max(2 * bwid, N_BRANCH * dm)), F32)

    bias = _bias_tiles(rel_bias, t_attn)
    x2 = x.reshape(n, dm)
    xb = x2.astype(BF16)
    for i in range(depth):
        w = w_in[i]
        b = b_in[i]
        w_qk = jnp.concatenate([w[:, :o_va], w[:, o_c:o_vc]], axis=1).astype(BF16)
        b_qk = jnp.concatenate([b[:o_va], b[o_c:o_vc]])[None]
        qk = _proj(xb, w_qk, b_qk, qk_scale, out_dtype=BF16, tm=1024, tn=2048, name="proj_qk")
        w_v = jnp.concatenate([w[:, o_va:o_lru], w[:, o_vc:o_f]], axis=1).T.astype(BF16)
        b_v = jnp.concatenate([b[o_va:o_lru], b[o_vc:o_f]])[:, None]
        vt = _proj_t(xb, w_v, b_v, tm=t_attn, tn=1024, name="proj_v_t")
        bxg = _proj(xb, w[:, o_lru:o_c].astype(BF16), b[None, o_lru:o_c], ones_row[:, :2 * bwid],
                    out_dtype=F32, tm=1024, tn=1024, name="proj_lru")
        gates = _proj(xb, w[:, o_g:].astype(BF16), b[None, o_g:], ones_row[:, :N_BRANCH * dm],
                      out_dtype=BF16, tm=1024, tn=2048, act="sigmoid", name="proj_gates")
        ckb = _fgate_cumsum(xb.reshape(bsz, seq, dm), w[:, o_f:o_g].T.astype(BF16), b[o_f:o_g, None],
                            ts=min(seq, 1024))

        qk3 = qk.reshape(bsz, seq, 4 * bwid)
        vt4 = vt.reshape(bsz, seq // t_attn, 2 * bwid, t_attn)
        lam_init = 0.8 - 0.6 * math.exp(-0.3 * i)
        ya = _diff_attention(qk3, vt4, diff_lambda[i], diff_subln[i][:, None], bias, t=t_attn, lam_init=lam_init)
        yc = _fox_attention(qk3, vt4, ckb, t=t_attn)

        bw = bwid // LRU_BLOCKS
        wax = jnp.concatenate([lru_wa[i], lru_wx[i]], axis=-1).astype(BF16)
        bax = jnp.concatenate([lru_ba[i].reshape(LRU_BLOCKS, 1, bw), lru_bx[i].reshape(LRU_BLOCKS, 1, bw)], axis=-1)
        yb = _lru_branch(bxg, conv_w[i], conv_b[i][None], wax, bax, lru_lambda[i][None], bsz=bsz, t=256)

        merged = _merge(ya.reshape(n, bwid), yb, yc.reshape(n, bwid), w_branch[i].astype(BF16), gates,
                        tm=512, tn=1024)
        x1, x1b = _wo_ln(merged, w_o[i].astype(BF16), x2, ln1_g[i][None], ln1_b[i][None], alpha=alpha, tm=256)

        ple = _ple(x1b, w_ple_gate[i].astype(BF16), b_ple_gate[i][None], p[i].reshape(n, -1).astype(BF16),
                   w_ple[i].astype(BF16), tm=1024, tn=1024)
        idx_t, g_t = _peer_route(x1b, peer_wq[i].astype(BF16), _peer_keys_layout(peer_keys[i]), tm=512)
        wdense = _peer_gate_matrix(idx_t, g_t, tb=128)
        yf = _peer_experts(x1b, peer_u[i].T.astype(BF16), peer_v[i].astype(BF16), wdense, tm=1024, ce=512)
        x2, xb = _residual_ln(x1, yf, ple, ln2_g[i][None], ln2_b[i][None], alpha=alpha, tm=512)
    return x2.reshape(bsz, seq, dm)
```

```python
import functools
import math

import jax
import jax.numpy as jnp
import numpy as np
from jax import lax
from jax.experimental import pallas as pl
from jax.experimental.pallas import tpu as pltpu

F32, BF16, I32 = jnp.float32, jnp.bfloat16, jnp.int32

V7X_VMEM_BYTES = 64 * 2**20
V7X_LANES = 128
V7X_SUBLANES = 8
VMEM_HEADROOM_BYTES = 8 * 2**20

A_HEADS = 8
A_HALF = 64
A_VDIM = 2 * A_HALF
LRU_BLOCKS = 8
CONV_WIDTH = 4
LRU_C = 8.0
C_HEADS = 8
C_HDIM = 128
N_BRANCH = 3
REL_BUCKETS = 32
REL_MAX_DIST = 128
PEER_HEADS = 8
PEER_NKEYS = 128
PEER_HALF = 64
PEER_TOPK = 16
LN_EPS = 1e-5

LOG2E = 1.4426950408889634
MASKED = -1e30


def _vmem_limit(*block_bytes, scratch_bytes=0):
    need = 2 * sum(block_bytes) + scratch_bytes + VMEM_HEADROOM_BYTES
    return int(min(max(need, 32 * 2**20), V7X_VMEM_BYTES - 4 * 2**20))


def _nbytes(shape, dtype):
    return int(np.prod(shape)) * jnp.dtype(dtype).itemsize


def _log_sigmoid(z):
    return jnp.minimum(z, 0.0) - jnp.log1p(jnp.exp(-jnp.abs(z)))


def _layer_norm_rows(y, g, b):
    mu = jnp.mean(y, axis=1, keepdims=True)
    yc = y - mu
    var = jnp.mean(yc * yc, axis=1, keepdims=True)
    return yc * lax.rsqrt(var + LN_EPS) * g + b


def _proj_kernel(x_ref, w_ref, b_ref, s_ref, o_ref, *, act):
    y = jnp.dot(x_ref[...], w_ref[...], preferred_element_type=F32)
    y = (y + b_ref[...]) * s_ref[...]
    if act == "sigmoid":
        y = jax.nn.sigmoid(y)
    o_ref[...] = y.astype(o_ref.dtype)


def _proj(x, w, b, s, *, out_dtype, tm, tn, act=None, name):
    m, k = x.shape
    n = w.shape[1]
    grid = (n // tn, m // tm)
    return pl.pallas_call(
        functools.partial(_proj_kernel, act=act),
        out_shape=jax.ShapeDtypeStruct((m, n), out_dtype),
        grid=grid,
        in_specs=[
            pl.BlockSpec((tm, k), lambda j, i: (i, 0)),
            pl.BlockSpec((k, tn), lambda j, i: (0, j)),
            pl.BlockSpec((1, tn), lambda j, i: (0, j)),
            pl.BlockSpec((1, tn), lambda j, i: (0, j)),
        ],
        out_specs=pl.BlockSpec((tm, tn), lambda j, i: (i, j)),
        compiler_params=pltpu.CompilerParams(
            dimension_semantics=("parallel", "parallel"),
            vmem_limit_bytes=_vmem_limit(_nbytes((tm, k), x.dtype), _nbytes((k, tn), w.dtype),
                                         _nbytes((tm, tn), out_dtype)),
        ),
        name=name,
    )(x, w, b, s)


def _proj_t_kernel(w_ref, x_ref, b_ref, o_ref):
    y = lax.dot_general(w_ref[...], x_ref[...], (((1,), (1,)), ((), ())), preferred_element_type=F32)
    o_ref[...] = (y + b_ref[...]).astype(o_ref.dtype)


def _proj_t(x, w_t, b_col, *, tm, tn, name):
    m, k = x.shape
    n = w_t.shape[0]
    return pl.pallas_call(
        _proj_t_kernel,
        out_shape=jax.ShapeDtypeStruct((m // tm, n, tm), BF16),
        grid=(n // tn, m // tm),
        in_specs=[
            pl.BlockSpec((tn, k), lambda j, i: (j, 0)),
            pl.BlockSpec((tm, k), lambda j, i: (i, 0)),
            pl.BlockSpec((tn, 1), lambda j, i: (j, 0)),
        ],
        out_specs=pl.BlockSpec((None, tn, tm), lambda j, i: (i, j, 0)),
        compiler_params=pltpu.CompilerParams(
            dimension_semantics=("parallel", "parallel"),
            vmem_limit_bytes=_vmem_limit(_nbytes((tm, k), BF16), _nbytes((tn, k), BF16), _nbytes((tn, tm), BF16)),
        ),
        name=name,
    )(w_t, x, b_col)


def _fgate_kernel(x_ref, w_ref, b_ref, o_ref, carry_ref, *, ts):
    @pl.when(pl.program_id(1) == 0)
    def _():
        carry_ref[...] = jnp.zeros_like(carry_ref)

    z = lax.dot_general(w_ref[...], x_ref[...], (((1,), (1,)), ((), ())),
                        preferred_element_type=F32) + b_ref[...]
    c = _log_sigmoid(z)
    lane = lax.broadcasted_iota(I32, c.shape, 1)
    d = 1
    while d < ts:
        c = c + jnp.where(lane >= d, pltpu.roll(c, d, 1), 0.0)
        d *= 2
    c = c + carry_ref[:, 0:1]
    carry_ref[...] = jnp.broadcast_to(c[:, ts - 1:ts], carry_ref.shape)
    c = c * LOG2E
    for h in range(C_HEADS):
        o_ref[h] = jnp.broadcast_to(c[h:h + 1, :], (V7X_LANES, ts)).T


def _fgate_cumsum(x3, wf_t, bf, *, ts):
    bsz, seq, dm = x3.shape
    return pl.pallas_call(
        functools.partial(_fgate_kernel, ts=ts),
        out_shape=jax.ShapeDtypeStruct((bsz, C_HEADS, seq, V7X_LANES), F32),
        grid=(bsz, seq // ts),
        in_specs=[
            pl.BlockSpec((None, ts, dm), lambda b, i: (b, i, 0)),
            pl.BlockSpec((C_HEADS, dm), lambda b, i: (0, 0)),
            pl.BlockSpec((C_HEADS, 1), lambda b, i: (0, 0)),
        ],
        out_specs=pl.BlockSpec((None, C_HEADS, ts, V7X_LANES), lambda b, i: (b, 0, i, 0)),
        scratch_shapes=[pltpu.VMEM((C_HEADS, V7X_LANES), F32)],
        compiler_params=pltpu.CompilerParams(dimension_semantics=("parallel", "arbitrary")),
        name="fgate_cumsum",
    )(x3, wf_t, bf)


def _rel_bucket_tiles(t):
    q = np.arange(t)[None, :]
    k = np.arange(t)[:, None]
    out = []
    for off in (0, t):
        rel = q - k + off
        n = np.maximum(rel, 0)
        max_exact = REL_BUCKETS // 2
        nf = np.maximum(n, 1).astype(np.float32)
        large = max_exact + (np.log(nf / np.float32(max_exact)) / np.float32(math.log(REL_MAX_DIST / max_exact))
                             * np.float32(REL_BUCKETS - max_exact)).astype(np.int32)
        large = np.minimum(large, REL_BUCKETS - 1)
        bkt = np.where(n < max_exact, n, large)
        out.append(np.where(rel >= 0, bkt, -1))
    return np.stack(out).astype(np.int32)


def _bias_kernel(rel_ref, bkt_ref, o_ref):
    h = pl.program_id(0)
    bkt = bkt_ref[...]
    far = rel_ref[REL_BUCKETS - 1, h]
    acc = jnp.zeros(bkt.shape, F32)
    for b in range(REL_BUCKETS):
        acc = jnp.where(bkt == b, rel_ref[b, h] - far, acc)
    o_ref[...] = jnp.where(bkt < 0, MASKED, acc * LOG2E)


def _bias_tiles(rel_bias, t):
    bkt = jnp.asarray(_rel_bucket_tiles(t))
    return pl.pallas_call(
        _bias_kernel,
        out_shape=jax.ShapeDtypeStruct((A_HEADS, 2, t, t), F32),
        grid=(A_HEADS,),
        in_specs=[
            pl.BlockSpec(memory_space=pltpu.SMEM),
            pl.BlockSpec((2, t, t), lambda h: (0, 0, 0)),
        ],
        out_specs=pl.BlockSpec((None, 2, t, t), lambda h: (h, 0, 0, 0)),
        name="rel_bias_tiles",
    )(rel_bias, bkt)


HEADS_PER_STEP = 2


def _sublane_allreduce(x, op):
    for shift in (1, 2, 4):
        x = op(x, pltpu.roll(x, shift, 0))
    return x


def _rows3(x):
    return x.reshape(x.shape[0] // V7X_SUBLANES, V7X_SUBLANES, x.shape[1])


def _attn_scratch(hp, dv, t, c):
    stat = pltpu.VMEM((hp, V7X_SUBLANES, c), F32)
    return [stat, stat, stat, pltpu.VMEM((hp, dv, c), F32), pltpu.VMEM((hp, t, c), BF16)]


def _attn_init(m_sc, l_sc, a_sc, acc_sc, p_sc):
    m_sc[...] = jnp.full_like(m_sc, MASKED)
    l_sc[...] = jnp.zeros_like(l_sc)
    a_sc[...] = jnp.ones_like(a_sc)
    acc_sc[...] = jnp.zeros_like(acc_sc)
    p_sc[...] = jnp.zeros_like(p_sc)


def _attn_fold(vt_prev, h, a_sc, acc_sc, p_sc):
    pv = jnp.dot(vt_prev, p_sc[h], preferred_element_type=F32)
    acc_sc[h] = (_rows3(acc_sc[h]) * a_sc[h][None]).reshape(pv.shape) + pv


def _attn_stage(s, vt_prev, h, m_sc, l_sc, a_sc, acc_sc, p_sc):
    _attn_fold(vt_prev, h, a_sc, acc_sc, p_sc)
    s3 = _rows3(s)
    m_prev = m_sc[h]
    m_new = jnp.maximum(m_prev, _sublane_allreduce(jnp.max(s3, axis=0), jnp.maximum))
    alpha = jnp.exp2(m_prev - m_new)
    p3 = jnp.exp2(s3 - m_new[None])
    l_sc[h] = alpha * l_sc[h] + jnp.sum(p3, axis=0)
    p_sc[h] = p3.reshape(s.shape).astype(BF16)
    a_sc[h] = alpha
    m_sc[h] = m_new


def _diff_attn_kernel(lam_ref, subln_ref, q_ref, k_ref, vt_ref, bias_ref, o_ref,
                      m_sc, l_sc, a_sc, acc_sc, p_sc, *, t, lam_init):
    qi = pl.program_id(2)
    state = (m_sc, l_sc, a_sc, acc_sc, p_sc)
    _attn_init(*state)

    qqs = []
    for h in range(HEADS_PER_STEP):
        q = q_ref[:, h * A_VDIM:(h + 1) * A_VDIM]
        lane = lax.broadcasted_iota(I32, q.shape, 1)
        zero = jnp.zeros_like(q)
        qqs.append(jnp.concatenate([jnp.where(lane < A_HALF, q, zero), jnp.where(lane >= A_HALF, q, zero)], axis=0))

    def tile(kj, near):
        start = pl.multiple_of(kj * t, t)
        prev = jnp.maximum(kj - 1, 0)
        for h in range(HEADS_PER_STEP):
            cols = slice(h * A_VDIM, (h + 1) * A_VDIM)
            s = lax.dot_general(k_ref[pl.ds(start, t), cols], qqs[h], (((1,), (1,)), ((), ())),
                                preferred_element_type=F32)
            if near is not None:
                bias = bias_ref[h, near]
                s = s + jnp.concatenate([bias, bias], axis=1)
            _attn_stage(s, vt_ref[prev, cols, :], h, *state)

    def far_body(kj, carry):
        tile(kj, None)
        return carry

    lax.fori_loop(0, jnp.maximum(qi - 1, 0), far_body, 0)

    @pl.when(qi >= 1)
    def _():
        tile(qi - 1, 1)

    tile(qi, 0)

    lv = lam_ref[...]
    lam = (jnp.exp(jnp.sum(lv[0:1] * lv[1:2], axis=1, keepdims=True))
           - jnp.exp(jnp.sum(lv[2:3] * lv[3:4], axis=1, keepdims=True)) + lam_init)
    for h in range(HEADS_PER_STEP):
        _attn_fold(vt_ref[qi, h * A_VDIM:(h + 1) * A_VDIM, :], h, a_sc, acc_sc, p_sc)
        l = _sublane_allreduce(l_sc[h], jnp.add)
        r = (_rows3(acc_sc[h]) / l[None]).reshape(A_VDIM, 2 * t)
        o = r[:, :t] - lam * r[:, t:]
        ms = _sublane_allreduce(jnp.sum(_rows3(o * o), axis=0), jnp.add) * (1.0 / A_VDIM)
        y = (_rows3(o) * lax.rsqrt(ms + LN_EPS)[None]).reshape(A_VDIM, t) * (subln_ref[...] * (1.0 - lam_init))
        o_ref[:, h * A_VDIM:(h + 1) * A_VDIM] = y.T.astype(o_ref.dtype)


def _diff_attention(qk3, vt4, lam4, subln_col, bias, *, t, lam_init):
    bsz, seq, _ = qk3.shape
    nq = seq // t
    hp = HEADS_PER_STEP
    wid = hp * A_VDIM
    kblk = A_HEADS // hp
    return pl.pallas_call(
        functools.partial(_diff_attn_kernel, t=t, lam_init=lam_init),
        out_shape=jax.ShapeDtypeStruct((bsz, seq, A_HEADS * A_VDIM), BF16),
        grid=(bsz, A_HEADS // hp, nq),
        in_specs=[
            pl.BlockSpec((4, A_HALF), lambda b, h, i: (0, 0)),
            pl.BlockSpec((A_VDIM, 1), lambda b, h, i: (0, 0)),
            pl.BlockSpec((None, t, wid), lambda b, h, i: (b, i, h)),
            pl.BlockSpec((None, seq, wid), lambda b, h, i: (b, 0, kblk + h)),
            pl.BlockSpec((None, nq, wid, t), lambda b, h, i: (b, 0, h, 0)),
            pl.BlockSpec((hp, 2, t, t), lambda b, h, i: (h, 0, 0, 0)),
        ],
        out_specs=pl.BlockSpec((None, t, wid), lambda b, h, i: (b, i, h)),
        scratch_shapes=_attn_scratch(hp, A_VDIM, t, 2 * t),
        compiler_params=pltpu.CompilerParams(
            dimension_semantics=("parallel", "parallel", "arbitrary"),
            vmem_limit_bytes=_vmem_limit(2 * _nbytes((seq, wid), BF16), _nbytes((hp, 2, t, t), F32),
                                         scratch_bytes=_nbytes((hp, A_VDIM + 2 * t, 2 * t), F32)),
        ),
        name="diff_attention",
    )(lam4, subln_col, qk3, qk3, vt4, bias)


def _fox_attn_kernel(q_ref, k_ref, vt_ref, ck_ref, o_ref, m_sc, l_sc, a_sc, acc_sc, p_sc, *, t):
    qi = pl.program_id(2)
    state = (m_sc, l_sc, a_sc, acc_sc, p_sc)
    _attn_init(*state)
    qs = [q_ref[:, h * C_HDIM:(h + 1) * C_HDIM] for h in range(HEADS_PER_STEP)]

    def tile(kj, diagonal):
        start = pl.multiple_of(kj * t, t)
        prev = jnp.maximum(kj - 1, 0)
        for h in range(HEADS_PER_STEP):
            cols = slice(h * C_HDIM, (h + 1) * C_HDIM)
            s = lax.dot_general(k_ref[pl.ds(start, t), cols], qs[h], (((1,), (1,)), ((), ())),
                                preferred_element_type=F32)
            ck = ck_ref[h, pl.ds(start, t), :]
            s = s - jnp.concatenate([ck] * (t // V7X_LANES), axis=1)
            if diagonal:
                key = lax.broadcasted_iota(I32, s.shape, 0)
                qry = lax.broadcasted_iota(I32, s.shape, 1)
                s = jnp.where(key <= qry, s, MASKED)
            _attn_stage(s, vt_ref[prev, cols, :], h, *state)

    def body(kj, carry):
        tile(kj, False)
        return carry

    lax.fori_loop(0, qi, body, 0)
    tile(qi, True)
    for h in range(HEADS_PER_STEP):
        _attn_fold(vt_ref[qi, h * C_HDIM:(h + 1) * C_HDIM, :], h, a_sc, acc_sc, p_sc)
        l = _sublane_allreduce(l_sc[h], jnp.add)
        o = (_rows3(acc_sc[h]) / l[None]).reshape(C_HDIM, t)
        o_ref[:, h * C_HDIM:(h + 1) * C_HDIM] = o.T.astype(o_ref.dtype)


def _fox_attention(qk3, vt4, ckb, *, t):
    bsz, seq, _ = qk3.shape
    nq = seq // t
    hp = HEADS_PER_STEP
    wid = hp * C_HDIM
    qblk = 2 * (A_HEADS // hp)
    kblk = qblk + C_HEADS // hp
    vblk = A_HEADS // hp
    return pl.pallas_call(
        functools.partial(_fox_attn_kernel, t=t),
        out_shape=jax.ShapeDtypeStruct((bsz, seq, C_HEADS * C_HDIM), BF16),
        grid=(bsz, C_HEADS // hp, nq),
        in_specs=[
            pl.BlockSpec((None, t, wid), lambda b, h, i: (b, i, qblk + h)),
            pl.BlockSpec((None, seq, wid), lambda b, h, i: (b, 0, kblk + h)),
            pl.BlockSpec((None, nq, wid, t), lambda b, h, i: (b, 0, vblk + h, 0)),
            pl.BlockSpec((None, hp, seq, V7X_LANES), lambda b, h, i: (b, h, 0, 0)),
        ],
        out_specs=pl.BlockSpec((None, t, wid), lambda b, h, i: (b, i, h)),
        scratch_shapes=_attn_scratch(hp, C_HDIM, t, t),
        compiler_params=pltpu.CompilerParams(
            dimension_semantics=("parallel", "parallel", "arbitrary"),
            vmem_limit_bytes=_vmem_limit(2 * _nbytes((seq, wid), BF16), _nbytes((hp, seq, V7X_LANES), F32),
                                         scratch_bytes=_nbytes((hp, C_HDIM + 2 * t, t), F32)),
        ),
        name="fox_attention",
    )(qk3, qk3, vt4, ckb)


def _lru_kernel(bx_ref, bg_ref, cw_ref, cb_ref, wax_ref, bax_ref, lam_ref, o_ref, xbuf, hprev, *, t):
    pad = V7X_SUBLANES

    @pl.when(pl.program_id(1) == 0)
    def _():
        xbuf[0:pad, :] = jnp.zeros((pad, xbuf.shape[1]), F32)
        hprev[...] = jnp.zeros_like(hprev)

    xbuf[pad:pad + t, :] = bx_ref[...]
    xc = cb_ref[...]
    for tap in range(CONV_WIDTH):
        xc = xc + xbuf[pl.ds(pad - (CONV_WIDTH - 1) + tap, t), :] * cw_ref[tap:tap + 1, :]
    xbuf[0:pad, :] = bx_ref[t - pad:t, :]

    bw = xc.shape[1] // LRU_BLOCKS
    row = lax.broadcasted_iota(I32, (t, bw), 0)
    for g in range(LRU_BLOCKS):
        cols = slice(g * bw, (g + 1) * bw)
        xg = xc[:, cols]
        z = jnp.dot(xg.astype(BF16), wax_ref[g], preferred_element_type=F32) + bax_ref[g]
        r = jax.nn.sigmoid(z[:, :bw])
        gi = jax.nn.sigmoid(z[:, bw:])
        log_a = (LRU_C * r) * _log_sigmoid(lam_ref[:, cols])
        a = jnp.exp(log_a)
        u = jnp.sqrt(1.0 - jnp.exp(2.0 * log_a)) * (gi * xg)
        d = 1
        while d < t:
            keep = row >= d
            a_sh = jnp.where(keep, pltpu.roll(a, d, 0), 1.0)
            u_sh = jnp.where(keep, pltpu.roll(u, d, 0), 0.0)
            u = a * u_sh + u
            a = a * a_sh
            d *= 2
        h = a * hprev[0:1, cols] + u
        hprev[0:1, cols] = h[t - 1:t, :]
        o_ref[:, cols] = (jax.nn.gelu(bg_ref[:, cols], approximate=True) * h).astype(o_ref.dtype)


def _lru_branch(bxg, conv_w, conv_b, wax, bax, lam, *, bsz, t):
    n, two_w = bxg.shape
    w = two_w // 2
    nt = n // bsz // t
    bw = w // LRU_BLOCKS
    return pl.pallas_call(
        functools.partial(_lru_kernel, t=t),
        out_shape=jax.ShapeDtypeStruct((n, w), BF16),
        grid=(bsz, nt),
        in_specs=[
            pl.BlockSpec((t, w), lambda b, i: (b * nt + i, 0)),
            pl.BlockSpec((t, w), lambda b, i: (b * nt + i, 1)),
            pl.BlockSpec((CONV_WIDTH, w), lambda b, i: (0, 0)),
            pl.BlockSpec((1, w), lambda b, i: (0, 0)),
            pl.BlockSpec((LRU_BLOCKS, bw, 2 * bw), lambda b, i: (0, 0, 0)),
            pl.BlockSpec((LRU_BLOCKS, 1, 2 * bw), lambda b, i: (0, 0, 0)),
            pl.BlockSpec((1, w), lambda b, i: (0, 0)),
        ],
        out_specs=pl.BlockSpec((t, w), lambda b, i: (b * nt + i, 0)),
        scratch_shapes=[
            pltpu.VMEM((t + V7X_SUBLANES, w), F32),
            pltpu.VMEM((V7X_SUBLANES, w), F32),
        ],
        compiler_params=pltpu.CompilerParams(dimension_semantics=("parallel", "arbitrary")),
        name="conv_rglru",
    )(bxg, bxg, conv_w, conv_b, wax, bax, lam)


def _merge_kernel(ya_ref, yb_ref, yc_ref, w_ref, g0_ref, g1_ref, g2_ref, o_ref):
    acc = g0_ref[...].astype(F32) * jnp.dot(ya_ref[...], w_ref[0], preferred_element_type=F32)
    acc = acc + g1_ref[...].astype(F32) * jnp.dot(yb_ref[...], w_ref[1], preferred_element_type=F32)
    acc = acc + g2_ref[...].astype(F32) * jnp.dot(yc_ref[...], w_ref[2], preferred_element_type=F32)
    o_ref[...] = acc.astype(o_ref.dtype)


def _merge(ya, yb, yc, wb, gates, *, tm, tn):
    n, bwid = ya.shape
    dm = wb.shape[2]
    nc = dm // tn
    y_spec = pl.BlockSpec((tm, bwid), lambda j, i: (i, 0))
    return pl.pallas_call(
        _merge_kernel,
        out_shape=jax.ShapeDtypeStruct((n, dm), BF16),
        grid=(nc, n // tm),
        in_specs=[
            y_spec, y_spec, y_spec,
            pl.BlockSpec((N_BRANCH, bwid, tn), lambda j, i: (0, 0, j)),
            pl.BlockSpec((tm, tn), lambda j, i: (i, j)),
            pl.BlockSpec((tm, tn), lambda j, i: (i, nc + j)),
            pl.BlockSpec((tm, tn), lambda j, i: (i, 2 * nc + j)),
        ],
        out_specs=pl.BlockSpec((tm, tn), lambda j, i: (i, j)),
        compiler_params=pltpu.CompilerParams(
            dimension_semantics=("parallel", "parallel"),
            vmem_limit_bytes=_vmem_limit(3 * _nbytes((tm, bwid), BF16), _nbytes((N_BRANCH, bwid, tn), BF16),
                                         4 * _nbytes((tm, tn), BF16)),
        ),
        name="gated_merge",
    )(ya, yb, yc, wb, gates, gates, gates)


def _wo_ln_kernel(m_ref, w_ref, x_ref, g_ref, b_ref, o32_ref, o16_ref, *, alpha):
    y = jnp.dot(m_ref[...], w_ref[...], preferred_element_type=F32) + alpha * x_ref[...]
    out = _layer_norm_rows(y, g_ref[...], b_ref[...])
    o32_ref[...] = out
    o16_ref[...] = out.astype(BF16)


def _wo_ln(merged, wo, x, g, b, *, alpha, tm):
    n, dm = x.shape
    row = pl.BlockSpec((tm, dm), lambda i: (i, 0))
    vec = pl.BlockSpec((1, dm), lambda i: (0, 0))
    return pl.pallas_call(
        functools.partial(_wo_ln_kernel, alpha=alpha),
        out_shape=(jax.ShapeDtypeStruct((n, dm), F32), jax.ShapeDtypeStruct((n, dm), BF16)),
        grid=(n // tm,),
        in_specs=[row, pl.BlockSpec((dm, dm), lambda i: (0, 0)), row, vec, vec],
        out_specs=(row, row),
        compiler_params=pltpu.CompilerParams(
            dimension_semantics=("parallel",),
            vmem_limit_bytes=_vmem_limit(_nbytes((tm, dm), BF16), _nbytes((dm, dm), BF16),
                                         2 * _nbytes((tm, dm), F32), _nbytes((tm, dm), BF16)),
        ),
        name="wo_layernorm",
    )(merged, wo, x, g, b)


def _ple_kernel(x_ref, wg_ref, bg_ref, p_ref, we_ref, o_ref):
    gate = jax.nn.sigmoid(jnp.dot(x_ref[...], wg_ref[...], preferred_element_type=F32) + bg_ref[...])
    o_ref[...] = gate * jnp.dot(p_ref[...], we_ref[...], preferred_element_type=F32)


def _ple(x1b, wg, bg, pb, we, *, tm, tn):
    n, dm = x1b.shape
    pdim = pb.shape[1]
    return pl.pallas_call(
        _ple_kernel,
        out_shape=jax.ShapeDtypeStruct((n, dm), F32),
        grid=(dm // tn, n // tm),
        in_specs=[
            pl.BlockSpec((tm, dm), lambda j, i: (i, 0)),
            pl.BlockSpec((dm, tn), lambda j, i: (0, j)),
            pl.BlockSpec((1, tn), lambda j, i: (0, j)),
            pl.BlockSpec((tm, pdim), lambda j, i: (i, 0)),
            pl.BlockSpec((pdim, tn), lambda j, i: (0, j)),
        ],
        out_specs=pl.BlockSpec((tm, tn), lambda j, i: (i, j)),
        compiler_params=pltpu.CompilerParams(
            dimension_semantics=("parallel", "parallel"),
            vmem_limit_bytes=_vmem_limit(_nbytes((tm, dm), BF16), _nbytes((dm, tn), BF16),
                                         _nbytes((tm, tn), F32)),
        ),
        name="gated_ple",
    )(x1b, wg, bg, pb, we)


def _top_rows(s, payload, k):
    rows = s.shape[0]
    rid = lax.broadcasted_iota(I32, s.shape, 0)
    vals, pays = [], []
    for _ in range(k):
        m = jnp.max(s, axis=0, keepdims=True)
        pos = jnp.min(jnp.where(s == m, rid, rows), axis=0, keepdims=True)
        hit = rid == pos
        vals.append(m)
        pays.append(pos if payload is None else jnp.max(jnp.where(hit, payload, -1), axis=0, keepdims=True))
        s = jnp.where(hit, -jnp.inf, s)
    return jnp.concatenate(vals, axis=0), jnp.concatenate(pays, axis=0)


def _peer_route_kernel(x_ref, wq_ref, keys_ref, idx_ref, g_ref):
    q = jnp.dot(x_ref[...], wq_ref[...], preferred_element_type=F32).astype(BF16)
    st = lax.dot_general(keys_ref[...], q, (((1,), (1,)), ((), ())), preferred_element_type=F32)
    s1, i1 = _top_rows(st[:PEER_NKEYS], None, PEER_TOPK)
    s2, i2 = _top_rows(st[PEER_NKEYS:], None, PEER_TOPK)
    cand, cidx = [], []
    for a in range(PEER_TOPK):
        nb = PEER_TOPK // (a + 1)
        cand.append(s1[a:a + 1] + s2[:nb])
        cidx.append(i1[a:a + 1] * PEER_NKEYS + i2[:nb])
    npairs = sum(c.shape[0] for c in cand)
    npad = -npairs % V7X_SUBLANES
    cand.append(jnp.full((npad, st.shape[1]), -jnp.inf, F32))
    cidx.append(jnp.full((npad, st.shape[1]), -1, I32))
    sc, idx = _top_rows(jnp.concatenate(cand, axis=0), jnp.concatenate(cidx, axis=0), PEER_TOPK)
    e = jnp.exp(sc - jnp.max(sc, axis=0, keepdims=True))
    idx_ref[...] = idx
    g_ref[...] = e / jnp.sum(e, axis=0, keepdims=True)


def _peer_route(x1b, wq, keys_t, *, tm):
    n, dm = x1b.shape
    qd = 2 * PEER_HALF
    slots = PEER_HEADS * PEER_TOPK
    out_spec = pl.BlockSpec((PEER_TOPK, tm), lambda i, h: (h, i))
    return pl.pallas_call(
        _peer_route_kernel,
        out_shape=(jax.ShapeDtypeStruct((slots, n), I32), jax.ShapeDtypeStruct((slots, n), F32)),
        grid=(n // tm, PEER_HEADS),
        in_specs=[
            pl.BlockSpec((tm, dm), lambda i, h: (i, 0)),
            pl.BlockSpec((dm, qd), lambda i, h: (0, h)),
            pl.BlockSpec((None, 2 * PEER_NKEYS, qd), lambda i, h: (h, 0, 0)),
        ],
        out_specs=(out_spec, out_spec),
        compiler_params=pltpu.CompilerParams(dimension_semantics=("parallel", "parallel")),
        name="peer_route",
    )(x1b, wq, keys_t)


W_ROW_PITCH = PEER_NKEYS + V7X_SUBLANES


def _peer_gate_kernel(idx_ref, g_ref, o_ref, wbuf, idx_sc, g_sc, *, tb):
    idx_sc[...] = idx_ref[...].T
    g_sc[...] = g_ref[...].T
    rid = lax.broadcasted_iota(I32, (PEER_NKEYS, PEER_NKEYS), 0).astype(F32).astype(BF16)
    one = jnp.ones((PEER_NKEYS, PEER_NKEYS), BF16)
    zero = jnp.zeros((PEER_NKEYS, PEER_NKEYS), BF16)
    group = 2 * V7X_SUBLANES
    packed_rows = 2 * V7X_SUBLANES

    def body(j, carry):
        base = pl.multiple_of(j * group, group)
        ib = idx_sc[pl.ds(base, group), :]
        gb = g_sc[pl.ds(base, group), :]
        i1b = (ib >> 7).astype(F32)
        i2b = (ib & (PEER_NKEYS - 1)).astype(F32)
        for r in range(group):
            def rep(v):
                one_vreg = jnp.broadcast_to(v[r:r + 1, :], (packed_rows, PEER_NKEYS)).astype(BF16)
                return jnp.concatenate([one_vreg] * (PEER_NKEYS // packed_rows), axis=0)
            r1 = jnp.where(rid == rep(i1b), rep(gb), zero)
            r2t = jnp.where(rid == rep(i2b), one, zero)
            wt = lax.dot_general(r1, r2t, (((1,), (1,)), ((), ())), preferred_element_type=F32)
            wbuf[pl.ds(pl.multiple_of((base + r) * W_ROW_PITCH, V7X_SUBLANES), PEER_NKEYS), :] = wt
        return carry

    lax.fori_loop(0, tb // group, body, 0)
    for i1 in range(PEER_NKEYS):
        o_ref[:, i1 * PEER_NKEYS:(i1 + 1) * PEER_NKEYS] = (
            wbuf[pl.ds(i1, tb, stride=W_ROW_PITCH), :].astype(o_ref.dtype))


def _peer_gate_matrix(idx_t, g_t, *, tb):
    slots, n = idx_t.shape
    ne = PEER_NKEYS * PEER_NKEYS
    in_spec = pl.BlockSpec((slots, tb), lambda i: (0, i))
    return pl.pallas_call(
        functools.partial(_peer_gate_kernel, tb=tb),
        out_shape=jax.ShapeDtypeStruct((n, ne), BF16),
        grid=(n // tb,),
        in_specs=[in_spec, in_spec],
        out_specs=pl.BlockSpec((tb, ne), lambda i: (i, 0)),
        scratch_shapes=[
            pltpu.VMEM((tb * W_ROW_PITCH, PEER_NKEYS), F32),
            pltpu.VMEM((tb, slots), I32),
            pltpu.VMEM((tb, slots), F32),
        ],
        compiler_params=pltpu.CompilerParams(
            dimension_semantics=("parallel",),
            vmem_limit_bytes=_vmem_limit(_nbytes((tb, ne), BF16),
                                         scratch_bytes=_nbytes((tb * W_ROW_PITCH, PEER_NKEYS), F32)),
        ),
        name="peer_gate_matrix",
    )(idx_t, g_t)


def _peer_expert_kernel(x_ref, ut_ref, v_ref, w_ref, o_ref):
    @pl.when(pl.program_id(1) == 0)
    def _():
        o_ref[...] = jnp.zeros_like(o_ref)

    a = jnp.dot(x_ref[...], ut_ref[...], preferred_element_type=F32)
    gelu = 0.5 * a * (1.0 + lax.erf(a * (0.5 ** 0.5)))
    g = (w_ref[...].astype(F32) * gelu).astype(BF16)
    o_ref[...] += jnp.dot(g, v_ref[...], preferred_element_type=F32)


def _peer_experts(x1b, ut, v, w, *, tm, ce):
    n, dm = x1b.shape
    ne = v.shape[0]
    return pl.pallas_call(
        _peer_expert_kernel,
        out_shape=jax.ShapeDtypeStruct((n, dm), F32),
        grid=(n // tm, ne // ce),
        in_specs=[
            pl.BlockSpec((tm, dm), lambda i, j: (i, 0)),
            pl.BlockSpec((dm, ce), lambda i, j: (0, j)),
            pl.BlockSpec((ce, dm), lambda i, j: (j, 0)),
            pl.BlockSpec((tm, ce), lambda i, j: (i, j)),
        ],
        out_specs=pl.BlockSpec((tm, dm), lambda i, j: (i, 0)),
        compiler_params=pltpu.CompilerParams(
            dimension_semantics=("parallel", "arbitrary"),
            vmem_limit_bytes=_vmem_limit(_nbytes((tm, dm), BF16), 2 * _nbytes((dm, ce), BF16),
                                         _nbytes((tm, ce), BF16), _nbytes((tm, dm), F32)),
        ),
        name="peer_experts",
    )(x1b, ut, v, w)


def _residual_ln_kernel(x_ref, y_ref, p_ref, g_ref, b_ref, o32_ref, o16_ref, *, alpha):
    out = _layer_norm_rows(alpha * x_ref[...] + y_ref[...] + p_ref[...], g_ref[...], b_ref[...])
    o32_ref[...] = out
    o16_ref[...] = out.astype(BF16)


def _residual_ln(x1, yf, ple, g, b, *, alpha, tm):
    n, dm = x1.shape
    row = pl.BlockSpec((tm, dm), lambda i: (i, 0))
    vec = pl.BlockSpec((1, dm), lambda i: (0, 0))
    return pl.pallas_call(
        functools.partial(_residual_ln_kernel, alpha=alpha),
        out_shape=(jax.ShapeDtypeStruct((n, dm), F32), jax.ShapeDtypeStruct((n, dm), BF16)),
        grid=(n // tm,),
        in_specs=[row, row, row, vec, vec],
        out_specs=(row, row),
        compiler_params=pltpu.CompilerParams(
            dimension_semantics=("parallel",),
            vmem_limit_bytes=_vmem_limit(4 * _nbytes((tm, dm), F32), _nbytes((tm, dm), BF16)),
        ),
        name="residual_layernorm",
    )(x1, yf, ple, g, b)


def _peer_keys_layout(keys):
    z = jnp.zeros_like(keys[:, 0])
    top = jnp.concatenate([keys[:, 0], z], axis=-1)
    bot = jnp.concatenate([z, keys[:, 1]], axis=-1)
    return jnp.concatenate([top, bot], axis=1).astype(BF16)


def kernel(x, p, w_in, b_in, diff_lambda, diff_subln, rel_bias, conv_w, conv_b, lru_wa, lru_ba, lru_wx,
           lru_bx, lru_lambda, w_branch, w_o, ln1_g, ln1_b, peer_wq, peer_keys, peer_u, peer_v, w_ple,
           w_ple_gate, b_ple_gate, ln2_g, ln2_b):
    bsz, seq, dm = x.shape
    depth = w_in.shape[0]
    n = bsz * seq
    alpha = (2 * depth) ** 0.25
    bwid = A_HEADS * A_VDIM
    t_attn = 512
    assert seq % t_attn == 0 and t_attn >= REL_MAX_DIST

    o_ka, o_va, o_lru, o_c, o_kc, o_vc, o_f = (j * bwid for j in (1, 2, 3, 5, 6, 7, 8))
    o_g = o_f + C_HEADS
    qk_scale = jnp.concatenate([
        jnp.full((bwid,), A_HALF ** -0.5 * LOG2E, F32), jnp.ones((bwid,), F32),
        jnp.full((bwid,), C_HDIM ** -0.5 * LOG2E, F32), jnp.ones((bwid,), F32)])[None]
    ones_row = jnp.ones((1, max(2 * bwid, N_BRANCH * dm)), F32)

    bias = _bias_tiles(rel_bias, t_attn)
    x2 = x.reshape(n, dm)
    xb = x2.astype(BF16)
    for i in range(depth):
        w = w_in[i]
        b = b_in[i]
        w_qk = jnp.concatenate([w[:, :o_va], w[:, o_c:o_vc]], axis=1).astype(BF16)
        b_qk = jnp.concatenate([b[:o_va], b[o_c:o_vc]])[None]
        qk = _proj(xb, w_qk, b_qk, qk_scale, out_dtype=BF16, tm=1024, tn=2048, name="proj_qk")
        w_v = jnp.concatenate([w[:, o_va:o_lru], w[:, o_vc:o_f]], axis=1).T.astype(BF16)
        b_v = jnp.concatenate([b[o_va:o_lru], b[o_vc:o_f]])[:, None]
        vt = _proj_t(xb, w_v, b_v, tm=t_attn, tn=1024, name="proj_v_t")
        bxg = _proj(xb, w[:, o_lru:o_c].astype(BF16), b[None, o_lru:o_c], ones_row[:, :2 * bwid],
                    out_dtype=F32, tm=1024, tn=1024, name="proj_lru")
        gates = _proj(xb, w[:, o_g:].astype(BF16), b[None, o_g:], ones_row[:, :N_BRANCH * dm],
                      out_dtype=BF16, tm=1024, tn=2048, act="sigmoid", name="proj_gates")
        ckb = _fgate_cumsum(xb.reshape(bsz, seq, dm), w[:, o_f:o_g].T.astype(BF16), b[o_f:o_g, None],
                            ts=min(seq, 1024))

        qk3 = qk.reshape(bsz, seq, 4 * bwid)
        vt4 = vt.reshape(bsz, seq // t_attn, 2 * bwid, t_attn)
        lam_init = 0.8 - 0.6 * math.exp(-0.3 * i)
        ya = _diff_attention(qk3, vt4, diff_lambda[i], diff_subln[i][:, None], bias, t=t_attn, lam_init=lam_init)
        yc = _fox_attention(qk3, vt4, ckb, t=t_attn)

        bw = bwid // LRU_BLOCKS
        wax = jnp.concatenate([lru_wa[i], lru_wx[i]], axis=-1).astype(BF16)
        bax = jnp.concatenate([lru_ba[i].reshape(LRU_BLOCKS, 1, bw), lru_bx[i].reshape(LRU_BLOCKS, 1, bw)], axis=-1)
        yb = _lru_branch(bxg, conv_w[i], conv_b[i][None], wax, bax, lru_lambda[i][None], bsz=bsz, t=256)

        merged = _merge(ya.reshape(n, bwid), yb, yc.reshape(n, bwid), w_branch[i].astype(BF16), gates,
                        tm=512, tn=1024)
        x1, x1b = _wo_ln(merged, w_o[i].astype(BF16), x2, ln1_g[i][None], ln1_b[i][None], alpha=alpha, tm=256)

        ple = _ple(x1b, w_ple_gate[i].astype(BF16), b_ple_gate[i][None], p[i].reshape(n, -1).astype(BF16),
                   w_ple[i].astype(BF16), tm=1024, tn=1024)
        idx_t, g_t = _peer_route(x1b, peer_wq[i].astype(BF16), _peer_keys_layout(peer_keys[i]), tm=512)
        wdense = _peer_gate_matrix(idx_t, g_t, tb=128)
        yf = _peer_experts(x1b, peer_u[i].T.astype(BF16), peer_v[i].astype(BF16), wdense, tm=1024, ce=512)
        x2, xb = _residual_ln(x1, yf, ple, ln2_g[i][None], ln2_b[i][None], alpha=alpha, tm=512)
    return x2.reshape(bsz, seq, dm)
```

```python
import functools
import math

import jax
import jax.numpy as jnp
import numpy as np
from jax import lax
from jax.experimental import pallas as pl
from jax.experimental.pallas import tpu as pltpu

F32, BF16, I32 = jnp.float32, jnp.bfloat16, jnp.int32

V7X_VMEM_BYTES = 64 * 2**20
V7X_LANES = 128
V7X_SUBLANES = 8
VMEM_HEADROOM_BYTES = 8 * 2**20

A_HEADS = 8
A_HALF = 64
A_VDIM = 2 * A_HALF
LRU_BLOCKS = 8
CONV_WIDTH = 4
LRU_C = 8.0
C_HEADS = 8
C_HDIM = 128
N_BRANCH = 3
REL_BUCKETS = 32
REL_MAX_DIST = 128
PEER_HEADS = 8
PEER_NKEYS = 128
PEER_HALF = 64
PEER_TOPK = 16
LN_EPS = 1e-5

LOG2E = 1.4426950408889634
MASKED = -1e30


def _vmem_limit(*block_bytes, scratch_bytes=0, single_buffered_bytes=0):
    need = 2 * sum(block_bytes) + single_buffered_bytes + scratch_bytes + VMEM_HEADROOM_BYTES
    return int(min(max(need, 32 * 2**20), V7X_VMEM_BYTES - 4 * 2**20))


def _nbytes(shape, dtype):
    return int(np.prod(shape)) * jnp.dtype(dtype).itemsize


def _log_sigmoid(z):
    return jnp.minimum(z, 0.0) - jnp.log1p(jnp.exp(-jnp.abs(z)))


def _layer_norm_rows(y, g, b):
    mu = jnp.mean(y, axis=1, keepdims=True)
    yc = y - mu
    var = jnp.mean(yc * yc, axis=1, keepdims=True)
    return yc * lax.rsqrt(var + LN_EPS) * g + b


def _proj_kernel(x_ref, w_ref, b_ref, s_ref, o_ref, *, act):
    y = jnp.dot(x_ref[...], w_ref[...], preferred_element_type=F32)
    y = (y + b_ref[...]) * s_ref[...]
    if act == "sigmoid":
        y = jax.nn.sigmoid(y)
    o_ref[...] = y.astype(o_ref.dtype)


def _proj(x, w, b, s, *, out_dtype, tm, tn, act=None, name):
    m, k = x.shape
    n = w.shape[1]
    grid = (n // tn, m // tm)
    return pl.pallas_call(
        functools.partial(_proj_kernel, act=act),
        out_shape=jax.ShapeDtypeStruct((m, n), out_dtype),
        grid=grid,
        in_specs=[
            pl.BlockSpec((tm, k), lambda j, i: (i, 0)),
            pl.BlockSpec((k, tn), lambda j, i: (0, j)),
            pl.BlockSpec((1, tn), lambda j, i: (0, j)),
            pl.BlockSpec((1, tn), lambda j, i: (0, j)),
        ],
        out_specs=pl.BlockSpec((tm, tn), lambda j, i: (i, j)),
        compiler_params=pltpu.CompilerParams(
            dimension_semantics=("parallel", "parallel"),
            vmem_limit_bytes=_vmem_limit(_nbytes((tm, k), x.dtype), _nbytes((k, tn), w.dtype),
                                         _nbytes((tm, tn), out_dtype)),
        ),
        name=name,
    )(x, w, b, s)


def _proj_t_kernel(w_ref, x_ref, b_ref, o_ref):
    y = lax.dot_general(w_ref[...], x_ref[...], (((1,), (1,)), ((), ())), preferred_element_type=F32)
    o_ref[...] = (y + b_ref[...]).astype(o_ref.dtype)


def _proj_t(x, w_t, b_col, *, tm, tn, name):
    m, k = x.shape
    n = w_t.shape[0]
    return pl.pallas_call(
        _proj_t_kernel,
        out_shape=jax.ShapeDtypeStruct((m // tm, n, tm), BF16),
        grid=(n // tn, m // tm),
        in_specs=[
            pl.BlockSpec((tn, k), lambda j, i: (j, 0)),
            pl.BlockSpec((tm, k), lambda j, i: (i, 0)),
            pl.BlockSpec((tn, 1), lambda j, i: (j, 0)),
        ],
        out_specs=pl.BlockSpec((None, tn, tm), lambda j, i: (i, j, 0)),
        compiler_params=pltpu.CompilerParams(
            dimension_semantics=("parallel", "parallel"),
            vmem_limit_bytes=_vmem_limit(_nbytes((tm, k), BF16), _nbytes((tn, k), BF16), _nbytes((tn, tm), BF16)),
        ),
        name=name,
    )(w_t, x, b_col)


def _fgate_kernel(x_ref, w_ref, b_ref, o_ref, carry_ref, *, ts):
    @pl.when(pl.program_id(1) == 0)
    def _():
        carry_ref[...] = jnp.zeros_like(carry_ref)

    z = lax.dot_general(w_ref[...], x_ref[...], (((1,), (1,)), ((), ())),
                        preferred_element_type=F32) + b_ref[...]
    c = _log_sigmoid(z)
    lane = lax.broadcasted_iota(I32, c.shape, 1)
    d = 1
    while d < ts:
        c = c + jnp.where(lane >= d, pltpu.roll(c, d, 1), 0.0)
        d *= 2
    c = c + carry_ref[:, 0:1]
    carry_ref[...] = jnp.broadcast_to(c[:, ts - 1:ts], carry_ref.shape)
    c = c * LOG2E
    for h in range(C_HEADS):
        o_ref[h] = jnp.broadcast_to(c[h:h + 1, :], (V7X_LANES, ts)).T


def _fgate_cumsum(x3, wf_t, bf, *, ts):
    bsz, seq, dm = x3.shape
    return pl.pallas_call(
        functools.partial(_fgate_kernel, ts=ts),
        out_shape=jax.ShapeDtypeStruct((bsz, C_HEADS, seq, V7X_LANES), F32),
        grid=(bsz, seq // ts),
        in_specs=[
            pl.BlockSpec((None, ts, dm), lambda b, i: (b, i, 0)),
            pl.BlockSpec((C_HEADS, dm), lambda b, i: (0, 0)),
            pl.BlockSpec((C_HEADS, 1), lambda b, i: (0, 0)),
        ],
        out_specs=pl.BlockSpec((None, C_HEADS, ts, V7X_LANES), lambda b, i: (b, 0, i, 0)),
        scratch_shapes=[pltpu.VMEM((C_HEADS, V7X_LANES), F32)],
        compiler_params=pltpu.CompilerParams(dimension_semantics=("parallel", "arbitrary")),
        name="fgate_cumsum",
    )(x3, wf_t, bf)


def _rel_bucket_tiles(t):
    q = np.arange(t)[None, :]
    k = np.arange(t)[:, None]
    out = []
    for off in (0, t):
        rel = q - k + off
        n = np.maximum(rel, 0)
        max_exact = REL_BUCKETS // 2
        nf = np.maximum(n, 1).astype(np.float32)
        large = max_exact + (np.log(nf / np.float32(max_exact)) / np.float32(math.log(REL_MAX_DIST / max_exact))
                             * np.float32(REL_BUCKETS - max_exact)).astype(np.int32)
        large = np.minimum(large, REL_BUCKETS - 1)
        bkt = np.where(n < max_exact, n, large)
        out.append(np.where(rel >= 0, bkt, -1))
    return np.stack(out).astype(np.int32)


def _bias_kernel(rel_ref, bkt_ref, o_ref):
    h = pl.program_id(0)
    bkt = bkt_ref[...]
    far = rel_ref[REL_BUCKETS - 1, h]
    acc = jnp.zeros(bkt.shape, F32)
    for b in range(REL_BUCKETS):
        acc = jnp.where(bkt == b, rel_ref[b, h] - far, acc)
    o_ref[...] = jnp.where(bkt < 0, MASKED, acc * LOG2E)


def _bias_tiles(rel_bias, t):
    bkt = jnp.asarray(_rel_bucket_tiles(t))
    return pl.pallas_call(
        _bias_kernel,
        out_shape=jax.ShapeDtypeStruct((A_HEADS, 2, t, t), F32),
        grid=(A_HEADS,),
        in_specs=[
            pl.BlockSpec(memory_space=pltpu.SMEM),
            pl.BlockSpec((2, t, t), lambda h: (0, 0, 0)),
        ],
        out_specs=pl.BlockSpec((None, 2, t, t), lambda h: (h, 0, 0, 0)),
        name="rel_bias_tiles",
    )(rel_bias, bkt)


HEADS_PER_STEP = 4
PER_GROUP_WINDOW = pl.Buffered(1)


def _sublane_allreduce(x, op):
    for shift in (1, 2, 4):
        x = op(x, pltpu.roll(x, shift, 0))
    return x


def _rows3(x):
    return x.reshape(x.shape[0] // V7X_SUBLANES, V7X_SUBLANES, x.shape[1])


def _attn_scratch(hp, dv, t, c):
    stat = pltpu.VMEM((hp, V7X_SUBLANES, c), F32)
    return [stat, stat, stat, pltpu.VMEM((hp, dv, c), F32), pltpu.VMEM((hp, t, c), BF16)]


def _attn_init(m_sc, l_sc, a_sc, acc_sc, p_sc):
    m_sc[...] = jnp.full_like(m_sc, MASKED)
    l_sc[...] = jnp.zeros_like(l_sc)
    a_sc[...] = jnp.ones_like(a_sc)
    acc_sc[...] = jnp.zeros_like(acc_sc)
    p_sc[...] = jnp.zeros_like(p_sc)


def _attn_fold(vt_prev, h, a_sc, acc_sc, p_sc):
    pv = jnp.dot(vt_prev, p_sc[h], preferred_element_type=F32)
    acc_sc[h] = (_rows3(acc_sc[h]) * a_sc[h][None]).reshape(pv.shape) + pv


def _attn_stage(s, vt_prev, h, m_sc, l_sc, a_sc, acc_sc, p_sc):
    _attn_fold(vt_prev, h, a_sc, acc_sc, p_sc)
    s3 = _rows3(s)
    m_prev = m_sc[h]
    m_new = jnp.maximum(m_prev, _sublane_allreduce(jnp.max(s3, axis=0), jnp.maximum))
    alpha = jnp.exp2(m_prev - m_new)
    p3 = jnp.exp2(s3 - m_new[None])
    l_sc[h] = alpha * l_sc[h] + jnp.sum(p3, axis=0)
    p_sc[h] = p3.reshape(s.shape).astype(BF16)
    a_sc[h] = alpha
    m_sc[h] = m_new


def _diff_attn_kernel(lam_ref, subln_ref, q_ref, k_ref, vt_ref, bias_ref, o_ref,
                      m_sc, l_sc, a_sc, acc_sc, p_sc, *, t, lam_init):
    qi = pl.program_id(2)
    state = (m_sc, l_sc, a_sc, acc_sc, p_sc)
    _attn_init(*state)

    qqs = []
    for h in range(HEADS_PER_STEP):
        q = q_ref[:, h * A_VDIM:(h + 1) * A_VDIM]
        lane = lax.broadcasted_iota(I32, q.shape, 1)
        zero = jnp.zeros_like(q)
        qqs.append(jnp.concatenate([jnp.where(lane < A_HALF, q, zero), jnp.where(lane >= A_HALF, q, zero)], axis=0))

    def tile(kj, near):
        start = pl.multiple_of(kj * t, t)
        prev = jnp.maximum(kj - 1, 0)
        for h in range(HEADS_PER_STEP):
            cols = slice(h * A_VDIM, (h + 1) * A_VDIM)
            s = lax.dot_general(k_ref[pl.ds(start, t), cols], qqs[h], (((1,), (1,)), ((), ())),
                                preferred_element_type=F32)
            if near is not None:
                bias = bias_ref[h, near]
                s = s + jnp.concatenate([bias, bias], axis=1)
            _attn_stage(s, vt_ref[prev, cols, :], h, *state)

    def far_body(kj, carry):
        tile(kj, None)
        return carry

    lax.fori_loop(0, jnp.maximum(qi - 1, 0), far_body, 0)

    @pl.when(qi >= 1)
    def _():
        tile(qi - 1, 1)

    tile(qi, 0)

    lv = lam_ref[...]
    lam = (jnp.exp(jnp.sum(lv[0:1] * lv[1:2], axis=1, keepdims=True))
           - jnp.exp(jnp.sum(lv[2:3] * lv[3:4], axis=1, keepdims=True)) + lam_init)
    for h in range(HEADS_PER_STEP):
        _attn_fold(vt_ref[qi, h * A_VDIM:(h + 1) * A_VDIM, :], h, a_sc, acc_sc, p_sc)
        l = _sublane_allreduce(l_sc[h], jnp.add)
        r = (_rows3(acc_sc[h]) / l[None]).reshape(A_VDIM, 2 * t)
        o = r[:, :t] - lam * r[:, t:]
        ms = _sublane_allreduce(jnp.sum(_rows3(o * o), axis=0), jnp.add) * (1.0 / A_VDIM)
        y = (_rows3(o) * lax.rsqrt(ms + LN_EPS)[None]).reshape(A_VDIM, t) * (subln_ref[...] * (1.0 - lam_init))
        o_ref[:, h * A_VDIM:(h + 1) * A_VDIM] = y.T.astype(o_ref.dtype)


def _diff_attention(qk3, vt4, lam4, subln_col, bias, *, t, lam_init):
    bsz, seq, _ = qk3.shape
    nq = seq // t
    hp = HEADS_PER_STEP
    wid = hp * A_VDIM
    kblk = A_HEADS // hp
    return pl.pallas_call(
        functools.partial(_diff_attn_kernel, t=t, lam_init=lam_init),
        out_shape=jax.ShapeDtypeStruct((bsz, seq, A_HEADS * A_VDIM), BF16),
        grid=(bsz, A_HEADS // hp, nq),
        in_specs=[
            pl.BlockSpec((4, A_HALF), lambda b, h, i: (0, 0)),
            pl.BlockSpec((A_VDIM, 1), lambda b, h, i: (0, 0)),
            pl.BlockSpec((None, t, wid), lambda b, h, i: (b, i, h)),
            pl.BlockSpec((None, seq, wid), lambda b, h, i: (b, 0, kblk + h), pipeline_mode=PER_GROUP_WINDOW),
            pl.BlockSpec((None, nq, wid, t), lambda b, h, i: (b, 0, h, 0), pipeline_mode=PER_GROUP_WINDOW),
            pl.BlockSpec((hp, 2, t, t), lambda b, h, i: (h, 0, 0, 0), pipeline_mode=PER_GROUP_WINDOW),
        ],
        out_specs=pl.BlockSpec((None, t, wid), lambda b, h, i: (b, i, h)),
        scratch_shapes=_attn_scratch(hp, A_VDIM, t, 2 * t),
        compiler_params=pltpu.CompilerParams(
            dimension_semantics=("parallel", "parallel", "arbitrary"),
            vmem_limit_bytes=_vmem_limit(
                2 * _nbytes((t, wid), BF16),
                single_buffered_bytes=2 * _nbytes((seq, wid), BF16) + _nbytes((hp, 2, t, t), F32),
                scratch_bytes=_nbytes((hp, A_VDIM + t // 2 + t, 2 * t), F32)),
        ),
        name="diff_attention",
    )(lam4, subln_col, qk3, qk3, vt4, bias)


def _fox_attn_kernel(q_ref, k_ref, vt_ref, ck_ref, o_ref, m_sc, l_sc, a_sc, acc_sc, p_sc, *, t):
    qi = pl.program_id(2)
    state = (m_sc, l_sc, a_sc, acc_sc, p_sc)
    _attn_init(*state)
    qs = [q_ref[:, h * C_HDIM:(h + 1) * C_HDIM] for h in range(HEADS_PER_STEP)]

    def tile(kj, diagonal):
        start = pl.multiple_of(kj * t, t)
        prev = jnp.maximum(kj - 1, 0)
        for h in range(HEADS_PER_STEP):
            cols = slice(h * C_HDIM, (h + 1) * C_HDIM)
            s = lax.dot_general(k_ref[pl.ds(start, t), cols], qs[h], (((1,), (1,)), ((), ())),
                                preferred_element_type=F32)
            ck = ck_ref[h, pl.ds(start, t), :]
            s = s - jnp.concatenate([ck] * (t // V7X_LANES), axis=1)
            if diagonal:
                key = lax.broadcasted_iota(I32, s.shape, 0)
                qry = lax.broadcasted_iota(I32, s.shape, 1)
                s = jnp.where(key <= qry, s, MASKED)
            _attn_stage(s, vt_ref[prev, cols, :], h, *state)

    def body(kj, carry):
        tile(kj, False)
        return carry

    lax.fori_loop(0, qi, body, 0)
    tile(qi, True)
    for h in range(HEADS_PER_STEP):
        _attn_fold(vt_ref[qi, h * C_HDIM:(h + 1) * C_HDIM, :], h, a_sc, acc_sc, p_sc)
        l = _sublane_allreduce(l_sc[h], jnp.add)
        o = (_rows3(acc_sc[h]) / l[None]).reshape(C_HDIM, t)
        o_ref[:, h * C_HDIM:(h + 1) * C_HDIM] = o.T.astype(o_ref.dtype)


def _fox_attention(qk3, vt4, ckb, *, t):
    bsz, seq, _ = qk3.shape
    nq = seq // t
    hp = HEADS_PER_STEP
    wid = hp * C_HDIM
    qblk = 2 * (A_HEADS // hp)
    kblk = qblk + C_HEADS // hp
    vblk = A_HEADS // hp
    return pl.pallas_call(
        functools.partial(_fox_attn_kernel, t=t),
        out_shape=jax.ShapeDtypeStruct((bsz, seq, C_HEADS * C_HDIM), BF16),
        grid=(bsz, C_HEADS // hp, nq),
        in_specs=[
            pl.BlockSpec((None, t, wid), lambda b, h, i: (b, i, qblk + h)),
            pl.BlockSpec((None, seq, wid), lambda b, h, i: (b, 0, kblk + h), pipeline_mode=PER_GROUP_WINDOW),
            pl.BlockSpec((None, nq, wid, t), lambda b, h, i: (b, 0, vblk + h, 0), pipeline_mode=PER_GROUP_WINDOW),
            pl.BlockSpec((None, hp, seq, V7X_LANES), lambda b, h, i: (b, h, 0, 0), pipeline_mode=PER_GROUP_WINDOW),
        ],
        out_specs=pl.BlockSpec((None, t, wid), lambda b, h, i: (b, i, h)),
        scratch_shapes=_attn_scratch(hp, C_HDIM, t, t),
        compiler_params=pltpu.CompilerParams(
            dimension_semantics=("parallel", "parallel", "arbitrary"),
            vmem_limit_bytes=_vmem_limit(
                2 * _nbytes((t, wid), BF16),
                single_buffered_bytes=2 * _nbytes((seq, wid), BF16) + _nbytes((hp, seq, V7X_LANES), F32),
                scratch_bytes=_nbytes((hp, C_HDIM + t // 2 + t, t), F32)),
        ),
        name="fox_attention",
    )(qk3, qk3, vt4, ckb)


def _lru_kernel(bx_ref, bg_ref, cw_ref, cb_ref, wax_ref, bax_ref, lam_ref, o_ref, xbuf, hprev, *, t):
    pad = V7X_SUBLANES

    @pl.when(pl.program_id(1) == 0)
    def _():
        xbuf[0:pad, :] = jnp.zeros((pad, xbuf.shape[1]), F32)
        hprev[...] = jnp.zeros_like(hprev)

    xbuf[pad:pad + t, :] = bx_ref[...]
    xc = cb_ref[...]
    for tap in range(CONV_WIDTH):
        xc = xc + xbuf[pl.ds(pad - (CONV_WIDTH - 1) + tap, t), :] * cw_ref[tap:tap + 1, :]
    xbuf[0:pad, :] = bx_ref[t - pad:t, :]

    bw = xc.shape[1] // LRU_BLOCKS
    row = lax.broadcasted_iota(I32, (t, bw), 0)
    for g in range(LRU_BLOCKS):
        cols = slice(g * bw, (g + 1) * bw)
        xg = xc[:, cols]
        z = jnp.dot(xg.astype(BF16), wax_ref[g], preferred_element_type=F32) + bax_ref[g]
        r = jax.nn.sigmoid(z[:, :bw])
        gi = jax.nn.sigmoid(z[:, bw:])
        log_a = (LRU_C * r) * _log_sigmoid(lam_ref[:, cols])
        a = jnp.exp(log_a)
        u = jnp.sqrt(1.0 - jnp.exp(2.0 * log_a)) * (gi * xg)
        d = 1
        while d < t:
            keep = row >= d
            a_sh = jnp.where(keep, pltpu.roll(a, d, 0), 1.0)
            u_sh = jnp.where(keep, pltpu.roll(u, d, 0), 0.0)
            u = a * u_sh + u
            a = a * a_sh
            d *= 2
        h = a * hprev[0:1, cols] + u
        hprev[0:1, cols] = h[t - 1:t, :]
        o_ref[:, cols] = (jax.nn.gelu(bg_ref[:, cols], approximate=True) * h).astype(o_ref.dtype)


def _lru_branch(bxg, conv_w, conv_b, wax, bax, lam, *, bsz, t):
    n, two_w = bxg.shape
    w = two_w // 2
    nt = n // bsz // t
    bw = w // LRU_BLOCKS
    return pl.pallas_call(
        functools.partial(_lru_kernel, t=t),
        out_shape=jax.ShapeDtypeStruct((n, w), BF16),
        grid=(bsz, nt),
        in_specs=[
            pl.BlockSpec((t, w), lambda b, i: (b * nt + i, 0)),
            pl.BlockSpec((t, w), lambda b, i: (b * nt + i, 1)),
            pl.BlockSpec((CONV_WIDTH, w), lambda b, i: (0, 0)),
            pl.BlockSpec((1, w), lambda b, i: (0, 0)),
            pl.BlockSpec((LRU_BLOCKS, bw, 2 * bw), lambda b, i: (0, 0, 0)),
            pl.BlockSpec((LRU_BLOCKS, 1, 2 * bw), lambda b, i: (0, 0, 0)),
            pl.BlockSpec((1, w), lambda b, i: (0, 0)),
        ],
        out_specs=pl.BlockSpec((t, w), lambda b, i: (b * nt + i, 0)),
        scratch_shapes=[
            pltpu.VMEM((t + V7X_SUBLANES, w), F32),
            pltpu.VMEM((V7X_SUBLANES, w), F32),
        ],
        compiler_params=pltpu.CompilerParams(dimension_semantics=("parallel", "arbitrary")),
        name="conv_rglru",
    )(bxg, bxg, conv_w, conv_b, wax, bax, lam)


def _merge_kernel(ya_ref, yb_ref, yc_ref, w_ref, g0_ref, g1_ref, g2_ref, o_ref):
    acc = g0_ref[...].astype(F32) * jnp.dot(ya_ref[...], w_ref[0], preferred_element_type=F32)
    acc = acc + g1_ref[...].astype(F32) * jnp.dot(yb_ref[...], w_ref[1], preferred_element_type=F32)
    acc = acc + g2_ref[...].astype(F32) * jnp.dot(yc_ref[...], w_ref[2], preferred_element_type=F32)
    o_ref[...] = acc.astype(o_ref.dtype)


def _merge(ya, yb, yc, wb, gates, *, tm, tn):
    n, bwid = ya.shape
    dm = wb.shape[2]
    nc = dm // tn
    y_spec = pl.BlockSpec((tm, bwid), lambda j, i: (i, 0))
    return pl.pallas_call(
        _merge_kernel,
        out_shape=jax.ShapeDtypeStruct((n, dm), BF16),
        grid=(nc, n // tm),
        in_specs=[
            y_spec, y_spec, y_spec,
            pl.BlockSpec((N_BRANCH, bwid, tn), lambda j, i: (0, 0, j)),
            pl.BlockSpec((tm, tn), lambda j, i: (i, j)),
            pl.BlockSpec((tm, tn), lambda j, i: (i, nc + j)),
            pl.BlockSpec((tm, tn), lambda j, i: (i, 2 * nc + j)),
        ],
        out_specs=pl.BlockSpec((tm, tn), lambda j, i: (i, j)),
        compiler_params=pltpu.CompilerParams(
            dimension_semantics=("parallel", "parallel"),
            vmem_limit_bytes=_vmem_limit(3 * _nbytes((tm, bwid), BF16), _nbytes((N_BRANCH, bwid, tn), BF16),
                                         4 * _nbytes((tm, tn), BF16)),
        ),
        name="gated_merge",
    )(ya, yb, yc, wb, gates, gates, gates)


def _wo_ln_kernel(m_ref, w_ref, x_ref, g_ref, b_ref, o32_ref, o16_ref, *, alpha):
    y = jnp.dot(m_ref[...], w_ref[...], preferred_element_type=F32) + alpha * x_ref[...]
    out = _layer_norm_rows(y, g_ref[...], b_ref[...])
    o32_ref[...] = out
    o16_ref[...] = out.astype(BF16)


def _wo_ln(merged, wo, x, g, b, *, alpha, tm):
    n, dm = x.shape
    row = pl.BlockSpec((tm, dm), lambda i: (i, 0))
    vec = pl.BlockSpec((1, dm), lambda i: (0, 0))
    return pl.pallas_call(
        functools.partial(_wo_ln_kernel, alpha=alpha),
        out_shape=(jax.ShapeDtypeStruct((n, dm), F32), jax.ShapeDtypeStruct((n, dm), BF16)),
        grid=(n // tm,),
        in_specs=[row, pl.BlockSpec((dm, dm), lambda i: (0, 0)), row, vec, vec],
        out_specs=(row, row),
        compiler_params=pltpu.CompilerParams(
            dimension_semantics=("parallel",),
            vmem_limit_bytes=_vmem_limit(_nbytes((tm, dm), BF16), _nbytes((dm, dm), BF16),
                                         2 * _nbytes((tm, dm), F32), _nbytes((tm, dm), BF16)),
        ),
        name="wo_layernorm",
    )(merged, wo, x, g, b)


def _ple_kernel(x_ref, wg_ref, bg_ref, p_ref, we_ref, o_ref):
    gate = jax.nn.sigmoid(jnp.dot(x_ref[...], wg_ref[...], preferred_element_type=F32) + bg_ref[...])
    o_ref[...] = gate * jnp.dot(p_ref[...], we_ref[...], preferred_element_type=F32)


def _ple(x1b, wg, bg, pb, we, *, tm, tn):
    n, dm = x1b.shape
    pdim = pb.shape[1]
    return pl.pallas_call(
        _ple_kernel,
        out_shape=jax.ShapeDtypeStruct((n, dm), F32),
        grid=(dm // tn, n // tm),
        in_specs=[
            pl.BlockSpec((tm, dm), lambda j, i: (i, 0)),
            pl.BlockSpec((dm, tn), lambda j, i: (0, j)),
            pl.BlockSpec((1, tn), lambda j, i: (0, j)),
            pl.BlockSpec((tm, pdim), lambda j, i: (i, 0)),
            pl.BlockSpec((pdim, tn), lambda j, i: (0, j)),
        ],
        out_specs=pl.BlockSpec((tm, tn), lambda j, i: (i, j)),
        compiler_params=pltpu.CompilerParams(
            dimension_semantics=("parallel", "parallel"),
            vmem_limit_bytes=_vmem_limit(_nbytes((tm, dm), BF16), _nbytes((dm, tn), BF16),
                                         _nbytes((tm, tn), F32)),
        ),
        name="gated_ple",
    )(x1b, wg, bg, pb, we)


def _top_rows(s, payload, k):
    rows = s.shape[0]
    rid = lax.broadcasted_iota(I32, s.shape, 0)
    vals, pays = [], []
    for _ in range(k):
        m = jnp.max(s, axis=0, keepdims=True)
        pos = jnp.min(jnp.where(s == m, rid, rows), axis=0, keepdims=True)
        hit = rid == pos
        vals.append(m)
        pays.append(pos if payload is None else jnp.max(jnp.where(hit, payload, -1), axis=0, keepdims=True))
        s = jnp.where(hit, -jnp.inf, s)
    return jnp.concatenate(vals, axis=0), jnp.concatenate(pays, axis=0)


def _peer_route_kernel(x_ref, wq_ref, keys_ref, idx_ref, g_ref):
    q = jnp.dot(x_ref[...], wq_ref[...], preferred_element_type=F32).astype(BF16)
    st = lax.dot_general(keys_ref[...], q, (((1,), (1,)), ((), ())), preferred_element_type=F32)
    s1, i1 = _top_rows(st[:PEER_NKEYS], None, PEER_TOPK)
    s2, i2 = _top_rows(st[PEER_NKEYS:], None, PEER_TOPK)
    cand, cidx = [], []
    for a in range(PEER_TOPK):
        nb = PEER_TOPK // (a + 1)
        cand.append(s1[a:a + 1] + s2[:nb])
        cidx.append(i1[a:a + 1] * PEER_NKEYS + i2[:nb])
    npairs = sum(c.shape[0] for c in cand)
    npad = -npairs % V7X_SUBLANES
    cand.append(jnp.full((npad, st.shape[1]), -jnp.inf, F32))
    cidx.append(jnp.full((npad, st.shape[1]), -1, I32))
    sc, idx = _top_rows(jnp.concatenate(cand, axis=0), jnp.concatenate(cidx, axis=0), PEER_TOPK)
    e = jnp.exp(sc - jnp.max(sc, axis=0, keepdims=True))
    idx_ref[...] = idx
    g_ref[...] = e / jnp.sum(e, axis=0, keepdims=True)


def _peer_route(x1b, wq, keys_t, *, tm):
    n, dm = x1b.shape
    qd = 2 * PEER_HALF
    slots = PEER_HEADS * PEER_TOPK
    out_spec = pl.BlockSpec((PEER_TOPK, tm), lambda i, h: (h, i))
    return pl.pallas_call(
        _peer_route_kernel,
        out_shape=(jax.ShapeDtypeStruct((slots, n), I32), jax.ShapeDtypeStruct((slots, n), F32)),
        grid=(n // tm, PEER_HEADS),
        in_specs=[
            pl.BlockSpec((tm, dm), lambda i, h: (i, 0)),
            pl.BlockSpec((dm, qd), lambda i, h: (0, h)),
            pl.BlockSpec((None, 2 * PEER_NKEYS, qd), lambda i, h: (h, 0, 0)),
        ],
        out_specs=(out_spec, out_spec),
        compiler_params=pltpu.CompilerParams(dimension_semantics=("parallel", "parallel")),
        name="peer_route",
    )(x1b, wq, keys_t)


W_ROW_PITCH = PEER_NKEYS + V7X_SUBLANES


def _peer_gate_kernel(idx_ref, g_ref, o_ref, wbuf, idx_sc, g_sc, *, tb):
    idx_sc[...] = idx_ref[...].T
    g_sc[...] = g_ref[...].T
    rid = lax.broadcasted_iota(I32, (PEER_NKEYS, PEER_NKEYS), 0).astype(F32).astype(BF16)
    one = jnp.ones((PEER_NKEYS, PEER_NKEYS), BF16)
    zero = jnp.zeros((PEER_NKEYS, PEER_NKEYS), BF16)
    group = 2 * V7X_SUBLANES
    packed_rows = 2 * V7X_SUBLANES

    def body(j, carry):
        base = pl.multiple_of(j * group, group)
        ib = idx_sc[pl.ds(base, group), :]
        gb = g_sc[pl.ds(base, group), :]
        i1b = (ib >> 7).astype(F32)
        i2b = (ib & (PEER_NKEYS - 1)).astype(F32)
        for r in range(group):
            def rep(v):
                one_vreg = jnp.broadcast_to(v[r:r + 1, :], (packed_rows, PEER_NKEYS)).astype(BF16)
                return jnp.concatenate([one_vreg] * (PEER_NKEYS // packed_rows), axis=0)
            r1 = jnp.where(rid == rep(i1b), rep(gb), zero)
            r2t = jnp.where(rid == rep(i2b), one, zero)
            wt = lax.dot_general(r1, r2t, (((1,), (1,)), ((), ())), preferred_element_type=F32)
            wbuf[pl.ds(pl.multiple_of((base + r) * W_ROW_PITCH, V7X_SUBLANES), PEER_NKEYS), :] = wt
        return carry

    lax.fori_loop(0, tb // group, body, 0)
    for i1 in range(PEER_NKEYS):
        o_ref[:, i1 * PEER_NKEYS:(i1 + 1) * PEER_NKEYS] = (
            wbuf[pl.ds(i1, tb, stride=W_ROW_PITCH), :].astype(o_ref.dtype))


def _peer_gate_matrix(idx_t, g_t, *, tb):
    slots, n = idx_t.shape
    ne = PEER_NKEYS * PEER_NKEYS
    in_spec = pl.BlockSpec((slots, tb), lambda i: (0, i))
    return pl.pallas_call(
        functools.partial(_peer_gate_kernel, tb=tb),
        out_shape=jax.ShapeDtypeStruct((n, ne), BF16),
        grid=(n // tb,),
        in_specs=[in_spec, in_spec],
        out_specs=pl.BlockSpec((tb, ne), lambda i: (i, 0)),
        scratch_shapes=[
            pltpu.VMEM((tb * W_ROW_PITCH, PEER_NKEYS), F32),
            pltpu.VMEM((tb, slots), I32),
            pltpu.VMEM((tb, slots), F32),
        ],
        compiler_params=pltpu.CompilerParams(
            dimension_semantics=("parallel",),
            vmem_limit_bytes=_vmem_limit(_nbytes((tb, ne), BF16),
                                         scratch_bytes=_nbytes((tb * W_ROW_PITCH, PEER_NKEYS), F32)),
        ),
        name="peer_gate_matrix",
    )(idx_t, g_t)


def _peer_expert_kernel(x_ref, ut_ref, v_ref, w_ref, o_ref):
    @pl.when(pl.program_id(1) == 0)
    def _():
        o_ref[...] = jnp.zeros_like(o_ref)

    a = jnp.dot(x_ref[...], ut_ref[...], preferred_element_type=F32)
    gelu = 0.5 * a * (1.0 + lax.erf(a * (0.5 ** 0.5)))
    g = (w_ref[...].astype(F32) * gelu).astype(BF16)
    o_ref[...] += jnp.dot(g, v_ref[...], preferred_element_type=F32)


def _peer_experts(x1b, ut, v, w, *, tm, ce):
    n, dm = x1b.shape
    ne = v.shape[0]
    return pl.pallas_call(
        _peer_expert_kernel,
        out_shape=jax.ShapeDtypeStruct((n, dm), F32),
        grid=(n // tm, ne // ce),
        in_specs=[
            pl.BlockSpec((tm, dm), lambda i, j: (i, 0)),
            pl.BlockSpec((dm, ce), lambda i, j: (0, j)),
            pl.BlockSpec((ce, dm), lambda i, j: (j, 0)),
            pl.BlockSpec((tm, ce), lambda i, j: (i, j)),
        ],
        out_specs=pl.BlockSpec((tm, dm), lambda i, j: (i, 0)),
        compiler_params=pltpu.CompilerParams(
            dimension_semantics=("parallel", "arbitrary"),
            vmem_limit_bytes=_vmem_limit(_nbytes((tm, dm), BF16), 2 * _nbytes((dm, ce), BF16),
                                         _nbytes((tm, ce), BF16), _nbytes((tm, dm), F32)),
        ),
        name="peer_experts",
    )(x1b, ut, v, w)


def _residual_ln_kernel(x_ref, y_ref, p_ref, g_ref, b_ref, o32_ref, o16_ref, *, alpha):
    out = _layer_norm_rows(alpha * x_ref[...] + y_ref[...] + p_ref[...], g_ref[...], b_ref[...])
    o32_ref[...] = out
    o16_ref[...] = out.astype(BF16)


def _residual_ln(x1, yf, ple, g, b, *, alpha, tm):
    n, dm = x1.shape
    row = pl.BlockSpec((tm, dm), lambda i: (i, 0))
    vec = pl.BlockSpec((1, dm), lambda i: (0, 0))
    return pl.pallas_call(
        functools.partial(_residual_ln_kernel, alpha=alpha),
        out_shape=(jax.ShapeDtypeStruct((n, dm), F32), jax.ShapeDtypeStruct((n, dm), BF16)),
        grid=(n // tm,),
        in_specs=[row, row, row, vec, vec],
        out_specs=(row, row),
        compiler_params=pltpu.CompilerParams(
            dimension_semantics=("parallel",),
            vmem_limit_bytes=_vmem_limit(4 * _nbytes((tm, dm), F32), _nbytes((tm, dm), BF16)),
        ),
        name="residual_layernorm",
    )(x1, yf, ple, g, b)


def _peer_keys_layout(keys):
    z = jnp.zeros_like(keys[:, 0])
    top = jnp.concatenate([keys[:, 0], z], axis=-1)
    bot = jnp.concatenate([z, keys[:, 1]], axis=-1)
    return jnp.concatenate([top, bot], axis=1).astype(BF16)


def kernel(x, p, w_in, b_in, diff_lambda, diff_subln, rel_bias, conv_w, conv_b, lru_wa, lru_ba, lru_wx,
           lru_bx, lru_lambda, w_branch, w_o, ln1_g, ln1_b, peer_wq, peer_keys, peer_u, peer_v, w_ple,
           w_ple_gate, b_ple_gate, ln2_g, ln2_b):
    bsz, seq, dm = x.shape
    depth = w_in.shape[0]
    n = bsz * seq
    alpha = (2 * depth) ** 0.25
    bwid = A_HEADS * A_VDIM
    t_attn = 512
    assert seq % t_attn == 0 and t_attn >= REL_MAX_DIST

    o_ka, o_va, o_lru, o_c, o_kc, o_vc, o_f = (j * bwid for j in (1, 2, 3, 5, 6, 7, 8))
    o_g = o_f + C_HEADS
    qk_scale = jnp.concatenate([
        jnp.full((bwid,), A_HALF ** -0.5 * LOG2E, F32), jnp.ones((bwid,), F32),
        jnp.full((bwid,), C_HDIM ** -0.5 * LOG2E, F32), jnp.ones((bwid,), F32)])[None]
    ones_row = jnp.ones((1, max(2 * bwid, N_BRANCH * dm)), F32)

    bias = _bias_tiles(rel_bias, t_attn)
    x2 = x.reshape(n, dm)
    xb = x2.astype(BF16)
    for i in range(depth):
        w = w_in[i]
        b = b_in[i]
        w_qk = jnp.concatenate([w[:, :o_va], w[:, o_c:o_vc]], axis=1).astype(BF16)
        b_qk = jnp.concatenate([b[:o_va], b[o_c:o_vc]])[None]
        qk = _proj(xb, w_qk, b_qk, qk_scale, out_dtype=BF16, tm=1024, tn=2048, name="proj_qk")
        w_v = jnp.concatenate([w[:, o_va:o_lru], w[:, o_vc:o_f]], axis=1).T.astype(BF16)
        b_v = jnp.concatenate([b[o_va:o_lru], b[o_vc:o_f]])[:, None]
        vt = _proj_t(xb, w_v, b_v, tm=t_attn, tn=1024, name="proj_v_t")
        bxg = _proj(xb, w[:, o_lru:o_c].astype(BF16), b[None, o_lru:o_c], ones_row[:, :2 * bwid],
                    out_dtype=F32, tm=1024, tn=1024, name="proj_lru")
        gates = _proj(xb, w[:, o_g:].astype(BF16), b[None, o_g:], ones_row[:, :N_BRANCH * dm],
                      out_dtype=BF16, tm=1024, tn=2048, act="sigmoid", name="proj_gates")
        ckb = _fgate_cumsum(xb.reshape(bsz, seq, dm), w[:, o_f:o_g].T.astype(BF16), b[o_f:o_g, None],
                            ts=min(seq, 1024))

        qk3 = qk.reshape(bsz, seq, 4 * bwid)
        vt4 = vt.reshape(bsz, seq // t_attn, 2 * bwid, t_attn)
        lam_init = 0.8 - 0.6 * math.exp(-0.3 * i)
        ya = _diff_attention(qk3, vt4, diff_lambda[i], diff_subln[i][:, None], bias, t=t_attn, lam_init=lam_init)
        yc = _fox_attention(qk3, vt4, ckb, t=t_attn)

        bw = bwid // LRU_BLOCKS
        wax = jnp.concatenate([lru_wa[i], lru_wx[i]], axis=-1).astype(BF16)
        bax = jnp.concatenate([lru_ba[i].reshape(LRU_BLOCKS, 1, bw), lru_bx[i].reshape(LRU_BLOCKS, 1, bw)], axis=-1)
        yb = _lru_branch(bxg, conv_w[i], conv_b[i][None], wax, bax, lru_lambda[i][None], bsz=bsz, t=256)

        merged = _merge(ya.reshape(n, bwid), yb, yc.reshape(n, bwid), w_branch[i].astype(BF16), gates,
                        tm=512, tn=1024)
        x1, x1b = _wo_ln(merged, w_o[i].astype(BF16), x2, ln1_g[i][None], ln1_b[i][None], alpha=alpha, tm=256)

        ple = _ple(x1b, w_ple_gate[i].astype(BF16), b_ple_gate[i][None], p[i].reshape(n, -1).astype(BF16),
                   w_ple[i].astype(BF16), tm=1024, tn=1024)
        idx_t, g_t = _peer_route(x1b, peer_wq[i].astype(BF16), _peer_keys_layout(peer_keys[i]), tm=1024)
        wdense = _peer_gate_matrix(idx_t, g_t, tb=128)
        yf = _peer_experts(x1b, peer_u[i].T.astype(BF16), peer_v[i].astype(BF16), wdense, tm=1024, ce=512)
        x2, xb = _residual_ln(x1, yf, ple, ln2_g[i][None], ln2_b[i][None], alpha=alpha, tm=512)
    return x2.reshape(bsz, seq, dm)
```

```python
import functools
import math

import jax
import jax.numpy as jnp
import numpy as np
from jax import lax
from jax.experimental import pallas as pl
from jax.experimental.pallas import tpu as pltpu

F32, BF16, I32 = jnp.float32, jnp.bfloat16, jnp.int32

V7X_VMEM_BYTES = 64 * 2**20
V7X_LANES = 128
V7X_SUBLANES = 8
VMEM_HEADROOM_BYTES = 8 * 2**20

A_HEADS = 8
A_HALF = 64
A_VDIM = 2 * A_HALF
LRU_BLOCKS = 8
CONV_WIDTH = 4
LRU_C = 8.0
C_HEADS = 8
C_HDIM = 128
N_BRANCH = 3
REL_BUCKETS = 32
REL_MAX_DIST = 128
PEER_HEADS = 8
PEER_NKEYS = 128
PEER_HALF = 64
PEER_TOPK = 16
LN_EPS = 1e-5

LOG2E = 1.4426950408889634
MASKED = -1e30


def _vmem_limit(*block_bytes, scratch_bytes=0, single_buffered_bytes=0):
    need = 2 * sum(block_bytes) + single_buffered_bytes + scratch_bytes + VMEM_HEADROOM_BYTES
    return int(min(max(need, 32 * 2**20), V7X_VMEM_BYTES - 4 * 2**20))


def _nbytes(shape, dtype):
    return int(np.prod(shape)) * jnp.dtype(dtype).itemsize


def _log_sigmoid(z):
    return jnp.minimum(z, 0.0) - jnp.log1p(jnp.exp(-jnp.abs(z)))


def _layer_norm_rows(y, g, b):
    mu = jnp.mean(y, axis=1, keepdims=True)
    yc = y - mu
    var = jnp.mean(yc * yc, axis=1, keepdims=True)
    return yc * lax.rsqrt(var + LN_EPS) * g + b


def _proj_kernel(x_ref, w_ref, b_ref, s_ref, o_ref, *, act):
    y = jnp.dot(x_ref[...], w_ref[...], preferred_element_type=F32)
    y = (y + b_ref[...]) * s_ref[...]
    if act == "sigmoid":
        y = jax.nn.sigmoid(y)
    o_ref[...] = y.astype(o_ref.dtype)


def _proj(x, w, b, s, *, out_dtype, tm, tn, act=None, name):
    m, k = x.shape
    n = w.shape[1]
    grid = (n // tn, m // tm)
    return pl.pallas_call(
        functools.partial(_proj_kernel, act=act),
        out_shape=jax.ShapeDtypeStruct((m, n), out_dtype),
        grid=grid,
        in_specs=[
            pl.BlockSpec((tm, k), lambda j, i: (i, 0)),
            pl.BlockSpec((k, tn), lambda j, i: (0, j)),
            pl.BlockSpec((1, tn), lambda j, i: (0, j)),
            pl.BlockSpec((1, tn), lambda j, i: (0, j)),
        ],
        out_specs=pl.BlockSpec((tm, tn), lambda j, i: (i, j)),
        compiler_params=pltpu.CompilerParams(
            dimension_semantics=("parallel", "parallel"),
            vmem_limit_bytes=_vmem_limit(_nbytes((tm, k), x.dtype), _nbytes((k, tn), w.dtype),
                                         _nbytes((tm, tn), out_dtype)),
        ),
        name=name,
    )(x, w, b, s)


def _proj_t_kernel(w_ref, x_ref, b_ref, o_ref):
    y = lax.dot_general(w_ref[...], x_ref[...], (((1,), (1,)), ((), ())), preferred_element_type=F32)
    o_ref[...] = (y + b_ref[...]).astype(o_ref.dtype)


def _proj_t(x, w_t, b_col, *, tm, tn, name):
    m, k = x.shape
    n = w_t.shape[0]
    return pl.pallas_call(
        _proj_t_kernel,
        out_shape=jax.ShapeDtypeStruct((m // tm, n, tm), BF16),
        grid=(n // tn, m // tm),
        in_specs=[
            pl.BlockSpec((tn, k), lambda j, i: (j, 0)),
            pl.BlockSpec((tm, k), lambda j, i: (i, 0)),
            pl.BlockSpec((tn, 1), lambda j, i: (j, 0)),
        ],
        out_specs=pl.BlockSpec((None, tn, tm), lambda j, i: (i, j, 0)),
        compiler_params=pltpu.CompilerParams(
            dimension_semantics=("parallel", "parallel"),
            vmem_limit_bytes=_vmem_limit(_nbytes((tm, k), BF16), _nbytes((tn, k), BF16), _nbytes((tn, tm), BF16)),
        ),
        name=name,
    )(w_t, x, b_col)


def _fgate_kernel(x_ref, w_ref, b_ref, o_ref, carry_ref, *, ts):
    @pl.when(pl.program_id(1) == 0)
    def _():
        carry_ref[...] = jnp.zeros_like(carry_ref)

    z = lax.dot_general(w_ref[...], x_ref[...], (((1,), (1,)), ((), ())),
                        preferred_element_type=F32) + b_ref[...]
    c = _log_sigmoid(z)
    lane = lax.broadcasted_iota(I32, c.shape, 1)
    d = 1
    while d < ts:
        c = c + jnp.where(lane >= d, pltpu.roll(c, d, 1), 0.0)
        d *= 2
    c = c + carry_ref[:, 0:1]
    carry_ref[...] = jnp.broadcast_to(c[:, ts - 1:ts], carry_ref.shape)
    c = c * LOG2E
    for h in range(C_HEADS):
        o_ref[h] = jnp.broadcast_to(c[h:h + 1, :], (V7X_LANES, ts)).T


def _fgate_cumsum(x3, wf_t, bf, *, ts):
    bsz, seq, dm = x3.shape
    return pl.pallas_call(
        functools.partial(_fgate_kernel, ts=ts),
        out_shape=jax.ShapeDtypeStruct((bsz, C_HEADS, seq, V7X_LANES), F32),
        grid=(bsz, seq // ts),
        in_specs=[
            pl.BlockSpec((None, ts, dm), lambda b, i: (b, i, 0)),
            pl.BlockSpec((C_HEADS, dm), lambda b, i: (0, 0)),
            pl.BlockSpec((C_HEADS, 1), lambda b, i: (0, 0)),
        ],
        out_specs=pl.BlockSpec((None, C_HEADS, ts, V7X_LANES), lambda b, i: (b, 0, i, 0)),
        scratch_shapes=[pltpu.VMEM((C_HEADS, V7X_LANES), F32)],
        compiler_params=pltpu.CompilerParams(dimension_semantics=("parallel", "arbitrary")),
        name="fgate_cumsum",
    )(x3, wf_t, bf)


def _rel_bucket_tiles(t):
    q = np.arange(t)[None, :]
    k = np.arange(t)[:, None]
    out = []
    for off in (0, t):
        rel = q - k + off
        n = np.maximum(rel, 0)
        max_exact = REL_BUCKETS // 2
        nf = np.maximum(n, 1).astype(np.float32)
        large = max_exact + (np.log(nf / np.float32(max_exact)) / np.float32(math.log(REL_MAX_DIST / max_exact))
                             * np.float32(REL_BUCKETS - max_exact)).astype(np.int32)
        large = np.minimum(large, REL_BUCKETS - 1)
        bkt = np.where(n < max_exact, n, large)
        out.append(np.where(rel >= 0, bkt, -1))
    return np.stack(out).astype(np.int32)


def _bias_kernel(rel_ref, bkt_ref, o_ref):
    h = pl.program_id(0)
    bkt = bkt_ref[...]
    far = rel_ref[REL_BUCKETS - 1, h]
    acc = jnp.zeros(bkt.shape, F32)
    for b in range(REL_BUCKETS):
        acc = jnp.where(bkt == b, rel_ref[b, h] - far, acc)
    o_ref[...] = jnp.where(bkt < 0, MASKED, acc * LOG2E)


def _bias_tiles(rel_bias, t):
    bkt = jnp.asarray(_rel_bucket_tiles(t))
    return pl.pallas_call(
        _bias_kernel,
        out_shape=jax.ShapeDtypeStruct((A_HEADS, 2, t, t), F32),
        grid=(A_HEADS,),
        in_specs=[
            pl.BlockSpec(memory_space=pltpu.SMEM),
            pl.BlockSpec((2, t, t), lambda h: (0, 0, 0)),
        ],
        out_specs=pl.BlockSpec((None, 2, t, t), lambda h: (h, 0, 0, 0)),
        name="rel_bias_tiles",
    )(rel_bias, bkt)


HEADS_PER_STEP = 4
PER_GROUP_WINDOW = pl.Buffered(1)


def _sublane_allreduce(x, op):
    for shift in (1, 2, 4):
        x = op(x, pltpu.roll(x, shift, 0))
    return x


def _rows3(x):
    return x.reshape(x.shape[0] // V7X_SUBLANES, V7X_SUBLANES, x.shape[1])


def _attn_scratch(hp, dv, t, c):
    stat = pltpu.VMEM((hp, V7X_SUBLANES, c), F32)
    return [stat, stat, stat, pltpu.VMEM((hp, dv, c), F32), pltpu.VMEM((hp, t, c), BF16)]


def _attn_init(m_sc, l_sc, a_sc, acc_sc, p_sc):
    m_sc[...] = jnp.full_like(m_sc, MASKED)
    l_sc[...] = jnp.zeros_like(l_sc)
    a_sc[...] = jnp.ones_like(a_sc)
    acc_sc[...] = jnp.zeros_like(acc_sc)
    p_sc[...] = jnp.zeros_like(p_sc)


def _attn_fold(vt_prev, h, a_sc, acc_sc, p_sc):
    pv = jnp.dot(vt_prev, p_sc[h], preferred_element_type=F32)
    acc_sc[h] = (_rows3(acc_sc[h]) * a_sc[h][None]).reshape(pv.shape) + pv


def _attn_stage(s, vt_prev, h, m_sc, l_sc, a_sc, acc_sc, p_sc):
    _attn_fold(vt_prev, h, a_sc, acc_sc, p_sc)
    s3 = _rows3(s)
    m_prev = m_sc[h]
    m_new = jnp.maximum(m_prev, _sublane_allreduce(jnp.max(s3, axis=0), jnp.maximum))
    alpha = jnp.exp2(m_prev - m_new)
    p3 = jnp.exp2(s3 - m_new[None])
    l_sc[h] = alpha * l_sc[h] + jnp.sum(p3, axis=0)
    p_sc[h] = p3.reshape(s.shape).astype(BF16)
    a_sc[h] = alpha
    m_sc[h] = m_new


def _attn_stage_fixed(s, vt_prev, h, m_sc, l_sc, a_sc, acc_sc, p_sc):
    _attn_fold(vt_prev, h, a_sc, acc_sc, p_sc)
    p3 = jnp.exp2(_rows3(s) - m_sc[h][None])
    l_sc[h] = l_sc[h] + jnp.sum(p3, axis=0)
    p_sc[h] = p3.reshape(s.shape).astype(BF16)
    a_sc[h] = jnp.ones_like(a_sc[h])


OVERFLOW_GUARD_LOG2 = 100


def _attn_run_guarded(run, l_sc):
    run(_attn_stage_fixed)

    @pl.when(jnp.max(l_sc[...]) > 2.0 ** OVERFLOW_GUARD_LOG2)
    def _():
        run(_attn_stage)


def _diff_attn_kernel(lam_ref, subln_ref, q_ref, k_ref, vt_ref, bias_ref, o_ref,
                      m_sc, l_sc, a_sc, acc_sc, p_sc, *, t, lam_init):
    qi = pl.program_id(2)
    state = (m_sc, l_sc, a_sc, acc_sc, p_sc)

    qqs = []
    for h in range(HEADS_PER_STEP):
        q = q_ref[:, h * A_VDIM:(h + 1) * A_VDIM]
        lane = lax.broadcasted_iota(I32, q.shape, 1)
        zero = jnp.zeros_like(q)
        qqs.append(jnp.concatenate([jnp.where(lane < A_HALF, q, zero), jnp.where(lane >= A_HALF, q, zero)], axis=0))

    lv = lam_ref[...]
    lam = (jnp.exp(jnp.sum(lv[0:1] * lv[1:2], axis=1, keepdims=True))
           - jnp.exp(jnp.sum(lv[2:3] * lv[3:4], axis=1, keepdims=True)) + lam_init)

    def tile(kj, near, stage):
        start = pl.multiple_of(kj * t, t)
        prev = jnp.minimum(kj + 1, qi)
        for h in range(HEADS_PER_STEP):
            cols = slice(h * A_VDIM, (h + 1) * A_VDIM)
            s = lax.dot_general(k_ref[pl.ds(start, t), cols], qqs[h], (((1,), (1,)), ((), ())),
                                preferred_element_type=F32)
            if near is not None:
                bias = bias_ref[h, near]
                s = s + jnp.concatenate([bias, bias], axis=1)
            stage(s, vt_ref[prev, cols, :], h, *state)

    def run(later_stage):
        _attn_init(*state)
        tile(qi, 0, _attn_stage)

        @pl.when(qi >= 1)
        def _():
            tile(qi - 1, 1, later_stage)

        def far_body(i, carry):
            tile(qi - 2 - i, None, later_stage)
            return carry

        lax.fori_loop(0, jnp.maximum(qi - 1, 0), far_body, 0)
        for h in range(HEADS_PER_STEP):
            _attn_fold(vt_ref[0, h * A_VDIM:(h + 1) * A_VDIM, :], h, a_sc, acc_sc, p_sc)
            l = _sublane_allreduce(l_sc[h], jnp.add)
            r = (_rows3(acc_sc[h]) / l[None]).reshape(A_VDIM, 2 * t)
            o = r[:, :t] - lam * r[:, t:]
            ms = _sublane_allreduce(jnp.sum(_rows3(o * o), axis=0), jnp.add) * (1.0 / A_VDIM)
            y = (_rows3(o) * lax.rsqrt(ms + LN_EPS)[None]).reshape(A_VDIM, t) * (subln_ref[...] * (1.0 - lam_init))
            o_ref[:, h * A_VDIM:(h + 1) * A_VDIM] = y.T.astype(o_ref.dtype)

    _attn_run_guarded(run, l_sc)


def _diff_attention(qk3, vt4, lam4, subln_col, bias, *, t, lam_init):
    bsz, seq, _ = qk3.shape
    nq = seq // t
    hp = HEADS_PER_STEP
    wid = hp * A_VDIM
    kblk = A_HEADS // hp
    return pl.pallas_call(
        functools.partial(_diff_attn_kernel, t=t, lam_init=lam_init),
        out_shape=jax.ShapeDtypeStruct((bsz, seq, A_HEADS * A_VDIM), BF16),
        grid=(bsz, A_HEADS // hp, nq),
        in_specs=[
            pl.BlockSpec((4, A_HALF), lambda b, h, i: (0, 0)),
            pl.BlockSpec((A_VDIM, 1), lambda b, h, i: (0, 0)),
            pl.BlockSpec((None, t, wid), lambda b, h, i: (b, i, h)),
            pl.BlockSpec((None, seq, wid), lambda b, h, i: (b, 0, kblk + h), pipeline_mode=PER_GROUP_WINDOW),
            pl.BlockSpec((None, nq, wid, t), lambda b, h, i: (b, 0, h, 0), pipeline_mode=PER_GROUP_WINDOW),
            pl.BlockSpec((hp, 2, t, t), lambda b, h, i: (h, 0, 0, 0), pipeline_mode=PER_GROUP_WINDOW),
        ],
        out_specs=pl.BlockSpec((None, t, wid), lambda b, h, i: (b, i, h)),
        scratch_shapes=_attn_scratch(hp, A_VDIM, t, 2 * t),
        compiler_params=pltpu.CompilerParams(
            dimension_semantics=("parallel", "parallel", "arbitrary"),
            vmem_limit_bytes=_vmem_limit(
                2 * _nbytes((t, wid), BF16),
                single_buffered_bytes=2 * _nbytes((seq, wid), BF16) + _nbytes((hp, 2, t, t), F32),
                scratch_bytes=_nbytes((hp, A_VDIM + t // 2 + t, 2 * t), F32)),
        ),
        name="diff_attention",
    )(lam4, subln_col, qk3, qk3, vt4, bias)


def _fox_attn_kernel(q_ref, k_ref, vt_ref, ck_ref, o_ref, m_sc, l_sc, a_sc, acc_sc, p_sc, *, t):
    qi = pl.program_id(2)
    state = (m_sc, l_sc, a_sc, acc_sc, p_sc)
    qs = [q_ref[:, h * C_HDIM:(h + 1) * C_HDIM] for h in range(HEADS_PER_STEP)]

    def tile(kj, diagonal, stage):
        start = pl.multiple_of(kj * t, t)
        prev = jnp.minimum(kj + 1, qi)
        for h in range(HEADS_PER_STEP):
            cols = slice(h * C_HDIM, (h + 1) * C_HDIM)
            s = lax.dot_general(k_ref[pl.ds(start, t), cols], qs[h], (((1,), (1,)), ((), ())),
                                preferred_element_type=F32)
            ck = ck_ref[h, pl.ds(start, t), :]
            s = s - jnp.concatenate([ck] * (t // V7X_LANES), axis=1)
            if diagonal:
                key = lax.broadcasted_iota(I32, s.shape, 0)
                qry = lax.broadcasted_iota(I32, s.shape, 1)
                s = jnp.where(key <= qry, s, MASKED)
            stage(s, vt_ref[prev, cols, :], h, *state)

    def run(later_stage):
        _attn_init(*state)
        tile(qi, True, _attn_stage)

        def body(i, carry):
            tile(qi - 1 - i, False, later_stage)
            return carry

        lax.fori_loop(0, qi, body, 0)
        for h in range(HEADS_PER_STEP):
            _attn_fold(vt_ref[0, h * C_HDIM:(h + 1) * C_HDIM, :], h, a_sc, acc_sc, p_sc)
            l = _sublane_allreduce(l_sc[h], jnp.add)
            o = (_rows3(acc_sc[h]) / l[None]).reshape(C_HDIM, t)
            o_ref[:, h * C_HDIM:(h + 1) * C_HDIM] = o.T.astype(o_ref.dtype)

    _attn_run_guarded(run, l_sc)


def _fox_attention(qk3, vt4, ckb, *, t):
    bsz, seq, _ = qk3.shape
    nq = seq // t
    hp = HEADS_PER_STEP
    wid = hp * C_HDIM
    qblk = 2 * (A_HEADS // hp)
    kblk = qblk + C_HEADS // hp
    vblk = A_HEADS // hp
    return pl.pallas_call(
        functools.partial(_fox_attn_kernel, t=t),
        out_shape=jax.ShapeDtypeStruct((bsz, seq, C_HEADS * C_HDIM), BF16),
        grid=(bsz, C_HEADS // hp, nq),
        in_specs=[
            pl.BlockSpec((None, t, wid), lambda b, h, i: (b, i, qblk + h)),
            pl.BlockSpec((None, seq, wid), lambda b, h, i: (b, 0, kblk + h), pipeline_mode=PER_GROUP_WINDOW),
            pl.BlockSpec((None, nq, wid, t), lambda b, h, i: (b, 0, vblk + h, 0), pipeline_mode=PER_GROUP_WINDOW),
            pl.BlockSpec((None, hp, seq, V7X_LANES), lambda b, h, i: (b, h, 0, 0), pipeline_mode=PER_GROUP_WINDOW),
        ],
        out_specs=pl.BlockSpec((None, t, wid), lambda b, h, i: (b, i, h)),
        scratch_shapes=_attn_scratch(hp, C_HDIM, t, t),
        compiler_params=pltpu.CompilerParams(
            dimension_semantics=("parallel", "parallel", "arbitrary"),
            vmem_limit_bytes=_vmem_limit(
                2 * _nbytes((t, wid), BF16),
                single_buffered_bytes=2 * _nbytes((seq, wid), BF16) + _nbytes((hp, seq, V7X_LANES), F32),
                scratch_bytes=_nbytes((hp, C_HDIM + t // 2 + t, t), F32)),
        ),
        name="fox_attention",
    )(qk3, qk3, vt4, ckb)


def _lru_kernel(bx_ref, bg_ref, cw_ref, cb_ref, wax_ref, bax_ref, lam_ref, o_ref, xbuf, hprev, *, t):
    pad = V7X_SUBLANES

    @pl.when(pl.program_id(1) == 0)
    def _():
        xbuf[0:pad, :] = jnp.zeros((pad, xbuf.shape[1]), F32)
        hprev[...] = jnp.zeros_like(hprev)

    xbuf[pad:pad + t, :] = bx_ref[...]
    xc = cb_ref[...]
    for tap in range(CONV_WIDTH):
        xc = xc + xbuf[pl.ds(pad - (CONV_WIDTH - 1) + tap, t), :] * cw_ref[tap:tap + 1, :]
    xbuf[0:pad, :] = bx_ref[t - pad:t, :]

    bw = xc.shape[1] // LRU_BLOCKS
    row = lax.broadcasted_iota(I32, (t, bw), 0)
    for g in range(LRU_BLOCKS):
        cols = slice(g * bw, (g + 1) * bw)
        xg = xc[:, cols]
        z = jnp.dot(xg.astype(BF16), wax_ref[g], preferred_element_type=F32) + bax_ref[g]
        r = jax.nn.sigmoid(z[:, :bw])
        gi = jax.nn.sigmoid(z[:, bw:])
        log_a = (LRU_C * r) * _log_sigmoid(lam_ref[:, cols])
        a = jnp.exp(log_a)
        u = jnp.sqrt(1.0 - jnp.exp(2.0 * log_a)) * (gi * xg)
        d = 1
        while d < t:
            keep = row >= d
            a_sh = jnp.where(keep, pltpu.roll(a, d, 0), 1.0)
            u_sh = jnp.where(keep, pltpu.roll(u, d, 0), 0.0)
            u = a * u_sh + u
            a = a * a_sh
            d *= 2
        h = a * hprev[0:1, cols] + u
        hprev[0:1, cols] = h[t - 1:t, :]
        o_ref[:, cols] = (jax.nn.gelu(bg_ref[:, cols], approximate=True) * h).astype(o_ref.dtype)


def _lru_branch(bxg, conv_w, conv_b, wax, bax, lam, *, bsz, t):
    n, two_w = bxg.shape
    w = two_w // 2
    nt = n // bsz // t
    bw = w // LRU_BLOCKS
    return pl.pallas_call(
        functools.partial(_lru_kernel, t=t),
        out_shape=jax.ShapeDtypeStruct((n, w), BF16),
        grid=(bsz, nt),
        in_specs=[
            pl.BlockSpec((t, w), lambda b, i: (b * nt + i, 0)),
            pl.BlockSpec((t, w), lambda b, i: (b * nt + i, 1)),
            pl.BlockSpec((CONV_WIDTH, w), lambda b, i: (0, 0)),
            pl.BlockSpec((1, w), lambda b, i: (0, 0)),
            pl.BlockSpec((LRU_BLOCKS, bw, 2 * bw), lambda b, i: (0, 0, 0)),
            pl.BlockSpec((LRU_BLOCKS, 1, 2 * bw), lambda b, i: (0, 0, 0)),
            pl.BlockSpec((1, w), lambda b, i: (0, 0)),
        ],
        out_specs=pl.BlockSpec((t, w), lambda b, i: (b * nt + i, 0)),
        scratch_shapes=[
            pltpu.VMEM((t + V7X_SUBLANES, w), F32),
            pltpu.VMEM((V7X_SUBLANES, w), F32),
        ],
        compiler_params=pltpu.CompilerParams(dimension_semantics=("parallel", "arbitrary")),
        name="conv_rglru",
    )(bxg, bxg, conv_w, conv_b, wax, bax, lam)


def _merge_kernel(ya_ref, yb_ref, yc_ref, w_ref, g0_ref, g1_ref, g2_ref, o_ref):
    acc = g0_ref[...].astype(F32) * jnp.dot(ya_ref[...], w_ref[0], preferred_element_type=F32)
    acc = acc + g1_ref[...].astype(F32) * jnp.dot(yb_ref[...], w_ref[1], preferred_element_type=F32)
    acc = acc + g2_ref[...].astype(F32) * jnp.dot(yc_ref[...], w_ref[2], preferred_element_type=F32)
    o_ref[...] = acc.astype(o_ref.dtype)


def _merge(ya, yb, yc, wb, gates, *, tm, tn):
    n, bwid = ya.shape
    dm = wb.shape[2]
    nc = dm // tn
    y_spec = pl.BlockSpec((tm, bwid), lambda j, i: (i, 0))
    return pl.pallas_call(
        _merge_kernel,
        out_shape=jax.ShapeDtypeStruct((n, dm), BF16),
        grid=(nc, n // tm),
        in_specs=[
            y_spec, y_spec, y_spec,
            pl.BlockSpec((N_BRANCH, bwid, tn), lambda j, i: (0, 0, j)),
            pl.BlockSpec((tm, tn), lambda j, i: (i, j)),
            pl.BlockSpec((tm, tn), lambda j, i: (i, nc + j)),
            pl.BlockSpec((tm, tn), lambda j, i: (i, 2 * nc + j)),
        ],
        out_specs=pl.BlockSpec((tm, tn), lambda j, i: (i, j)),
        compiler_params=pltpu.CompilerParams(
            dimension_semantics=("parallel", "parallel"),
            vmem_limit_bytes=_vmem_limit(3 * _nbytes((tm, bwid), BF16), _nbytes((N_BRANCH, bwid, tn), BF16),
                                         4 * _nbytes((tm, tn), BF16)),
        ),
        name="gated_merge",
    )(ya, yb, yc, wb, gates, gates, gates)


def _wo_ln_kernel(m_ref, w_ref, x_ref, g_ref, b_ref, o32_ref, o16_ref, *, alpha):
    y = jnp.dot(m_ref[...], w_ref[...], preferred_element_type=F32) + alpha * x_ref[...]
    out = _layer_norm_rows(y, g_ref[...], b_ref[...])
    o32_ref[...] = out
    o16_ref[...] = out.astype(BF16)


def _wo_ln(merged, wo, x, g, b, *, alpha, tm):
    n, dm = x.shape
    row = pl.BlockSpec((tm, dm), lambda i: (i, 0))
    vec = pl.BlockSpec((1, dm), lambda i: (0, 0))
    return pl.pallas_call(
        functools.partial(_wo_ln_kernel, alpha=alpha),
        out_shape=(jax.ShapeDtypeStruct((n, dm), F32), jax.ShapeDtypeStruct((n, dm), BF16)),
        grid=(n // tm,),
        in_specs=[row, pl.BlockSpec((dm, dm), lambda i: (0, 0)), row, vec, vec],
        out_specs=(row, row),
        compiler_params=pltpu.CompilerParams(
            dimension_semantics=("parallel",),
            vmem_limit_bytes=_vmem_limit(_nbytes((tm, dm), BF16), _nbytes((dm, dm), BF16),
                                         2 * _nbytes((tm, dm), F32), _nbytes((tm, dm), BF16)),
        ),
        name="wo_layernorm",
    )(merged, wo, x, g, b)


def _ple_kernel(x_ref, wg_ref, bg_ref, p_ref, we_ref, o_ref):
    gate = jax.nn.sigmoid(jnp.dot(x_ref[...], wg_ref[...], preferred_element_type=F32) + bg_ref[...])
    o_ref[...] = gate * jnp.dot(p_ref[...], we_ref[...], preferred_element_type=F32)


def _ple(x1b, wg, bg, pb, we, *, tm, tn):
    n, dm = x1b.shape
    pdim = pb.shape[1]
    return pl.pallas_call(
        _ple_kernel,
        out_shape=jax.ShapeDtypeStruct((n, dm), F32),
        grid=(dm // tn, n // tm),
        in_specs=[
            pl.BlockSpec((tm, dm), lambda j, i: (i, 0)),
            pl.BlockSpec((dm, tn), lambda j, i: (0, j)),
            pl.BlockSpec((1, tn), lambda j, i: (0, j)),
            pl.BlockSpec((tm, pdim), lambda j, i: (i, 0)),
            pl.BlockSpec((pdim, tn), lambda j, i: (0, j)),
        ],
        out_specs=pl.BlockSpec((tm, tn), lambda j, i: (i, j)),
        compiler_params=pltpu.CompilerParams(
            dimension_semantics=("parallel", "parallel"),
            vmem_limit_bytes=_vmem_limit(_nbytes((tm, dm), BF16), _nbytes((dm, tn), BF16),
                                         _nbytes((tm, tn), F32)),
        ),
        name="gated_ple",
    )(x1b, wg, bg, pb, we)


def _top_rows(s, payload, k):
    rows = s.shape[0]
    rid = lax.broadcasted_iota(I32, s.shape, 0)
    vals, pays = [], []
    for _ in range(k):
        m = jnp.max(s, axis=0, keepdims=True)
        pos = jnp.min(jnp.where(s == m, rid, rows), axis=0, keepdims=True)
        hit = rid == pos
        vals.append(m)
        pays.append(pos if payload is None else jnp.max(jnp.where(hit, payload, -1), axis=0, keepdims=True))
        s = jnp.where(hit, -jnp.inf, s)
    return jnp.concatenate(vals, axis=0), jnp.concatenate(pays, axis=0)


def _peer_route_kernel(x_ref, wq_ref, keys_ref, idx_ref, g_ref):
    q = jnp.dot(x_ref[...], wq_ref[...], preferred_element_type=F32).astype(BF16)
    st = lax.dot_general(keys_ref[...], q, (((1,), (1,)), ((), ())), preferred_element_type=F32)
    s1, i1 = _top_rows(st[:PEER_NKEYS], None, PEER_TOPK)
    s2, i2 = _top_rows(st[PEER_NKEYS:], None, PEER_TOPK)
    cand, cidx = [], []
    for a in range(PEER_TOPK):
        nb = PEER_TOPK // (a + 1)
        cand.append(s1[a:a + 1] + s2[:nb])
        cidx.append(i1[a:a + 1] * PEER_NKEYS + i2[:nb])
    npairs = sum(c.shape[0] for c in cand)
    npad = -npairs % V7X_SUBLANES
    cand.append(jnp.full((npad, st.shape[1]), -jnp.inf, F32))
    cidx.append(jnp.full((npad, st.shape[1]), -1, I32))
    sc, idx = _top_rows(jnp.concatenate(cand, axis=0), jnp.concatenate(cidx, axis=0), PEER_TOPK)
    e = jnp.exp(sc - jnp.max(sc, axis=0, keepdims=True))
    idx_ref[...] = idx
    g_ref[...] = e / jnp.sum(e, axis=0, keepdims=True)


def _peer_route(x1b, wq, keys_t, *, tm):
    n, dm = x1b.shape
    qd = 2 * PEER_HALF
    slots = PEER_HEADS * PEER_TOPK
    out_spec = pl.BlockSpec((PEER_TOPK, tm), lambda i, h: (h, i))
    return pl.pallas_call(
        _peer_route_kernel,
        out_shape=(jax.ShapeDtypeStruct((slots, n), I32), jax.ShapeDtypeStruct((slots, n), F32)),
        grid=(n // tm, PEER_HEADS),
        in_specs=[
            pl.BlockSpec((tm, dm), lambda i, h: (i, 0)),
            pl.BlockSpec((dm, qd), lambda i, h: (0, h)),
            pl.BlockSpec((None, 2 * PEER_NKEYS, qd), lambda i, h: (h, 0, 0)),
        ],
        out_specs=(out_spec, out_spec),
        compiler_params=pltpu.CompilerParams(dimension_semantics=("parallel", "parallel")),
        name="peer_route",
    )(x1b, wq, keys_t)


W_ROW_PITCH = PEER_NKEYS + V7X_SUBLANES


def _peer_gate_kernel(idx_ref, g_ref, o_ref, wbuf, idx_sc, g_sc, *, tb):
    idx_sc[...] = idx_ref[...].T
    g_sc[...] = g_ref[...].T
    rid = lax.broadcasted_iota(I32, (PEER_NKEYS, PEER_NKEYS), 0).astype(F32).astype(BF16)
    one = jnp.ones((PEER_NKEYS, PEER_NKEYS), BF16)
    zero = jnp.zeros((PEER_NKEYS, PEER_NKEYS), BF16)
    group = 2 * V7X_SUBLANES
    packed_rows = 2 * V7X_SUBLANES

    def body(j, carry):
        base = pl.multiple_of(j * group, group)
        ib = idx_sc[pl.ds(base, group), :]
        gb = g_sc[pl.ds(base, group), :]
        i1b = (ib >> 7).astype(F32)
        i2b = (ib & (PEER_NKEYS - 1)).astype(F32)
        for r in range(group):
            def rep(v):
                one_vreg = jnp.broadcast_to(v[r:r + 1, :], (packed_rows, PEER_NKEYS)).astype(BF16)
                return jnp.concatenate([one_vreg] * (PEER_NKEYS // packed_rows), axis=0)
            r1 = jnp.where(rid == rep(i1b), rep(gb), zero)
            r2t = jnp.where(rid == rep(i2b), one, zero)
            wt = lax.dot_general(r1, r2t, (((1,), (1,)), ((), ())), preferred_element_type=F32)
            wbuf[pl.ds(pl.multiple_of((base + r) * W_ROW_PITCH, V7X_SUBLANES), PEER_NKEYS), :] = wt
        return carry

    lax.fori_loop(0, tb // group, body, 0)
    for i1 in range(PEER_NKEYS):
        o_ref[:, i1 * PEER_NKEYS:(i1 + 1) * PEER_NKEYS] = (
            wbuf[pl.ds(i1, tb, stride=W_ROW_PITCH), :].astype(o_ref.dtype))


def _peer_gate_matrix(idx_t, g_t, *, tb):
    slots, n = idx_t.shape
    ne = PEER_NKEYS * PEER_NKEYS
    in_spec = pl.BlockSpec((slots, tb), lambda i: (0, i))
    return pl.pallas_call(
        functools.partial(_peer_gate_kernel, tb=tb),
        out_shape=jax.ShapeDtypeStruct((n, ne), BF16),
        grid=(n // tb,),
        in_specs=[in_spec, in_spec],
        out_specs=pl.BlockSpec((tb, ne), lambda i: (i, 0)),
        scratch_shapes=[
            pltpu.VMEM((tb * W_ROW_PITCH, PEER_NKEYS), F32),
            pltpu.VMEM((tb, slots), I32),
            pltpu.VMEM((tb, slots), F32),
        ],
        compiler_params=pltpu.CompilerParams(
            dimension_semantics=("parallel",),
            vmem_limit_bytes=_vmem_limit(_nbytes((tb, ne), BF16),
                                         scratch_bytes=_nbytes((tb * W_ROW_PITCH, PEER_NKEYS), F32)),
        ),
        name="peer_gate_matrix",
    )(idx_t, g_t)


def _peer_expert_kernel(x_ref, ut_ref, v_ref, w_ref, o_ref):
    @pl.when(pl.program_id(1) == 0)
    def _():
        o_ref[...] = jnp.zeros_like(o_ref)

    a = jnp.dot(x_ref[...], ut_ref[...], preferred_element_type=F32)
    gelu = 0.5 * a * (1.0 + lax.erf(a * (0.5 ** 0.5)))
    g = (w_ref[...].astype(F32) * gelu).astype(BF16)
    o_ref[...] += jnp.dot(g, v_ref[...], preferred_element_type=F32)


def _peer_experts(x1b, ut, v, w, *, tm, ce):
    n, dm = x1b.shape
    ne = v.shape[0]
    return pl.pallas_call(
        _peer_expert_kernel,
        out_shape=jax.ShapeDtypeStruct((n, dm), F32),
        grid=(n // tm, ne // ce),
        in_specs=[
            pl.BlockSpec((tm, dm), lambda i, j: (i, 0)),
            pl.BlockSpec((dm, ce), lambda i, j: (0, j)),
            pl.BlockSpec((ce, dm), lambda i, j: (j, 0)),
            pl.BlockSpec((tm, ce), lambda i, j: (i, j)),
        ],
        out_specs=pl.BlockSpec((tm, dm), lambda i, j: (i, 0)),
        compiler_params=pltpu.CompilerParams(
            dimension_semantics=("parallel", "arbitrary"),
            vmem_limit_bytes=_vmem_limit(_nbytes((tm, dm), BF16), 2 * _nbytes((dm, ce), BF16),
                                         _nbytes((tm, ce), BF16), _nbytes((tm, dm), F32)),
        ),
        name="peer_experts",
    )(x1b, ut, v, w)


def _residual_ln_kernel(x_ref, y_ref, p_ref, g_ref, b_ref, o32_ref, o16_ref, *, alpha):
    out = _layer_norm_rows(alpha * x_ref[...] + y_ref[...] + p_ref[...], g_ref[...], b_ref[...])
    o32_ref[...] = out
    o16_ref[...] = out.astype(BF16)


def _residual_ln(x1, yf, ple, g, b, *, alpha, tm):
    n, dm = x1.shape
    row = pl.BlockSpec((tm, dm), lambda i: (i, 0))
    vec = pl.BlockSpec((1, dm), lambda i: (0, 0))
    return pl.pallas_call(
        functools.partial(_residual_ln_kernel, alpha=alpha),
        out_shape=(jax.ShapeDtypeStruct((n, dm), F32), jax.ShapeDtypeStruct((n, dm), BF16)),
        grid=(n // tm,),
        in_specs=[row, row, row, vec, vec],
        out_specs=(row, row),
        compiler_params=pltpu.CompilerParams(
            dimension_semantics=("parallel",),
            vmem_limit_bytes=_vmem_limit(4 * _nbytes((tm, dm), F32), _nbytes((tm, dm), BF16)),
        ),
        name="residual_layernorm",
    )(x1, yf, ple, g, b)


def _peer_keys_layout(keys):
    z = jnp.zeros_like(keys[:, 0])
    top = jnp.concatenate([keys[:, 0], z], axis=-1)
    bot = jnp.concatenate([z, keys[:, 1]], axis=-1)
    return jnp.concatenate([top, bot], axis=1).astype(BF16)


def kernel(x, p, w_in, b_in, diff_lambda, diff_subln, rel_bias, conv_w, conv_b, lru_wa, lru_ba, lru_wx,
           lru_bx, lru_lambda, w_branch, w_o, ln1_g, ln1_b, peer_wq, peer_keys, peer_u, peer_v, w_ple,
           w_ple_gate, b_ple_gate, ln2_g, ln2_b):
    bsz, seq, dm = x.shape
    depth = w_in.shape[0]
    n = bsz * seq
    alpha = (2 * depth) ** 0.25
    bwid = A_HEADS * A_VDIM
    t_attn = 512
    assert seq % t_attn == 0 and t_attn >= REL_MAX_DIST

    o_ka, o_va, o_lru, o_c, o_kc, o_vc, o_f = (j * bwid for j in (1, 2, 3, 5, 6, 7, 8))
    o_g = o_f + C_HEADS
    qk_scale = jnp.concatenate([
        jnp.full((bwid,), A_HALF ** -0.5 * LOG2E, F32), jnp.ones((bwid,), F32),
        jnp.full((bwid,), C_HDIM ** -0.5 * LOG2E, F32), jnp.ones((bwid,), F32)])[None]
    ones_row = jnp.ones((1, max(2 * bwid, N_BRANCH * dm)), F32)

    bias = _bias_tiles(rel_bias, t_attn)
    x2 = x.reshape(n, dm)
    xb = x2.astype(BF16)
    for i in range(depth):
        w = w_in[i]
        b = b_in[i]
        w_qk = jnp.concatenate([w[:, :o_va], w[:, o_c:o_vc]], axis=1).astype(BF16)
        b_qk = jnp.concatenate([b[:o_va], b[o_c:o_vc]])[None]
        qk = _proj(xb, w_qk, b_qk, qk_scale, out_dtype=BF16, tm=1024, tn=2048, name="proj_qk")
        w_v = jnp.concatenate([w[:, o_va:o_lru], w[:, o_vc:o_f]], axis=1).T.astype(BF16)
        b_v = jnp.concatenate([b[o_va:o_lru], b[o_vc:o_f]])[:, None]
        vt = _proj_t(xb, w_v, b_v, tm=t_attn, tn=1024, name="proj_v_t")
        bxg = _proj(xb, w[:, o_lru:o_c].astype(BF16), b[None, o_lru:o_c], ones_row[:, :2 * bwid],
                    out_dtype=F32, tm=1024, tn=1024, name="proj_lru")
        gates = _proj(xb, w[:, o_g:].astype(BF16), b[None, o_g:], ones_row[:, :N_BRANCH * dm],
                      out_dtype=BF16, tm=1024, tn=2048, act="sigmoid", name="proj_gates")
        ckb = _fgate_cumsum(xb.reshape(bsz, seq, dm), w[:, o_f:o_g].T.astype(BF16), b[o_f:o_g, None],
                            ts=min(seq, 1024))

        qk3 = qk.reshape(bsz, seq, 4 * bwid)
        vt4 = vt.reshape(bsz, seq // t_attn, 2 * bwid, t_attn)
        lam_init = 0.8 - 0.6 * math.exp(-0.3 * i)
        ya = _diff_attention(qk3, vt4, diff_lambda[i], diff_subln[i][:, None], bias, t=t_attn, lam_init=lam_init)
        yc = _fox_attention(qk3, vt4, ckb, t=t_attn)

        bw = bwid // LRU_BLOCKS
        wax = jnp.concatenate([lru_wa[i], lru_wx[i]], axis=-1).astype(BF16)
        bax = jnp.concatenate([lru_ba[i].reshape(LRU_BLOCKS, 1, bw), lru_bx[i].reshape(LRU_BLOCKS, 1, bw)], axis=-1)
        yb = _lru_branch(bxg, conv_w[i], conv_b[i][None], wax, bax, lru_lambda[i][None], bsz=bsz, t=256)

        merged = _merge(ya.reshape(n, bwid), yb, yc.reshape(n, bwid), w_branch[i].astype(BF16), gates,
                        tm=512, tn=1024)
        x1, x1b = _wo_ln(merged, w_o[i].astype(BF16), x2, ln1_g[i][None], ln1_b[i][None], alpha=alpha, tm=256)

        ple = _ple(x1b, w_ple_gate[i].astype(BF16), b_ple_gate[i][None], p[i].reshape(n, -1).astype(BF16),
                   w_ple[i].astype(BF16), tm=1024, tn=1024)
        idx_t, g_t = _peer_route(x1b, peer_wq[i].astype(BF16), _peer_keys_layout(peer_keys[i]), tm=1024)
        wdense = _peer_gate_matrix(idx_t, g_t, tb=128)
        yf = _peer_experts(x1b, peer_u[i].T.astype(BF16), peer_v[i].astype(BF16), wdense, tm=1024, ce=512)
        x2, xb = _residual_ln(x1, yf, ple, ln2_g[i][None], ln2_b[i][None], alpha=alpha, tm=512)
    return x2.reshape(bsz, seq, dm)
```

```python
import functools
import math

import jax
import jax.numpy as jnp
import numpy as np
from jax import lax
from jax.experimental import pallas as pl
from jax.experimental.pallas import tpu as pltpu

F32, BF16, I32 = jnp.float32, jnp.bfloat16, jnp.int32

V7X_VMEM_BYTES = 64 * 2**20
V7X_LANES = 128
V7X_SUBLANES = 8
VMEM_HEADROOM_BYTES = 8 * 2**20

A_HEADS = 8
A_HALF = 64
A_VDIM = 2 * A_HALF
LRU_BLOCKS = 8
CONV_WIDTH = 4
LRU_C = 8.0
C_HEADS = 8
C_HDIM = 128
N_BRANCH = 3
REL_BUCKETS = 32
REL_MAX_DIST = 128
PEER_HEADS = 8
PEER_NKEYS = 128
PEER_HALF = 64
PEER_TOPK = 16
LN_EPS = 1e-5

LOG2E = 1.4426950408889634
MASKED = -1e30


def _vmem_limit(*block_bytes, scratch_bytes=0, single_buffered_bytes=0):
    need = 2 * sum(block_bytes) + single_buffered_bytes + scratch_bytes + VMEM_HEADROOM_BYTES
    return int(min(max(need, 32 * 2**20), V7X_VMEM_BYTES - 4 * 2**20))


def _nbytes(shape, dtype):
    return int(np.prod(shape)) * jnp.dtype(dtype).itemsize


def _log_sigmoid(z):
    return jnp.minimum(z, 0.0) - jnp.log1p(jnp.exp(-jnp.abs(z)))


def _layer_norm_rows(y, g, b):
    mu = jnp.mean(y, axis=1, keepdims=True)
    yc = y - mu
    var = jnp.mean(yc * yc, axis=1, keepdims=True)
    return yc * lax.rsqrt(var + LN_EPS) * g + b


def _proj_kernel(x_ref, w_ref, b_ref, s_ref, o_ref, *, act):
    y = jnp.dot(x_ref[...], w_ref[...], preferred_element_type=F32)
    y = (y + b_ref[...]) * s_ref[...]
    if act == "sigmoid":
        y = jax.nn.sigmoid(y)
    o_ref[...] = y.astype(o_ref.dtype)


def _proj(x, w, b, s, *, out_dtype, tm, tn, act=None, name):
    m, k = x.shape
    n = w.shape[1]
    grid = (n // tn, m // tm)
    return pl.pallas_call(
        functools.partial(_proj_kernel, act=act),
        out_shape=jax.ShapeDtypeStruct((m, n), out_dtype),
        grid=grid,
        in_specs=[
            pl.BlockSpec((tm, k), lambda j, i: (i, 0)),
            pl.BlockSpec((k, tn), lambda j, i: (0, j)),
            pl.BlockSpec((1, tn), lambda j, i: (0, j)),
            pl.BlockSpec((1, tn), lambda j, i: (0, j)),
        ],
        out_specs=pl.BlockSpec((tm, tn), lambda j, i: (i, j)),
        compiler_params=pltpu.CompilerParams(
            dimension_semantics=("parallel", "parallel"),
            vmem_limit_bytes=_vmem_limit(_nbytes((tm, k), x.dtype), _nbytes((k, tn), w.dtype),
                                         _nbytes((tm, tn), out_dtype)),
        ),
        name=name,
    )(x, w, b, s)


def _proj_t_kernel(w_ref, x_ref, b_ref, o_ref):
    y = lax.dot_general(w_ref[...], x_ref[...], (((1,), (1,)), ((), ())), preferred_element_type=F32)
    o_ref[...] = (y + b_ref[...]).astype(o_ref.dtype)


def _proj_t(x, w_t, b_col, *, tm, tn, name):
    m, k = x.shape
    n = w_t.shape[0]
    return pl.pallas_call(
        _proj_t_kernel,
        out_shape=jax.ShapeDtypeStruct((m // tm, n, tm), BF16),
        grid=(n // tn, m // tm),
        in_specs=[
            pl.BlockSpec((tn, k), lambda j, i: (j, 0)),
            pl.BlockSpec((tm, k), lambda j, i: (i, 0)),
            pl.BlockSpec((tn, 1), lambda j, i: (j, 0)),
        ],
        out_specs=pl.BlockSpec((None, tn, tm), lambda j, i: (i, j, 0)),
        compiler_params=pltpu.CompilerParams(
            dimension_semantics=("parallel", "parallel"),
            vmem_limit_bytes=_vmem_limit(_nbytes((tm, k), BF16), _nbytes((tn, k), BF16), _nbytes((tn, tm), BF16)),
        ),
        name=name,
    )(w_t, x, b_col)


def _fgate_kernel(x_ref, w_ref, b_ref, o_ref, carry_ref, *, ts):
    @pl.when(pl.program_id(1) == 0)
    def _():
        carry_ref[...] = jnp.zeros_like(carry_ref)

    z = lax.dot_general(w_ref[...], x_ref[...], (((1,), (1,)), ((), ())),
                        preferred_element_type=F32) + b_ref[...]
    c = _log_sigmoid(z)
    lane = lax.broadcasted_iota(I32, c.shape, 1)
    d = 1
    while d < ts:
        c = c + jnp.where(lane >= d, pltpu.roll(c, d, 1), 0.0)
        d *= 2
    c = c + carry_ref[:, 0:1]
    carry_ref[...] = jnp.broadcast_to(c[:, ts - 1:ts], carry_ref.shape)
    c = c * LOG2E
    for h in range(C_HEADS):
        o_ref[h] = jnp.broadcast_to(c[h:h + 1, :], (V7X_LANES, ts)).T


def _fgate_cumsum(x3, wf_t, bf, *, ts):
    bsz, seq, dm = x3.shape
    return pl.pallas_call(
        functools.partial(_fgate_kernel, ts=ts),
        out_shape=jax.ShapeDtypeStruct((bsz, C_HEADS, seq, V7X_LANES), F32),
        grid=(bsz, seq // ts),
        in_specs=[
            pl.BlockSpec((None, ts, dm), lambda b, i: (b, i, 0)),
            pl.BlockSpec((C_HEADS, dm), lambda b, i: (0, 0)),
            pl.BlockSpec((C_HEADS, 1), lambda b, i: (0, 0)),
        ],
        out_specs=pl.BlockSpec((None, C_HEADS, ts, V7X_LANES), lambda b, i: (b, 0, i, 0)),
        scratch_shapes=[pltpu.VMEM((C_HEADS, V7X_LANES), F32)],
        compiler_params=pltpu.CompilerParams(dimension_semantics=("parallel", "arbitrary")),
        name="fgate_cumsum",
    )(x3, wf_t, bf)


def _rel_bucket_tiles(t):
    q = np.arange(t)[None, :]
    k = np.arange(t)[:, None]
    out = []
    for off in (0, t):
        rel = q - k + off
        n = np.maximum(rel, 0)
        max_exact = REL_BUCKETS // 2
        nf = np.maximum(n, 1).astype(np.float32)
        large = max_exact + (np.log(nf / np.float32(max_exact)) / np.float32(math.log(REL_MAX_DIST / max_exact))
                             * np.float32(REL_BUCKETS - max_exact)).astype(np.int32)
        large = np.minimum(large, REL_BUCKETS - 1)
        bkt = np.where(n < max_exact, n, large)
        out.append(np.where(rel >= 0, bkt, -1))
    return np.stack(out).astype(np.int32)


def _bias_kernel(rel_ref, bkt_ref, o_ref):
    h = pl.program_id(0)
    bkt = bkt_ref[...]
    far = rel_ref[REL_BUCKETS - 1, h]
    acc = jnp.zeros(bkt.shape, F32)
    for b in range(REL_BUCKETS):
        acc = jnp.where(bkt == b, rel_ref[b, h] - far, acc)
    o_ref[...] = jnp.where(bkt < 0, MASKED, acc * LOG2E)


def _bias_tiles(rel_bias, t):
    bkt = jnp.asarray(_rel_bucket_tiles(t))
    return pl.pallas_call(
        _bias_kernel,
        out_shape=jax.ShapeDtypeStruct((A_HEADS, 2, t, t), F32),
        grid=(A_HEADS,),
        in_specs=[
            pl.BlockSpec(memory_space=pltpu.SMEM),
            pl.BlockSpec((2, t, t), lambda h: (0, 0, 0)),
        ],
        out_specs=pl.BlockSpec((None, 2, t, t), lambda h: (h, 0, 0, 0)),
        name="rel_bias_tiles",
    )(rel_bias, bkt)


HEADS_PER_STEP = 4
PER_GROUP_WINDOW = pl.Buffered(1)


def _sublane_allreduce(x, op):
    for shift in (1, 2, 4):
        x = op(x, pltpu.roll(x, shift, 0))
    return x


def _rows3(x):
    return x.reshape(x.shape[0] // V7X_SUBLANES, V7X_SUBLANES, x.shape[1])


def _attn_scratch(hp, dv, t, c):
    stat = pltpu.VMEM((hp, V7X_SUBLANES, c), F32)
    return [stat, stat, stat, pltpu.VMEM((hp, dv, c), F32), pltpu.VMEM((hp, t, c), BF16)]


def _attn_init(m_sc, l_sc, a_sc, acc_sc, p_sc):
    m_sc[...] = jnp.full_like(m_sc, MASKED)
    l_sc[...] = jnp.zeros_like(l_sc)
    a_sc[...] = jnp.ones_like(a_sc)
    acc_sc[...] = jnp.zeros_like(acc_sc)
    p_sc[...] = jnp.zeros_like(p_sc)


def _attn_fold(vt_prev, h, a_sc, acc_sc, p_sc):
    pv = jnp.dot(vt_prev, p_sc[h], preferred_element_type=F32)
    acc_sc[h] = (_rows3(acc_sc[h]) * a_sc[h][None]).reshape(pv.shape) + pv


def _attn_stage(s, vt_prev, h, m_sc, l_sc, a_sc, acc_sc, p_sc):
    _attn_fold(vt_prev, h, a_sc, acc_sc, p_sc)
    s3 = _rows3(s)
    m_prev = m_sc[h]
    m_new = jnp.maximum(m_prev, _sublane_allreduce(jnp.max(s3, axis=0), jnp.maximum))
    alpha = jnp.exp2(m_prev - m_new)
    p3 = jnp.exp2(s3 - m_new[None])
    l_sc[h] = alpha * l_sc[h] + jnp.sum(p3, axis=0)
    p_sc[h] = p3.reshape(s.shape).astype(BF16)
    a_sc[h] = alpha
    m_sc[h] = m_new


def _attn_stage_fixed(s, vt_prev, h, m_sc, l_sc, a_sc, acc_sc, p_sc):
    _attn_fold(vt_prev, h, a_sc, acc_sc, p_sc)
    p3 = jnp.exp2(_rows3(s) - m_sc[h][None])
    l_sc[h] = l_sc[h] + jnp.sum(p3, axis=0)
    p_sc[h] = p3.reshape(s.shape).astype(BF16)
    a_sc[h] = jnp.ones_like(a_sc[h])


OVERFLOW_GUARD_LOG2 = 100


def _attn_run_guarded(run, l_sc):
    run(_attn_stage_fixed)

    @pl.when(jnp.max(l_sc[...]) > 2.0 ** OVERFLOW_GUARD_LOG2)
    def _():
        run(_attn_stage)


def _diff_attn_kernel(lam_ref, subln_ref, q_ref, k_ref, vt_ref, bias_ref, o_ref,
                      m_sc, l_sc, a_sc, acc_sc, p_sc, *, t, lam_init):
    qi = pl.program_id(2)
    state = (m_sc, l_sc, a_sc, acc_sc, p_sc)

    qqs = []
    for h in range(HEADS_PER_STEP):
        q = q_ref[:, h * A_VDIM:(h + 1) * A_VDIM]
        lane = lax.broadcasted_iota(I32, q.shape, 1)
        zero = jnp.zeros_like(q)
        qqs.append(jnp.concatenate([jnp.where(lane < A_HALF, q, zero), jnp.where(lane >= A_HALF, q, zero)], axis=0))

    lv = lam_ref[...]
    lam = (jnp.exp(jnp.sum(lv[0:1] * lv[1:2], axis=1, keepdims=True))
           - jnp.exp(jnp.sum(lv[2:3] * lv[3:4], axis=1, keepdims=True)) + lam_init)

    def tile(kj, near, stage):
        start = pl.multiple_of(kj * t, t)
        prev = jnp.minimum(kj + 1, qi)
        for h in range(HEADS_PER_STEP):
            cols = slice(h * A_VDIM, (h + 1) * A_VDIM)
            s = lax.dot_general(k_ref[pl.ds(start, t), cols], qqs[h], (((1,), (1,)), ((), ())),
                                preferred_element_type=F32)
            if near is not None:
                bias = bias_ref[h, near]
                s = s + jnp.concatenate([bias, bias], axis=1)
            stage(s, vt_ref[prev, cols, :], h, *state)

    def run(later_stage):
        _attn_init(*state)
        tile(qi, 0, _attn_stage)

        @pl.when(qi >= 1)
        def _():
            tile(qi - 1, 1, later_stage)

        def far_body(i, carry):
            tile(qi - 2 - i, None, later_stage)
            return carry

        lax.fori_loop(0, jnp.maximum(qi - 1, 0), far_body, 0)
        for h in range(HEADS_PER_STEP):
            _attn_fold(vt_ref[0, h * A_VDIM:(h + 1) * A_VDIM, :], h, a_sc, acc_sc, p_sc)
            l = _sublane_allreduce(l_sc[h], jnp.add)
            r = (_rows3(acc_sc[h]) / l[None]).reshape(A_VDIM, 2 * t)
            o = r[:, :t] - lam * r[:, t:]
            ms = _sublane_allreduce(jnp.sum(_rows3(o * o), axis=0), jnp.add) * (1.0 / A_VDIM)
            y = (_rows3(o) * lax.rsqrt(ms + LN_EPS)[None]).reshape(A_VDIM, t) * (subln_ref[...] * (1.0 - lam_init))
            o_ref[:, h * A_VDIM:(h + 1) * A_VDIM] = y.T.astype(o_ref.dtype)

    _attn_run_guarded(run, l_sc)


def _diff_attention(qk3, vt4, lam4, subln_col, bias, *, t, lam_init):
    bsz, seq, _ = qk3.shape
    nq = seq // t
    hp = HEADS_PER_STEP
    wid = hp * A_VDIM
    kblk = A_HEADS // hp
    return pl.pallas_call(
        functools.partial(_diff_attn_kernel, t=t, lam_init=lam_init),
        out_shape=jax.ShapeDtypeStruct((bsz, seq, A_HEADS * A_VDIM), BF16),
        grid=(bsz, A_HEADS // hp, nq),
        in_specs=[
            pl.BlockSpec((4, A_HALF), lambda b, h, i: (0, 0)),
            pl.BlockSpec((A_VDIM, 1), lambda b, h, i: (0, 0)),
            pl.BlockSpec((None, t, wid), lambda b, h, i: (b, i, h)),
            pl.BlockSpec((None, seq, wid), lambda b, h, i: (b, 0, kblk + h), pipeline_mode=PER_GROUP_WINDOW),
            pl.BlockSpec((None, nq, wid, t), lambda b, h, i: (b, 0, h, 0), pipeline_mode=PER_GROUP_WINDOW),
            pl.BlockSpec((hp, 2, t, t), lambda b, h, i: (h, 0, 0, 0), pipeline_mode=PER_GROUP_WINDOW),
        ],
        out_specs=pl.BlockSpec((None, t, wid), lambda b, h, i: (b, i, h)),
        scratch_shapes=_attn_scratch(hp, A_VDIM, t, 2 * t),
        compiler_params=pltpu.CompilerParams(
            dimension_semantics=("parallel", "parallel", "arbitrary"),
            vmem_limit_bytes=_vmem_limit(
                2 * _nbytes((t, wid), BF16),
                single_buffered_bytes=2 * _nbytes((seq, wid), BF16) + _nbytes((hp, 2, t, t), F32),
                scratch_bytes=_nbytes((hp, A_VDIM + t // 2 + t, 2 * t), F32)),
        ),
        name="diff_attention",
    )(lam4, subln_col, qk3, qk3, vt4, bias)


def _fox_attn_kernel(q_ref, k_ref, vt_ref, ck_ref, o_ref, m_sc, l_sc, a_sc, acc_sc, p_sc, *, t):
    qi = pl.program_id(2)
    state = (m_sc, l_sc, a_sc, acc_sc, p_sc)
    qs = [q_ref[:, h * C_HDIM:(h + 1) * C_HDIM] for h in range(HEADS_PER_STEP)]

    def tile(kj, diagonal, stage):
        start = pl.multiple_of(kj * t, t)
        prev = jnp.minimum(kj + 1, qi)
        for h in range(HEADS_PER_STEP):
            cols = slice(h * C_HDIM, (h + 1) * C_HDIM)
            s = lax.dot_general(k_ref[pl.ds(start, t), cols], qs[h], (((1,), (1,)), ((), ())),
                                preferred_element_type=F32)
            ck = ck_ref[h, pl.ds(start, t), :]
            s = s - jnp.concatenate([ck] * (t // V7X_LANES), axis=1)
            if diagonal:
                key = lax.broadcasted_iota(I32, s.shape, 0)
                qry = lax.broadcasted_iota(I32, s.shape, 1)
                s = jnp.where(key <= qry, s, MASKED)
            stage(s, vt_ref[prev, cols, :], h, *state)

    def run(later_stage):
        _attn_init(*state)
        tile(qi, True, _attn_stage)

        def body(i, carry):
            tile(qi - 1 - i, False, later_stage)
            return carry

        lax.fori_loop(0, qi, body, 0)
        for h in range(HEADS_PER_STEP):
            _attn_fold(vt_ref[0, h * C_HDIM:(h + 1) * C_HDIM, :], h, a_sc, acc_sc, p_sc)
            l = _sublane_allreduce(l_sc[h], jnp.add)
            o = (_rows3(acc_sc[h]) / l[None]).reshape(C_HDIM, t)
            o_ref[:, h * C_HDIM:(h + 1) * C_HDIM] = o.T.astype(o_ref.dtype)

    _attn_run_guarded(run, l_sc)


def _fox_attention(qk3, vt4, ckb, *, t):
    bsz, seq, _ = qk3.shape
    nq = seq // t
    hp = HEADS_PER_STEP
    wid = hp * C_HDIM
    qblk = 2 * (A_HEADS // hp)
    kblk = qblk + C_HEADS // hp
    vblk = A_HEADS // hp
    return pl.pallas_call(
        functools.partial(_fox_attn_kernel, t=t),
        out_shape=jax.ShapeDtypeStruct((bsz, seq, C_HEADS * C_HDIM), BF16),
        grid=(bsz, C_HEADS // hp, nq),
        in_specs=[
            pl.BlockSpec((None, t, wid), lambda b, h, i: (b, i, qblk + h)),
            pl.BlockSpec((None, seq, wid), lambda b, h, i: (b, 0, kblk + h), pipeline_mode=PER_GROUP_WINDOW),
            pl.BlockSpec((None, nq, wid, t), lambda b, h, i: (b, 0, vblk + h, 0), pipeline_mode=PER_GROUP_WINDOW),
            pl.BlockSpec((None, hp, seq, V7X_LANES), lambda b, h, i: (b, h, 0, 0), pipeline_mode=PER_GROUP_WINDOW),
        ],
        out_specs=pl.BlockSpec((None, t, wid), lambda b, h, i: (b, i, h)),
        scratch_shapes=_attn_scratch(hp, C_HDIM, t, t),
        compiler_params=pltpu.CompilerParams(
            dimension_semantics=("parallel", "parallel", "arbitrary"),
            vmem_limit_bytes=_vmem_limit(
                2 * _nbytes((t, wid), BF16),
                single_buffered_bytes=2 * _nbytes((seq, wid), BF16) + _nbytes((hp, seq, V7X_LANES), F32),
                scratch_bytes=_nbytes((hp, C_HDIM + t // 2 + t, t), F32)),
        ),
        name="fox_attention",
    )(qk3, qk3, vt4, ckb)


def _lru_kernel(bx_ref, bg_ref, cw_ref, cb_ref, wax_ref, bax_ref, lam_ref, o_ref, xbuf, hprev, *, t):
    pad = V7X_SUBLANES

    @pl.when(pl.program_id(1) == 0)
    def _():
        xbuf[0:pad, :] = jnp.zeros((pad, xbuf.shape[1]), F32)
        hprev[...] = jnp.zeros_like(hprev)

    xbuf[pad:pad + t, :] = bx_ref[...]
    xc = cb_ref[...]
    for tap in range(CONV_WIDTH):
        xc = xc + xbuf[pl.ds(pad - (CONV_WIDTH - 1) + tap, t), :] * cw_ref[tap:tap + 1, :]
    xbuf[0:pad, :] = bx_ref[t - pad:t, :]

    bw = xc.shape[1] // LRU_BLOCKS
    row = lax.broadcasted_iota(I32, (t, bw), 0)
    for g in range(LRU_BLOCKS):
        cols = slice(g * bw, (g + 1) * bw)
        xg = xc[:, cols]
        z = jnp.dot(xg.astype(BF16), wax_ref[g], preferred_element_type=F32) + bax_ref[g]
        r = jax.nn.sigmoid(z[:, :bw])
        gi = jax.nn.sigmoid(z[:, bw:])
        log_a = (LRU_C * r) * _log_sigmoid(lam_ref[:, cols])
        a = jnp.exp(log_a)
        u = jnp.sqrt(1.0 - jnp.exp(2.0 * log_a)) * (gi * xg)
        d = 1
        while d < t:
            keep = row >= d
            a_sh = jnp.where(keep, pltpu.roll(a, d, 0), 1.0)
            u_sh = jnp.where(keep, pltpu.roll(u, d, 0), 0.0)
            u = a * u_sh + u
            a = a * a_sh
            d *= 2
        h = a * hprev[0:1, cols] + u
        hprev[0:1, cols] = h[t - 1:t, :]
        o_ref[:, cols] = (jax.nn.gelu(bg_ref[:, cols], approximate=True) * h).astype(o_ref.dtype)


def _lru_branch(bxg, conv_w, conv_b, wax, bax, lam, *, bsz, t):
    n, two_w = bxg.shape
    w = two_w // 2
    nt = n // bsz // t
    bw = w // LRU_BLOCKS
    return pl.pallas_call(
        functools.partial(_lru_kernel, t=t),
        out_shape=jax.ShapeDtypeStruct((n, w), BF16),
        grid=(bsz, nt),
        in_specs=[
            pl.BlockSpec((t, w), lambda b, i: (b * nt + i, 0)),
            pl.BlockSpec((t, w), lambda b, i: (b * nt + i, 1)),
            pl.BlockSpec((CONV_WIDTH, w), lambda b, i: (0, 0)),
            pl.BlockSpec((1, w), lambda b, i: (0, 0)),
            pl.BlockSpec((LRU_BLOCKS, bw, 2 * bw), lambda b, i: (0, 0, 0)),
            pl.BlockSpec((LRU_BLOCKS, 1, 2 * bw), lambda b, i: (0, 0, 0)),
            pl.BlockSpec((1, w), lambda b, i: (0, 0)),
        ],
        out_specs=pl.BlockSpec((t, w), lambda b, i: (b * nt + i, 0)),
        scratch_shapes=[
            pltpu.VMEM((t + V7X_SUBLANES, w), F32),
            pltpu.VMEM((V7X_SUBLANES, w), F32),
        ],
        compiler_params=pltpu.CompilerParams(dimension_semantics=("parallel", "arbitrary")),
        name="conv_rglru",
    )(bxg, bxg, conv_w, conv_b, wax, bax, lam)


def _merge_kernel(ya_ref, yb_ref, yc_ref, w_ref, g0_ref, g1_ref, g2_ref, o_ref):
    acc = g0_ref[...].astype(F32) * jnp.dot(ya_ref[...], w_ref[0], preferred_element_type=F32)
    acc = acc + g1_ref[...].astype(F32) * jnp.dot(yb_ref[...], w_ref[1], preferred_element_type=F32)
    acc = acc + g2_ref[...].astype(F32) * jnp.dot(yc_ref[...], w_ref[2], preferred_element_type=F32)
    o_ref[...] = acc.astype(o_ref.dtype)


def _merge(ya, yb, yc, wb, gates, *, tm, tn):
    n, bwid = ya.shape
    dm = wb.shape[2]
    nc = dm // tn
    y_spec = pl.BlockSpec((tm, bwid), lambda j, i: (i, 0))
    return pl.pallas_call(
        _merge_kernel,
        out_shape=jax.ShapeDtypeStruct((n, dm), BF16),
        grid=(nc, n // tm),
        in_specs=[
            y_spec, y_spec, y_spec,
            pl.BlockSpec((N_BRANCH, bwid, tn), lambda j, i: (0, 0, j)),
            pl.BlockSpec((tm, tn), lambda j, i: (i, j)),
            pl.BlockSpec((tm, tn), lambda j, i: (i, nc + j)),
            pl.BlockSpec((tm, tn), lambda j, i: (i, 2 * nc + j)),
        ],
        out_specs=pl.BlockSpec((tm, tn), lambda j, i: (i, j)),
        compiler_params=pltpu.CompilerParams(
            dimension_semantics=("parallel", "parallel"),
            vmem_limit_bytes=_vmem_limit(3 * _nbytes((tm, bwid), BF16), _nbytes((N_BRANCH, bwid, tn), BF16),
                                         4 * _nbytes((tm, tn), BF16)),
        ),
        name="gated_merge",
    )(ya, yb, yc, wb, gates, gates, gates)


def _wo_ln_kernel(m_ref, w_ref, x_ref, g_ref, b_ref, o32_ref, o16_ref, *, alpha):
    y = jnp.dot(m_ref[...], w_ref[...], preferred_element_type=F32) + alpha * x_ref[...]
    out = _layer_norm_rows(y, g_ref[...], b_ref[...])
    o32_ref[...] = out
    o16_ref[...] = out.astype(BF16)


def _wo_ln(merged, wo, x, g, b, *, alpha, tm):
    n, dm = x.shape
    row = pl.BlockSpec((tm, dm), lambda i: (i, 0))
    vec = pl.BlockSpec((1, dm), lambda i: (0, 0))
    return pl.pallas_call(
        functools.partial(_wo_ln_kernel, alpha=alpha),
        out_shape=(jax.ShapeDtypeStruct((n, dm), F32), jax.ShapeDtypeStruct((n, dm), BF16)),
        grid=(n // tm,),
        in_specs=[row, pl.BlockSpec((dm, dm), lambda i: (0, 0)), row, vec, vec],
        out_specs=(row, row),
        compiler_params=pltpu.CompilerParams(
            dimension_semantics=("parallel",),
            vmem_limit_bytes=_vmem_limit(_nbytes((tm, dm), BF16), _nbytes((dm, dm), BF16),
                                         2 * _nbytes((tm, dm), F32), _nbytes((tm, dm), BF16)),
        ),
        name="wo_layernorm",
    )(merged, wo, x, g, b)


def _ple_kernel(x_ref, wg_ref, bg_ref, p_ref, we_ref, o_ref):
    gate = jax.nn.sigmoid(jnp.dot(x_ref[...], wg_ref[...], preferred_element_type=F32) + bg_ref[...])
    o_ref[...] = gate * jnp.dot(p_ref[...], we_ref[...], preferred_element_type=F32)


def _ple(x1b, wg, bg, pb, we, *, tm, tn):
    n, dm = x1b.shape
    pdim = pb.shape[1]
    return pl.pallas_call(
        _ple_kernel,
        out_shape=jax.ShapeDtypeStruct((n, dm), F32),
        grid=(dm // tn, n // tm),
        in_specs=[
            pl.BlockSpec((tm, dm), lambda j, i: (i, 0)),
            pl.BlockSpec((dm, tn), lambda j, i: (0, j)),
            pl.BlockSpec((1, tn), lambda j, i: (0, j)),
            pl.BlockSpec((tm, pdim), lambda j, i: (i, 0)),
            pl.BlockSpec((pdim, tn), lambda j, i: (0, j)),
        ],
        out_specs=pl.BlockSpec((tm, tn), lambda j, i: (i, j)),
        compiler_params=pltpu.CompilerParams(
            dimension_semantics=("parallel", "parallel"),
            vmem_limit_bytes=_vmem_limit(_nbytes((tm, dm), BF16), _nbytes((dm, tn), BF16),
                                         _nbytes((tm, tn), F32)),
        ),
        name="gated_ple",
    )(x1b, wg, bg, pb, we)


def _top_rows(s, payload, k):
    rows = s.shape[0]
    rid = lax.broadcasted_iota(I32, s.shape, 0)
    vals, pays = [], []
    for _ in range(k):
        m = jnp.max(s, axis=0, keepdims=True)
        pos = jnp.min(jnp.where(s == m, rid, rows), axis=0, keepdims=True)
        hit = rid == pos
        vals.append(m)
        pays.append(pos if payload is None else jnp.max(jnp.where(hit, payload, -1), axis=0, keepdims=True))
        s = jnp.where(hit, -jnp.inf, s)
    return jnp.concatenate(vals, axis=0), jnp.concatenate(pays, axis=0)


def _peer_route_kernel(x_ref, wq_ref, keys_ref, idx_ref, g_ref):
    q = jnp.dot(x_ref[...], wq_ref[...], preferred_element_type=F32).astype(BF16)
    st = lax.dot_general(keys_ref[...], q, (((1,), (1,)), ((), ())), preferred_element_type=F32)
    s1, i1 = _top_rows(st[:PEER_NKEYS], None, PEER_TOPK)
    s2, i2 = _top_rows(st[PEER_NKEYS:], None, PEER_TOPK)
    cand, cidx = [], []
    for a in range(PEER_TOPK):
        nb = PEER_TOPK // (a + 1)
        cand.append(s1[a:a + 1] + s2[:nb])
        cidx.append(i1[a:a + 1] * PEER_NKEYS + i2[:nb])
    npairs = sum(c.shape[0] for c in cand)
    npad = -npairs % V7X_SUBLANES
    cand.append(jnp.full((npad, st.shape[1]), -jnp.inf, F32))
    cidx.append(jnp.full((npad, st.shape[1]), -1, I32))
    sc, idx = _top_rows(jnp.concatenate(cand, axis=0), jnp.concatenate(cidx, axis=0), PEER_TOPK)
    e = jnp.exp(sc - jnp.max(sc, axis=0, keepdims=True))
    idx_ref[...] = idx
    g_ref[...] = e / jnp.sum(e, axis=0, keepdims=True)


def _peer_route(x1b, wq, keys_t, *, tm):
    n, dm = x1b.shape
    qd = 2 * PEER_HALF
    slots = PEER_HEADS * PEER_TOPK
    out_spec = pl.BlockSpec((PEER_TOPK, tm), lambda i, h: (h, i))
    return pl.pallas_call(
        _peer_route_kernel,
        out_shape=(jax.ShapeDtypeStruct((slots, n), I32), jax.ShapeDtypeStruct((slots, n), F32)),
        grid=(n // tm, PEER_HEADS),
        in_specs=[
            pl.BlockSpec((tm, dm), lambda i, h: (i, 0)),
            pl.BlockSpec((dm, qd), lambda i, h: (0, h)),
            pl.BlockSpec((None, 2 * PEER_NKEYS, qd), lambda i, h: (h, 0, 0)),
        ],
        out_specs=(out_spec, out_spec),
        compiler_params=pltpu.CompilerParams(dimension_semantics=("parallel", "parallel")),
        name="peer_route",
    )(x1b, wq, keys_t)


W_ROW_PITCH = PEER_NKEYS + V7X_SUBLANES


def _peer_gate_kernel(idx_ref, g_ref, o_ref, wbuf, idx_sc, g_sc, *, tb):
    idx_sc[...] = idx_ref[...].T
    g_sc[...] = g_ref[...].T
    rid = lax.broadcasted_iota(I32, (PEER_NKEYS, PEER_NKEYS), 0).astype(F32).astype(BF16)
    one = jnp.ones((PEER_NKEYS, PEER_NKEYS), BF16)
    zero = jnp.zeros((PEER_NKEYS, PEER_NKEYS), BF16)
    group = 2 * V7X_SUBLANES
    packed_rows = 2 * V7X_SUBLANES

    def body(j, carry):
        base = pl.multiple_of(j * group, group)
        ib = idx_sc[pl.ds(base, group), :]
        gb = g_sc[pl.ds(base, group), :]
        i1b = (ib >> 7).astype(F32)
        i2b = (ib & (PEER_NKEYS - 1)).astype(F32)
        for r in range(group):
            def rep(v):
                one_vreg = jnp.broadcast_to(v[r:r + 1, :], (packed_rows, PEER_NKEYS)).astype(BF16)
                return jnp.concatenate([one_vreg] * (PEER_NKEYS // packed_rows), axis=0)
            r1 = jnp.where(rid == rep(i1b), rep(gb), zero)
            r2t = jnp.where(rid == rep(i2b), one, zero)
            wt = lax.dot_general(r1, r2t, (((1,), (1,)), ((), ())), preferred_element_type=F32)
            wbuf[pl.ds(pl.multiple_of((base + r) * W_ROW_PITCH, V7X_SUBLANES), PEER_NKEYS), :] = wt
        return carry

    lax.fori_loop(0, tb // group, body, 0)
    for i1 in range(PEER_NKEYS):
        o_ref[:, i1 * PEER_NKEYS:(i1 + 1) * PEER_NKEYS] = (
            wbuf[pl.ds(i1, tb, stride=W_ROW_PITCH), :].astype(o_ref.dtype))


def _peer_gate_matrix(idx_t, g_t, *, tb):
    slots, n = idx_t.shape
    ne = PEER_NKEYS * PEER_NKEYS
    in_spec = pl.BlockSpec((slots, tb), lambda i: (0, i))
    return pl.pallas_call(
        functools.partial(_peer_gate_kernel, tb=tb),
        out_shape=jax.ShapeDtypeStruct((n, ne), BF16),
        grid=(n // tb,),
        in_specs=[in_spec, in_spec],
        out_specs=pl.BlockSpec((tb, ne), lambda i: (i, 0)),
        scratch_shapes=[
            pltpu.VMEM((tb * W_ROW_PITCH, PEER_NKEYS), F32),
            pltpu.VMEM((tb, slots), I32),
            pltpu.VMEM((tb, slots), F32),
        ],
        compiler_params=pltpu.CompilerParams(
            dimension_semantics=("parallel",),
            vmem_limit_bytes=_vmem_limit(_nbytes((tb, ne), BF16),
                                         scratch_bytes=_nbytes((tb * W_ROW_PITCH, PEER_NKEYS), F32)),
        ),
        name="peer_gate_matrix",
    )(idx_t, g_t)


def _peer_expert_kernel(x_ref, u_ref, v_ref, w_ref, o_ref):
    @pl.when(pl.program_id(1) == 0)
    def _():
        o_ref[...] = jnp.zeros_like(o_ref)

    a = lax.dot_general(x_ref[...], u_ref[...], (((1,), (1,)), ((), ())), preferred_element_type=F32)
    gelu = 0.5 * a * (1.0 + lax.erf(a * (0.5 ** 0.5)))
    g = (w_ref[...].astype(F32) * gelu).astype(BF16)
    o_ref[...] += jnp.dot(g, v_ref[...], preferred_element_type=F32)


def _peer_experts(x1b, u, v, w, *, tm, ce):
    n, dm = x1b.shape
    ne = v.shape[0]
    table_chunk = pl.BlockSpec((ce, dm), lambda i, j: (j, 0))
    return pl.pallas_call(
        _peer_expert_kernel,
        out_shape=jax.ShapeDtypeStruct((n, dm), F32),
        grid=(n // tm, ne // ce),
        in_specs=[
            pl.BlockSpec((tm, dm), lambda i, j: (i, 0)),
            table_chunk,
            table_chunk,
            pl.BlockSpec((tm, ce), lambda i, j: (i, j)),
        ],
        out_specs=pl.BlockSpec((tm, dm), lambda i, j: (i, 0)),
        compiler_params=pltpu.CompilerParams(
            dimension_semantics=("parallel", "arbitrary"),
            vmem_limit_bytes=_vmem_limit(_nbytes((tm, dm), BF16), 2 * _nbytes((dm, ce), BF16),
                                         _nbytes((tm, ce), BF16), _nbytes((tm, dm), F32)),
        ),
        name="peer_experts",
    )(x1b, u, v, w)


def _residual_ln_kernel(x_ref, y_ref, p_ref, g_ref, b_ref, o32_ref, o16_ref, *, alpha):
    out = _layer_norm_rows(alpha * x_ref[...] + y_ref[...] + p_ref[...], g_ref[...], b_ref[...])
    o32_ref[...] = out
    o16_ref[...] = out.astype(BF16)


def _residual_ln(x1, yf, ple, g, b, *, alpha, tm):
    n, dm = x1.shape
    row = pl.BlockSpec((tm, dm), lambda i: (i, 0))
    vec = pl.BlockSpec((1, dm), lambda i: (0, 0))
    return pl.pallas_call(
        functools.partial(_residual_ln_kernel, alpha=alpha),
        out_shape=(jax.ShapeDtypeStruct((n, dm), F32), jax.ShapeDtypeStruct((n, dm), BF16)),
        grid=(n // tm,),
        in_specs=[row, row, row, vec, vec],
        out_specs=(row, row),
        compiler_params=pltpu.CompilerParams(
            dimension_semantics=("parallel",),
            vmem_limit_bytes=_vmem_limit(4 * _nbytes((tm, dm), F32), _nbytes((tm, dm), BF16)),
        ),
        name="residual_layernorm",
    )(x1, yf, ple, g, b)


def _peer_keys_layout(keys):
    z = jnp.zeros_like(keys[:, 0])
    top = jnp.concatenate([keys[:, 0], z], axis=-1)
    bot = jnp.concatenate([z, keys[:, 1]], axis=-1)
    return jnp.concatenate([top, bot], axis=1).astype(BF16)


def kernel(x, p, w_in, b_in, diff_lambda, diff_subln, rel_bias, conv_w, conv_b, lru_wa, lru_ba, lru_wx,
           lru_bx, lru_lambda, w_branch, w_o, ln1_g, ln1_b, peer_wq, peer_keys, peer_u, peer_v, w_ple,
           w_ple_gate, b_ple_gate, ln2_g, ln2_b):
    bsz, seq, dm = x.shape
    depth = w_in.shape[0]
    n = bsz * seq
    alpha = (2 * depth) ** 0.25
    bwid = A_HEADS * A_VDIM
    t_attn = 512
    assert seq % t_attn == 0 and t_attn >= REL_MAX_DIST

    o_ka, o_va, o_lru, o_c, o_kc, o_vc, o_f = (j * bwid for j in (1, 2, 3, 5, 6, 7, 8))
    o_g = o_f + C_HEADS
    qk_scale = jnp.concatenate([
        jnp.full((bwid,), A_HALF ** -0.5 * LOG2E, F32), jnp.ones((bwid,), F32),
        jnp.full((bwid,), C_HDIM ** -0.5 * LOG2E, F32), jnp.ones((bwid,), F32)])[None]
    ones_row = jnp.ones((1, max(2 * bwid, N_BRANCH * dm)), F32)

    bias = _bias_tiles(rel_bias, t_attn)
    x2 = x.reshape(n, dm)
    xb = x2.astype(BF16)
    for i in range(depth):
        w = w_in[i]
        b = b_in[i]
        w_qk = jnp.concatenate([w[:, :o_va], w[:, o_c:o_vc]], axis=1).astype(BF16)
        b_qk = jnp.concatenate([b[:o_va], b[o_c:o_vc]])[None]
        qk = _proj(xb, w_qk, b_qk, qk_scale, out_dtype=BF16, tm=1024, tn=2048, name="proj_qk")
        w_v = jnp.concatenate([w[:, o_va:o_lru], w[:, o_vc:o_f]], axis=1).T.astype(BF16)
        b_v = jnp.concatenate([b[o_va:o_lru], b[o_vc:o_f]])[:, None]
        vt = _proj_t(xb, w_v, b_v, tm=t_attn, tn=1024, name="proj_v_t")
        bxg = _proj(xb, w[:, o_lru:o_c].astype(BF16), b[None, o_lru:o_c], ones_row[:, :2 * bwid],
                    out_dtype=F32, tm=1024, tn=1024, name="proj_lru")
        gates = _proj(xb, w[:, o_g:].astype(BF16), b[None, o_g:], ones_row[:, :N_BRANCH * dm],
                      out_dtype=BF16, tm=1024, tn=2048, act="sigmoid", name="proj_gates")
        ckb = _fgate_cumsum(xb.reshape(bsz, seq, dm), w[:, o_f:o_g].T.astype(BF16), b[o_f:o_g, None],
                            ts=min(seq, 1024))

        qk3 = qk.reshape(bsz, seq, 4 * bwid)
        vt4 = vt.reshape(bsz, seq // t_attn, 2 * bwid, t_attn)
        lam_init = 0.8 - 0.6 * math.exp(-0.3 * i)
        ya = _diff_attention(qk3, vt4, diff_lambda[i], diff_subln[i][:, None], bias, t=t_attn, lam_init=lam_init)
        yc = _fox_attention(qk3, vt4, ckb, t=t_attn)

        bw = bwid // LRU_BLOCKS
        wax = jnp.concatenate([lru_wa[i], lru_wx[i]], axis=-1).astype(BF16)
        bax = jnp.concatenate([lru_ba[i].reshape(LRU_BLOCKS, 1, bw), lru_bx[i].reshape(LRU_BLOCKS, 1, bw)], axis=-1)
        yb = _lru_branch(bxg, conv_w[i], conv_b[i][None], wax, bax, lru_lambda[i][None], bsz=bsz, t=256)

        merged = _merge(ya.reshape(n, bwid), yb, yc.reshape(n, bwid), w_branch[i].astype(BF16), gates,
                        tm=512, tn=1024)
        x1, x1b = _wo_ln(merged, w_o[i].astype(BF16), x2, ln1_g[i][None], ln1_b[i][None], alpha=alpha, tm=256)

        ple = _ple(x1b, w_ple_gate[i].astype(BF16), b_ple_gate[i][None], p[i].reshape(n, -1).astype(BF16),
                   w_ple[i].astype(BF16), tm=1024, tn=1024)
        idx_t, g_t = _peer_route(x1b, peer_wq[i].astype(BF16), _peer_keys_layout(peer_keys[i]), tm=1024)
        wdense = _peer_gate_matrix(idx_t, g_t, tb=128)
        yf = _peer_experts(x1b, peer_u[i].astype(BF16), peer_v[i].astype(BF16), wdense, tm=1024, ce=1024)
        x2, xb = _residual_ln(x1, yf, ple, ln2_g[i][None], ln2_b[i][None], alpha=alpha, tm=512)
    return x2.reshape(bsz, seq, dm)
```

```python
import functools
import math

import jax
import jax.numpy as jnp
import numpy as np
from jax import lax
from jax.experimental import pallas as pl
from jax.experimental.pallas import tpu as pltpu

F32, BF16, I32 = jnp.float32, jnp.bfloat16, jnp.int32

V7X_VMEM_BYTES = 64 * 2**20
V7X_LANES = 128
V7X_SUBLANES = 8
VMEM_HEADROOM_BYTES = 8 * 2**20

A_HEADS = 8
A_HALF = 64
A_VDIM = 2 * A_HALF
LRU_BLOCKS = 8
CONV_WIDTH = 4
LRU_C = 8.0
C_HEADS = 8
C_HDIM = 128
N_BRANCH = 3
REL_BUCKETS = 32
REL_MAX_DIST = 128
PEER_HEADS = 8
PEER_NKEYS = 128
PEER_HALF = 64
PEER_TOPK = 16
LN_EPS = 1e-5

LOG2E = 1.4426950408889634
MASKED = -1e30

T_ATTN = 512
TM_PROJ, TN_PROJ = 1024, 2048
TN_PROJ_F32 = 1024
TS_FGATE = 1024
T_SCAN = 256
TM_MERGE, TN_MERGE = 512, 1024
TM_LN = 512
TM_PLE, TN_PLE = 1024, 1024
TM_ROUTE = 1024
TB_GATE = 256
TM_EXPERT, CE_EXPERT = 1024, 1024


def _vmem_limit(*block_bytes, scratch_bytes=0, single_buffered_bytes=0):
    need = 2 * sum(block_bytes) + single_buffered_bytes + scratch_bytes + VMEM_HEADROOM_BYTES
    return int(min(max(need, 32 * 2**20), V7X_VMEM_BYTES - 4 * 2**20))


def _nbytes(shape, dtype):
    return int(np.prod(shape)) * jnp.dtype(dtype).itemsize


def _log_sigmoid(z):
    return jnp.minimum(z, 0.0) - jnp.log1p(jnp.exp(-jnp.abs(z)))


def _layer_norm_rows(y, g, b):
    mu = jnp.mean(y, axis=1, keepdims=True)
    yc = y - mu
    var = jnp.mean(yc * yc, axis=1, keepdims=True)
    return yc * lax.rsqrt(var + LN_EPS) * g + b


def _proj_kernel(x_ref, w_ref, b_ref, *rest, act, scaled):
    o_ref = rest[-1]
    y = jnp.dot(x_ref[...], w_ref[...], preferred_element_type=F32) + b_ref[...]
    if scaled:
        y = y * rest[0][...]
    if act == "sigmoid":
        y = jax.nn.sigmoid(y)
    o_ref[...] = y.astype(o_ref.dtype)


def _proj(x, w, b, s=None, *, out_dtype, tm, tn, act=None, name):
    m, k = x.shape
    n = w.shape[1]
    grid = (n // tn, m // tm)
    row = pl.BlockSpec((1, tn), lambda j, i: (0, j))
    operands = (x, w, b) if s is None else (x, w, b, s)
    return pl.pallas_call(
        functools.partial(_proj_kernel, act=act, scaled=s is not None),
        out_shape=jax.ShapeDtypeStruct((m, n), out_dtype),
        grid=grid,
        in_specs=[
            pl.BlockSpec((tm, k), lambda j, i: (i, 0)),
            pl.BlockSpec((k, tn), lambda j, i: (0, j)),
        ] + [row] * (len(operands) - 2),
        out_specs=pl.BlockSpec((tm, tn), lambda j, i: (i, j)),
        compiler_params=pltpu.CompilerParams(
            dimension_semantics=("parallel", "parallel"),
            vmem_limit_bytes=_vmem_limit(_nbytes((tm, k), x.dtype), _nbytes((k, tn), w.dtype),
                                         _nbytes((tm, tn), out_dtype)),
        ),
        name=name,
    )(*operands)


def _proj_t_kernel(w_ref, x_ref, b_ref, o_ref):
    y = lax.dot_general(w_ref[...], x_ref[...], (((1,), (1,)), ((), ())), preferred_element_type=F32)
    o_ref[...] = (y + b_ref[...]).astype(o_ref.dtype)


def _proj_t(x, w_t, b_col, *, tm, tn, name):
    m, k = x.shape
    n = w_t.shape[0]
    return pl.pallas_call(
        _proj_t_kernel,
        out_shape=jax.ShapeDtypeStruct((m // tm, n, tm), BF16),
        grid=(n // tn, m // tm),
        in_specs=[
            pl.BlockSpec((tn, k), lambda j, i: (j, 0)),
            pl.BlockSpec((tm, k), lambda j, i: (i, 0)),
            pl.BlockSpec((tn, 1), lambda j, i: (j, 0)),
        ],
        out_specs=pl.BlockSpec((None, tn, tm), lambda j, i: (i, j, 0)),
        compiler_params=pltpu.CompilerParams(
            dimension_semantics=("parallel", "parallel"),
            vmem_limit_bytes=_vmem_limit(_nbytes((tm, k), BF16), _nbytes((tn, k), BF16), _nbytes((tn, tm), BF16)),
        ),
        name=name,
    )(w_t, x, b_col)


def _fgate_kernel(x_ref, w_ref, b_ref, o_ref, carry_ref, *, ts):
    @pl.when(pl.program_id(1) == 0)
    def _():
        carry_ref[...] = jnp.zeros_like(carry_ref)

    z = lax.dot_general(w_ref[...], x_ref[...], (((1,), (1,)), ((), ())),
                        preferred_element_type=F32) + b_ref[...]
    c = _log_sigmoid(z)
    lane = lax.broadcasted_iota(I32, c.shape, 1)
    d = 1
    while d < ts:
        c = c + jnp.where(lane >= d, pltpu.roll(c, d, 1), 0.0)
        d *= 2
    c = c + carry_ref[:, 0:1]
    carry_ref[...] = jnp.broadcast_to(c[:, ts - 1:ts], carry_ref.shape)
    c = c * LOG2E
    for h in range(C_HEADS):
        o_ref[h] = jnp.broadcast_to(c[h:h + 1, :], (V7X_LANES, ts)).T


def _fgate_cumsum(x3, wf_t, bf, *, ts):
    bsz, seq, dm = x3.shape
    return pl.pallas_call(
        functools.partial(_fgate_kernel, ts=ts),
        out_shape=jax.ShapeDtypeStruct((bsz, C_HEADS, seq, V7X_LANES), F32),
        grid=(bsz, seq // ts),
        in_specs=[
            pl.BlockSpec((None, ts, dm), lambda b, i: (b, i, 0)),
            pl.BlockSpec((C_HEADS, dm), lambda b, i: (0, 0)),
            pl.BlockSpec((C_HEADS, 1), lambda b, i: (0, 0)),
        ],
        out_specs=pl.BlockSpec((None, C_HEADS, ts, V7X_LANES), lambda b, i: (b, 0, i, 0)),
        scratch_shapes=[pltpu.VMEM((C_HEADS, V7X_LANES), F32)],
        compiler_params=pltpu.CompilerParams(dimension_semantics=("parallel", "arbitrary")),
        name="fgate_cumsum",
    )(x3, wf_t, bf)


def _rel_bucket_tiles(t):
    q = np.arange(t)[None, :]
    k = np.arange(t)[:, None]
    out = []
    for off in (0, t):
        rel = q - k + off
        n = np.maximum(rel, 0)
        max_exact = REL_BUCKETS // 2
        nf = np.maximum(n, 1).astype(np.float32)
        large = max_exact + (np.log(nf / np.float32(max_exact)) / np.float32(math.log(REL_MAX_DIST / max_exact))
                             * np.float32(REL_BUCKETS - max_exact)).astype(np.int32)
        large = np.minimum(large, REL_BUCKETS - 1)
        bkt = np.where(n < max_exact, n, large)
        out.append(np.where(rel >= 0, bkt, -1))
    return np.stack(out).astype(np.int32)


def _bias_kernel(rel_ref, bkt_ref, o_ref):
    h = pl.program_id(0)
    bkt = bkt_ref[...]
    far = rel_ref[REL_BUCKETS - 1, h]
    acc = jnp.zeros(bkt.shape, F32)
    for b in range(REL_BUCKETS):
        acc = jnp.where(bkt == b, rel_ref[b, h] - far, acc)
    o_ref[...] = jnp.where(bkt < 0, MASKED, acc * LOG2E)


def _bias_tiles(rel_bias, t):
    bkt = jnp.asarray(_rel_bucket_tiles(t))
    return pl.pallas_call(
        _bias_kernel,
        out_shape=jax.ShapeDtypeStruct((A_HEADS, 2, t, t), F32),
        grid=(A_HEADS,),
        in_specs=[
            pl.BlockSpec(memory_space=pltpu.SMEM),
            pl.BlockSpec((2, t, t), lambda h: (0, 0, 0)),
        ],
        out_specs=pl.BlockSpec((None, 2, t, t), lambda h: (h, 0, 0, 0)),
        name="rel_bias_tiles",
    )(rel_bias, bkt)


HEADS_PER_STEP = 4
PER_GROUP_WINDOW = pl.Buffered(1)


def _sublane_allreduce(x, op):
    for shift in (1, 2, 4):
        x = op(x, pltpu.roll(x, shift, 0))
    return x


def _rows3(x):
    return x.reshape(x.shape[0] // V7X_SUBLANES, V7X_SUBLANES, x.shape[1])


def _attn_scratch(hp, dv, t, c):
    stat = pltpu.VMEM((hp, V7X_SUBLANES, c), F32)
    return [stat, stat, stat, pltpu.VMEM((hp, dv, c), F32), pltpu.VMEM((hp, t, c), BF16)]


def _attn_init(m_sc, l_sc, a_sc, acc_sc, p_sc):
    m_sc[...] = jnp.full_like(m_sc, MASKED)
    l_sc[...] = jnp.zeros_like(l_sc)
    a_sc[...] = jnp.ones_like(a_sc)
    acc_sc[...] = jnp.zeros_like(acc_sc)
    p_sc[...] = jnp.zeros_like(p_sc)


def _attn_fold(vt_prev, h, a_sc, acc_sc, p_sc):
    pv = jnp.dot(vt_prev, p_sc[h], preferred_element_type=F32)
    acc_sc[h] = (_rows3(acc_sc[h]) * a_sc[h][None]).reshape(pv.shape) + pv


def _attn_stage(s, vt_prev, h, m_sc, l_sc, a_sc, acc_sc, p_sc):
    _attn_fold(vt_prev, h, a_sc, acc_sc, p_sc)
    s3 = _rows3(s)
    m_prev = m_sc[h]
    m_new = jnp.maximum(m_prev, _sublane_allreduce(jnp.max(s3, axis=0), jnp.maximum))
    alpha = jnp.exp2(m_prev - m_new)
    p3 = jnp.exp2(s3 - m_new[None])
    l_sc[h] = alpha * l_sc[h] + jnp.sum(p3, axis=0)
    p_sc[h] = p3.reshape(s.shape).astype(BF16)
    a_sc[h] = alpha
    m_sc[h] = m_new


def _attn_stage_fixed(s, vt_prev, h, m_sc, l_sc, a_sc, acc_sc, p_sc):
    _attn_fold(vt_prev, h, a_sc, acc_sc, p_sc)
    p3 = jnp.exp2(_rows3(s) - m_sc[h][None])
    l_sc[h] = l_sc[h] + jnp.sum(p3, axis=0)
    p_sc[h] = p3.reshape(s.shape).astype(BF16)
    a_sc[h] = jnp.ones_like(a_sc[h])


OVERFLOW_GUARD_LOG2 = 100


def _attn_run_guarded(run, l_sc):
    run(_attn_stage_fixed)

    @pl.when(jnp.max(l_sc[...]) > 2.0 ** OVERFLOW_GUARD_LOG2)
    def _():
        run(_attn_stage)


def _diff_attn_kernel(lam_ref, subln_ref, q_ref, k_ref, vt_ref, bias_ref, o_ref,
                      m_sc, l_sc, a_sc, acc_sc, p_sc, *, t, lam_init):
    qi = pl.program_id(2)
    state = (m_sc, l_sc, a_sc, acc_sc, p_sc)

    qqs = []
    for h in range(HEADS_PER_STEP):
        q = q_ref[:, h * A_VDIM:(h + 1) * A_VDIM]
        lane = lax.broadcasted_iota(I32, q.shape, 1)
        zero = jnp.zeros_like(q)
        qqs.append(jnp.concatenate([jnp.where(lane < A_HALF, q, zero), jnp.where(lane >= A_HALF, q, zero)], axis=0))

    lv = lam_ref[...]
    lam = (jnp.exp(jnp.sum(lv[0:1] * lv[1:2], axis=1, keepdims=True))
           - jnp.exp(jnp.sum(lv[2:3] * lv[3:4], axis=1, keepdims=True)) + lam_init)

    def tile(kj, near, stage):
        start = pl.multiple_of(kj * t, t)
        prev = jnp.minimum(kj + 1, qi)
        for h in range(HEADS_PER_STEP):
            cols = slice(h * A_VDIM, (h + 1) * A_VDIM)
            s = lax.dot_general(k_ref[pl.ds(start, t), cols], qqs[h], (((1,), (1,)), ((), ())),
                                preferred_element_type=F32)
            if near is not None:
                bias = bias_ref[h, near]
                s = s + jnp.concatenate([bias, bias], axis=1)
            stage(s, vt_ref[prev, cols, :], h, *state)

    def run(later_stage):
        _attn_init(*state)
        tile(qi, 0, _attn_stage)

        @pl.when(qi >= 1)
        def _():
            tile(qi - 1, 1, later_stage)

        def far_body(i, carry):
            tile(qi - 2 - i, None, later_stage)
            return carry

        lax.fori_loop(0, jnp.maximum(qi - 1, 0), far_body, 0)
        for h in range(HEADS_PER_STEP):
            _attn_fold(vt_ref[0, h * A_VDIM:(h + 1) * A_VDIM, :], h, a_sc, acc_sc, p_sc)
            l = _sublane_allreduce(l_sc[h], jnp.add)
            r = (_rows3(acc_sc[h]) / l[None]).reshape(A_VDIM, 2 * t)
            o = r[:, :t] - lam * r[:, t:]
            ms = _sublane_allreduce(jnp.sum(_rows3(o * o), axis=0), jnp.add) * (1.0 / A_VDIM)
            y = (_rows3(o) * lax.rsqrt(ms + LN_EPS)[None]).reshape(A_VDIM, t) * (subln_ref[...] * (1.0 - lam_init))
            o_ref[:, h * A_VDIM:(h + 1) * A_VDIM] = y.T.astype(o_ref.dtype)

    _attn_run_guarded(run, l_sc)


def _diff_attention(qk3, vt4, lam4, subln_col, bias, *, t, lam_init):
    bsz, seq, _ = qk3.shape
    nq = seq // t
    hp = HEADS_PER_STEP
    wid = hp * A_VDIM
    kblk = A_HEADS // hp
    return pl.pallas_call(
        functools.partial(_diff_attn_kernel, t=t, lam_init=lam_init),
        out_shape=jax.ShapeDtypeStruct((bsz, seq, A_HEADS * A_VDIM), BF16),
        grid=(bsz, A_HEADS // hp, nq),
        in_specs=[
            pl.BlockSpec((4, A_HALF), lambda b, h, i: (0, 0)),
            pl.BlockSpec((A_VDIM, 1), lambda b, h, i: (0, 0)),
            pl.BlockSpec((None, t, wid), lambda b, h, i: (b, i, h)),
            pl.BlockSpec((None, seq, wid), lambda b, h, i: (b, 0, kblk + h), pipeline_mode=PER_GROUP_WINDOW),
            pl.BlockSpec((None, nq, wid, t), lambda b, h, i: (b, 0, h, 0), pipeline_mode=PER_GROUP_WINDOW),
            pl.BlockSpec((hp, 2, t, t), lambda b, h, i: (h, 0, 0, 0), pipeline_mode=PER_GROUP_WINDOW),
        ],
        out_specs=pl.BlockSpec((None, t, wid), lambda b, h, i: (b, i, h)),
        scratch_shapes=_attn_scratch(hp, A_VDIM, t, 2 * t),
        compiler_params=pltpu.CompilerParams(
            dimension_semantics=("parallel", "parallel", "arbitrary"),
            vmem_limit_bytes=_vmem_limit(
                2 * _nbytes((t, wid), BF16),
                single_buffered_bytes=2 * _nbytes((seq, wid), BF16) + _nbytes((hp, 2, t, t), F32),
                scratch_bytes=_nbytes((hp, A_VDIM + t // 2 + t, 2 * t), F32)),
        ),
        name="diff_attention",
    )(lam4, subln_col, qk3, qk3, vt4, bias)


def _fox_attn_kernel(q_ref, k_ref, vt_ref, ck_ref, o_ref, m_sc, l_sc, a_sc, acc_sc, p_sc, *, t):
    qi = pl.program_id(2)
    state = (m_sc, l_sc, a_sc, acc_sc, p_sc)
    qs = [q_ref[:, h * C_HDIM:(h + 1) * C_HDIM] for h in range(HEADS_PER_STEP)]

    def tile(kj, diagonal, stage):
        start = pl.multiple_of(kj * t, t)
        prev = jnp.minimum(kj + 1, qi)
        for h in range(HEADS_PER_STEP):
            cols = slice(h * C_HDIM, (h + 1) * C_HDIM)
            s = lax.dot_general(k_ref[pl.ds(start, t), cols], qs[h], (((1,), (1,)), ((), ())),
                                preferred_element_type=F32)
            ck = ck_ref[h, pl.ds(start, t), :]
            s = s - jnp.concatenate([ck] * (t // V7X_LANES), axis=1)
            if diagonal:
                key = lax.broadcasted_iota(I32, s.shape, 0)
                qry = lax.broadcasted_iota(I32, s.shape, 1)
                s = jnp.where(key <= qry, s, MASKED)
            stage(s, vt_ref[prev, cols, :], h, *state)

    def run(later_stage):
        _attn_init(*state)
        tile(qi, True, _attn_stage)

        def body(i, carry):
            tile(qi - 1 - i, False, later_stage)
            return carry

        lax.fori_loop(0, qi, body, 0)
        for h in range(HEADS_PER_STEP):
            _attn_fold(vt_ref[0, h * C_HDIM:(h + 1) * C_HDIM, :], h, a_sc, acc_sc, p_sc)
            l = _sublane_allreduce(l_sc[h], jnp.add)
            o = (_rows3(acc_sc[h]) / l[None]).reshape(C_HDIM, t)
            o_ref[:, h * C_HDIM:(h + 1) * C_HDIM] = o.T.astype(o_ref.dtype)

    _attn_run_guarded(run, l_sc)


def _fox_attention(qk3, vt4, ckb, *, t):
    bsz, seq, _ = qk3.shape
    nq = seq // t
    hp = HEADS_PER_STEP
    wid = hp * C_HDIM
    qblk = 2 * (A_HEADS // hp)
    kblk = qblk + C_HEADS // hp
    vblk = A_HEADS // hp
    return pl.pallas_call(
        functools.partial(_fox_attn_kernel, t=t),
        out_shape=jax.ShapeDtypeStruct((bsz, seq, C_HEADS * C_HDIM), BF16),
        grid=(bsz, C_HEADS // hp, nq),
        in_specs=[
            pl.BlockSpec((None, t, wid), lambda b, h, i: (b, i, qblk + h)),
            pl.BlockSpec((None, seq, wid), lambda b, h, i: (b, 0, kblk + h), pipeline_mode=PER_GROUP_WINDOW),
            pl.BlockSpec((None, nq, wid, t), lambda b, h, i: (b, 0, vblk + h, 0), pipeline_mode=PER_GROUP_WINDOW),
            pl.BlockSpec((None, hp, seq, V7X_LANES), lambda b, h, i: (b, h, 0, 0), pipeline_mode=PER_GROUP_WINDOW),
        ],
        out_specs=pl.BlockSpec((None, t, wid), lambda b, h, i: (b, i, h)),
        scratch_shapes=_attn_scratch(hp, C_HDIM, t, t),
        compiler_params=pltpu.CompilerParams(
            dimension_semantics=("parallel", "parallel", "arbitrary"),
            vmem_limit_bytes=_vmem_limit(
                2 * _nbytes((t, wid), BF16),
                single_buffered_bytes=2 * _nbytes((seq, wid), BF16) + _nbytes((hp, seq, V7X_LANES), F32),
                scratch_bytes=_nbytes((hp, C_HDIM + t // 2 + t, t), F32)),
        ),
        name="fox_attention",
    )(qk3, qk3, vt4, ckb)


def _lru_kernel(bx_ref, bg_ref, cw_ref, cb_ref, wax_ref, bax_ref, lam_ref, o_ref, xbuf, hprev, *, t):
    pad = V7X_SUBLANES

    @pl.when(pl.program_id(1) == 0)
    def _():
        xbuf[0:pad, :] = jnp.zeros((pad, xbuf.shape[1]), F32)
        hprev[...] = jnp.zeros_like(hprev)

    xbuf[pad:pad + t, :] = bx_ref[...]
    xc = cb_ref[...]
    for tap in range(CONV_WIDTH):
        xc = xc + xbuf[pl.ds(pad - (CONV_WIDTH - 1) + tap, t), :] * cw_ref[tap:tap + 1, :]
    xbuf[0:pad, :] = bx_ref[t - pad:t, :]

    bw = xc.shape[1] // LRU_BLOCKS
    row = lax.broadcasted_iota(I32, (t, bw), 0)
    for g in range(LRU_BLOCKS):
        cols = slice(g * bw, (g + 1) * bw)
        xg = xc[:, cols]
        z = jnp.dot(xg.astype(BF16), wax_ref[g], preferred_element_type=F32) + bax_ref[g]
        r = jax.nn.sigmoid(z[:, :bw])
        gi = jax.nn.sigmoid(z[:, bw:])
        log_a = (LRU_C * r) * _log_sigmoid(lam_ref[:, cols])
        a = jnp.exp(log_a)
        u = jnp.sqrt(1.0 - jnp.exp(2.0 * log_a)) * (gi * xg)
        d = 1
        while d < t:
            keep = row >= d
            a_sh = jnp.where(keep, pltpu.roll(a, d, 0), 1.0)
            u_sh = jnp.where(keep, pltpu.roll(u, d, 0), 0.0)
            u = a * u_sh + u
            a = a * a_sh
            d *= 2
        h = a * hprev[0:1, cols] + u
        hprev[0:1, cols] = h[t - 1:t, :]
        o_ref[:, cols] = (jax.nn.gelu(bg_ref[:, cols], approximate=True) * h).astype(o_ref.dtype)


def _lru_branch(bxg, conv_w, conv_b, wax, bax, lam, *, bsz, t):
    n, two_w = bxg.shape
    w = two_w // 2
    nt = n // bsz // t
    bw = w // LRU_BLOCKS
    return pl.pallas_call(
        functools.partial(_lru_kernel, t=t),
        out_shape=jax.ShapeDtypeStruct((n, w), BF16),
        grid=(bsz, nt),
        in_specs=[
            pl.BlockSpec((t, w), lambda b, i: (b * nt + i, 0)),
            pl.BlockSpec((t, w), lambda b, i: (b * nt + i, 1)),
            pl.BlockSpec((CONV_WIDTH, w), lambda b, i: (0, 0)),
            pl.BlockSpec((1, w), lambda b, i: (0, 0)),
            pl.BlockSpec((LRU_BLOCKS, bw, 2 * bw), lambda b, i: (0, 0, 0)),
            pl.BlockSpec((LRU_BLOCKS, 1, 2 * bw), lambda b, i: (0, 0, 0)),
            pl.BlockSpec((1, w), lambda b, i: (0, 0)),
        ],
        out_specs=pl.BlockSpec((t, w), lambda b, i: (b * nt + i, 0)),
        scratch_shapes=[
            pltpu.VMEM((t + V7X_SUBLANES, w), F32),
            pltpu.VMEM((V7X_SUBLANES, w), F32),
        ],
        compiler_params=pltpu.CompilerParams(dimension_semantics=("parallel", "arbitrary")),
        name="conv_rglru",
    )(bxg, bxg, conv_w, conv_b, wax, bax, lam)


def _merge_kernel(ya_ref, yb_ref, yc_ref, w_ref, g0_ref, g1_ref, g2_ref, o_ref):
    acc = g0_ref[...].astype(F32) * jnp.dot(ya_ref[...], w_ref[0], preferred_element_type=F32)
    acc = acc + g1_ref[...].astype(F32) * jnp.dot(yb_ref[...], w_ref[1], preferred_element_type=F32)
    acc = acc + g2_ref[...].astype(F32) * jnp.dot(yc_ref[...], w_ref[2], preferred_element_type=F32)
    o_ref[...] = acc.astype(o_ref.dtype)


def _merge(ya, yb, yc, wb, gates, *, tm, tn):
    n, bwid = ya.shape
    dm = wb.shape[2]
    nc = dm // tn
    y_spec = pl.BlockSpec((tm, bwid), lambda j, i: (i, 0))
    return pl.pallas_call(
        _merge_kernel,
        out_shape=jax.ShapeDtypeStruct((n, dm), BF16),
        grid=(nc, n // tm),
        in_specs=[
            y_spec, y_spec, y_spec,
            pl.BlockSpec((N_BRANCH, bwid, tn), lambda j, i: (0, 0, j)),
            pl.BlockSpec((tm, tn), lambda j, i: (i, j)),
            pl.BlockSpec((tm, tn), lambda j, i: (i, nc + j)),
            pl.BlockSpec((tm, tn), lambda j, i: (i, 2 * nc + j)),
        ],
        out_specs=pl.BlockSpec((tm, tn), lambda j, i: (i, j)),
        compiler_params=pltpu.CompilerParams(
            dimension_semantics=("parallel", "parallel"),
            vmem_limit_bytes=_vmem_limit(3 * _nbytes((tm, bwid), BF16), _nbytes((N_BRANCH, bwid, tn), BF16),
                                         4 * _nbytes((tm, tn), BF16)),
        ),
        name="gated_merge",
    )(ya, yb, yc, wb, gates, gates, gates)


def _wo_ln_kernel(m_ref, w_ref, x_ref, g_ref, b_ref, o32_ref, o16_ref, *, alpha):
    y = jnp.dot(m_ref[...], w_ref[...], preferred_element_type=F32) + alpha * x_ref[...]
    out = _layer_norm_rows(y, g_ref[...], b_ref[...])
    o32_ref[...] = out
    o16_ref[...] = out.astype(BF16)


def _wo_ln(merged, wo, x, g, b, *, alpha, tm):
    n, dm = x.shape
    row = pl.BlockSpec((tm, dm), lambda i: (i, 0))
    vec = pl.BlockSpec((1, dm), lambda i: (0, 0))
    return pl.pallas_call(
        functools.partial(_wo_ln_kernel, alpha=alpha),
        out_shape=(jax.ShapeDtypeStruct((n, dm), F32), jax.ShapeDtypeStruct((n, dm), BF16)),
        grid=(n // tm,),
        in_specs=[row, pl.BlockSpec((dm, dm), lambda i: (0, 0), pipeline_mode=pl.Buffered(1)), row, vec, vec],
        out_specs=(row, row),
        compiler_params=pltpu.CompilerParams(
            dimension_semantics=("parallel",),
            vmem_limit_bytes=_vmem_limit(2 * _nbytes((tm, dm), BF16), 2 * _nbytes((tm, dm), F32),
                                         single_buffered_bytes=_nbytes((dm, dm), BF16),
                                         scratch_bytes=_nbytes((tm, dm), F32)),
        ),
        name="wo_layernorm",
    )(merged, wo, x, g, b)


def _ple_kernel(x_ref, wg_ref, bg_ref, p_ref, we_ref, o_ref):
    gate = jax.nn.sigmoid(jnp.dot(x_ref[...], wg_ref[...], preferred_element_type=F32) + bg_ref[...])
    o_ref[...] = gate * jnp.dot(p_ref[...], we_ref[...], preferred_element_type=F32)


def _ple(x1b, wg, bg, pb, we, *, tm, tn):
    n, dm = x1b.shape
    pdim = pb.shape[1]
    return pl.pallas_call(
        _ple_kernel,
        out_shape=jax.ShapeDtypeStruct((n, dm), F32),
        grid=(dm // tn, n // tm),
        in_specs=[
            pl.BlockSpec((tm, dm), lambda j, i: (i, 0)),
            pl.BlockSpec((dm, tn), lambda j, i: (0, j)),
            pl.BlockSpec((1, tn), lambda j, i: (0, j)),
            pl.BlockSpec((tm, pdim), lambda j, i: (i, 0)),
            pl.BlockSpec((pdim, tn), lambda j, i: (0, j)),
        ],
        out_specs=pl.BlockSpec((tm, tn), lambda j, i: (i, j)),
        compiler_params=pltpu.CompilerParams(
            dimension_semantics=("parallel", "parallel"),
            vmem_limit_bytes=_vmem_limit(_nbytes((tm, dm), BF16), _nbytes((dm, tn), BF16),
                                         _nbytes((tm, tn), F32)),
        ),
        name="gated_ple",
    )(x1b, wg, bg, pb, we)


def _top_rows(s, payload, k):
    rows = s.shape[0]
    rid = lax.broadcasted_iota(I32, s.shape, 0)
    vals, pays = [], []
    for _ in range(k):
        m = jnp.max(s, axis=0, keepdims=True)
        pos = jnp.min(jnp.where(s == m, rid, rows), axis=0, keepdims=True)
        hit = rid == pos
        vals.append(m)
        pays.append(pos if payload is None else jnp.max(jnp.where(hit, payload, -1), axis=0, keepdims=True))
        s = jnp.where(hit, -jnp.inf, s)
    return jnp.concatenate(vals, axis=0), jnp.concatenate(pays, axis=0)


def _peer_route_kernel(x_ref, wq_ref, keys_ref, idx_ref, g_ref):
    q = jnp.dot(x_ref[...], wq_ref[...], preferred_element_type=F32).astype(BF16)
    st = lax.dot_general(keys_ref[...], q, (((1,), (1,)), ((), ())), preferred_element_type=F32)
    s1, i1 = _top_rows(st[:PEER_NKEYS], None, PEER_TOPK)
    s2, i2 = _top_rows(st[PEER_NKEYS:], None, PEER_TOPK)
    cand, cidx = [], []
    for a in range(PEER_TOPK):
        nb = PEER_TOPK // (a + 1)
        cand.append(s1[a:a + 1] + s2[:nb])
        cidx.append(i1[a:a + 1] * PEER_NKEYS + i2[:nb])
    npairs = sum(c.shape[0] for c in cand)
    npad = -npairs % V7X_SUBLANES
    cand.append(jnp.full((npad, st.shape[1]), -jnp.inf, F32))
    cidx.append(jnp.full((npad, st.shape[1]), -1, I32))
    sc, idx = _top_rows(jnp.concatenate(cand, axis=0), jnp.concatenate(cidx, axis=0), PEER_TOPK)
    e = jnp.exp(sc - jnp.max(sc, axis=0, keepdims=True))
    idx_ref[...] = idx
    g_ref[...] = e / jnp.sum(e, axis=0, keepdims=True)


def _peer_route(x1b, wq, keys_t, *, tm):
    n, dm = x1b.shape
    qd = 2 * PEER_HALF
    slots = PEER_HEADS * PEER_TOPK
    out_spec = pl.BlockSpec((PEER_TOPK, tm), lambda i, h: (h, i))
    return pl.pallas_call(
        _peer_route_kernel,
        out_shape=(jax.ShapeDtypeStruct((slots, n), I32), jax.ShapeDtypeStruct((slots, n), F32)),
        grid=(n // tm, PEER_HEADS),
        in_specs=[
            pl.BlockSpec((tm, dm), lambda i, h: (i, 0)),
            pl.BlockSpec((dm, qd), lambda i, h: (0, h)),
            pl.BlockSpec((None, 2 * PEER_NKEYS, qd), lambda i, h: (h, 0, 0)),
        ],
        out_specs=(out_spec, out_spec),
        compiler_params=pltpu.CompilerParams(dimension_semantics=("parallel", "parallel")),
        name="peer_route",
    )(x1b, wq, keys_t)


W_ROW_PITCH = PEER_NKEYS + V7X_SUBLANES


def _peer_gate_kernel(idx_ref, g_ref, o_ref, wbuf, idx_sc, g_sc, *, tb):
    idx_sc[...] = idx_ref[...].T
    g_sc[...] = g_ref[...].T
    rid = lax.broadcasted_iota(I32, (PEER_NKEYS, PEER_NKEYS), 0).astype(F32).astype(BF16)
    one = jnp.ones((PEER_NKEYS, PEER_NKEYS), BF16)
    zero = jnp.zeros((PEER_NKEYS, PEER_NKEYS), BF16)
    group = 2 * V7X_SUBLANES
    packed_rows = 2 * V7X_SUBLANES

    def body(j, carry):
        base = pl.multiple_of(j * group, group)
        ib = idx_sc[pl.ds(base, group), :]
        gb = g_sc[pl.ds(base, group), :]
        i1b = (ib >> 7).astype(F32)
        i2b = (ib & (PEER_NKEYS - 1)).astype(F32)
        for r in range(group):
            def rep(v):
                one_vreg = jnp.broadcast_to(v[r:r + 1, :], (packed_rows, PEER_NKEYS)).astype(BF16)
                return jnp.concatenate([one_vreg] * (PEER_NKEYS // packed_rows), axis=0)
            r1 = jnp.where(rid == rep(i1b), rep(gb), zero)
            r2t = jnp.where(rid == rep(i2b), one, zero)
            wt = lax.dot_general(r1, r2t, (((1,), (1,)), ((), ())), preferred_element_type=F32)
            wbuf[pl.ds(pl.multiple_of((base + r) * W_ROW_PITCH, V7X_SUBLANES), PEER_NKEYS), :] = wt
        return carry

    lax.fori_loop(0, tb // group, body, 0)
    for i1 in range(PEER_NKEYS):
        o_ref[:, i1 * PEER_NKEYS:(i1 + 1) * PEER_NKEYS] = (
            wbuf[pl.ds(i1, tb, stride=W_ROW_PITCH), :].astype(o_ref.dtype))


def _peer_gate_matrix(idx_t, g_t, *, tb):
    slots, n = idx_t.shape
    ne = PEER_NKEYS * PEER_NKEYS
    in_spec = pl.BlockSpec((slots, tb), lambda i: (0, i))
    return pl.pallas_call(
        functools.partial(_peer_gate_kernel, tb=tb),
        out_shape=jax.ShapeDtypeStruct((n, ne), BF16),
        grid=(n // tb,),
        in_specs=[in_spec, in_spec],
        out_specs=pl.BlockSpec((tb, ne), lambda i: (i, 0)),
        scratch_shapes=[
            pltpu.VMEM((tb * W_ROW_PITCH, PEER_NKEYS), F32),
            pltpu.VMEM((tb, slots), I32),
            pltpu.VMEM((tb, slots), F32),
        ],
        compiler_params=pltpu.CompilerParams(
            dimension_semantics=("parallel",),
            vmem_limit_bytes=_vmem_limit(_nbytes((tb, ne), BF16),
                                         scratch_bytes=_nbytes((tb * W_ROW_PITCH, PEER_NKEYS), F32)),
        ),
        name="peer_gate_matrix",
    )(idx_t, g_t)


def _peer_expert_kernel(x_ref, u_ref, v_ref, w_ref, o_ref):
    @pl.when(pl.program_id(1) == 0)
    def _():
        o_ref[...] = jnp.zeros_like(o_ref)

    a = lax.dot_general(x_ref[...], u_ref[...], (((1,), (1,)), ((), ())), preferred_element_type=F32)
    gelu = 0.5 * a * (1.0 + lax.erf(a * (0.5 ** 0.5)))
    g = (w_ref[...].astype(F32) * gelu).astype(BF16)
    o_ref[...] += jnp.dot(g, v_ref[...], preferred_element_type=F32)


def _peer_experts(x1b, u, v, w, *, tm, ce):
    n, dm = x1b.shape
    ne = v.shape[0]
    table_chunk = pl.BlockSpec((ce, dm), lambda i, j: (j, 0))
    return pl.pallas_call(
        _peer_expert_kernel,
        out_shape=jax.ShapeDtypeStruct((n, dm), F32),
        grid=(n // tm, ne // ce),
        in_specs=[
            pl.BlockSpec((tm, dm), lambda i, j: (i, 0)),
            table_chunk,
            table_chunk,
            pl.BlockSpec((tm, ce), lambda i, j: (i, j)),
        ],
        out_specs=pl.BlockSpec((tm, dm), lambda i, j: (i, 0)),
        compiler_params=pltpu.CompilerParams(
            dimension_semantics=("parallel", "arbitrary"),
            vmem_limit_bytes=_vmem_limit(_nbytes((tm, dm), BF16), 2 * _nbytes((dm, ce), BF16),
                                         _nbytes((tm, ce), BF16), _nbytes((tm, dm), F32)),
        ),
        name="peer_experts",
    )(x1b, u, v, w)


def _residual_ln_kernel(x_ref, y_ref, p_ref, g_ref, b_ref, o32_ref, o16_ref, *, alpha):
    out = _layer_norm_rows(alpha * x_ref[...] + y_ref[...] + p_ref[...], g_ref[...], b_ref[...])
    o32_ref[...] = out
    o16_ref[...] = out.astype(BF16)


def _residual_ln(x1, yf, ple, g, b, *, alpha, tm):
    n, dm = x1.shape
    row = pl.BlockSpec((tm, dm), lambda i: (i, 0))
    vec = pl.BlockSpec((1, dm), lambda i: (0, 0))
    return pl.pallas_call(
        functools.partial(_residual_ln_kernel, alpha=alpha),
        out_shape=(jax.ShapeDtypeStruct((n, dm), F32), jax.ShapeDtypeStruct((n, dm), BF16)),
        grid=(n // tm,),
        in_specs=[row, row, row, vec, vec],
        out_specs=(row, row),
        compiler_params=pltpu.CompilerParams(
            dimension_semantics=("parallel",),
            vmem_limit_bytes=_vmem_limit(4 * _nbytes((tm, dm), F32), _nbytes((tm, dm), BF16)),
        ),
        name="residual_layernorm",
    )(x1, yf, ple, g, b)


def _peer_keys_layout(keys):
    z = jnp.zeros_like(keys[:, 0])
    top = jnp.concatenate([keys[:, 0], z], axis=-1)
    bot = jnp.concatenate([z, keys[:, 1]], axis=-1)
    return jnp.concatenate([top, bot], axis=1).astype(BF16)


def kernel(x, p, w_in, b_in, diff_lambda, diff_subln, rel_bias, conv_w, conv_b, lru_wa, lru_ba, lru_wx,
           lru_bx, lru_lambda, w_branch, w_o, ln1_g, ln1_b, peer_wq, peer_keys, peer_u, peer_v, w_ple,
           w_ple_gate, b_ple_gate, ln2_g, ln2_b):
    bsz, seq, dm = x.shape
    depth = w_in.shape[0]
    n = bsz * seq
    alpha = (2 * depth) ** 0.25
    bwid = A_HEADS * A_VDIM
    t_attn = min(T_ATTN, seq)
    assert seq % t_attn == 0 and t_attn >= REL_MAX_DIST

    o_ka, o_va, o_lru, o_c, o_kc, o_vc, o_f = (j * bwid for j in (1, 2, 3, 5, 6, 7, 8))
    o_g = o_f + C_HEADS
    qk_scale = jnp.concatenate([
        jnp.full((bwid,), A_HALF ** -0.5 * LOG2E, F32), jnp.ones((bwid,), F32),
        jnp.full((bwid,), C_HDIM ** -0.5 * LOG2E, F32), jnp.ones((bwid,), F32)])[None]

    bias = _bias_tiles(rel_bias, t_attn)
    x2 = x.reshape(n, dm)
    xb = x2.astype(BF16)
    for i in range(depth):
        w = w_in[i]
        b = b_in[i]
        w_qk = jnp.concatenate([w[:, :o_va], w[:, o_c:o_vc]], axis=1).astype(BF16)
        b_qk = jnp.concatenate([b[:o_va], b[o_c:o_vc]])[None]
        qk = _proj(xb, w_qk, b_qk, qk_scale, out_dtype=BF16, tm=TM_PROJ, tn=TN_PROJ, name="proj_qk")
        w_v = jnp.concatenate([w[:, o_va:o_lru], w[:, o_vc:o_f]], axis=1).T.astype(BF16)
        b_v = jnp.concatenate([b[o_va:o_lru], b[o_vc:o_f]])[:, None]
        vt = _proj_t(xb, w_v, b_v, tm=t_attn, tn=bwid, name="proj_v_t")
        bxg = _proj(xb, w[:, o_lru:o_c].astype(BF16), b[None, o_lru:o_c],
                    out_dtype=F32, tm=TM_PROJ, tn=TN_PROJ_F32, name="proj_lru")
        gates = _proj(xb, w[:, o_g:].astype(BF16), b[None, o_g:],
                      out_dtype=BF16, tm=TM_PROJ, tn=TN_PROJ, act="sigmoid", name="proj_gates")
        ckb = _fgate_cumsum(xb.reshape(bsz, seq, dm), w[:, o_f:o_g].T.astype(BF16), b[o_f:o_g, None],
                            ts=min(seq, TS_FGATE))

        qk3 = qk.reshape(bsz, seq, 4 * bwid)
        vt4 = vt.reshape(bsz, seq // t_attn, 2 * bwid, t_attn)
        lam_init = 0.8 - 0.6 * math.exp(-0.3 * i)
        ya = _diff_attention(qk3, vt4, diff_lambda[i], diff_subln[i][:, None], bias, t=t_attn, lam_init=lam_init)
        yc = _fox_attention(qk3, vt4, ckb, t=t_attn)

        bw = bwid // LRU_BLOCKS
        wax = jnp.concatenate([lru_wa[i], lru_wx[i]], axis=-1).astype(BF16)
        bax = jnp.concatenate([lru_ba[i].reshape(LRU_BLOCKS, 1, bw), lru_bx[i].reshape(LRU_BLOCKS, 1, bw)], axis=-1)
        yb = _lru_branch(bxg, conv_w[i], conv_b[i][None], wax, bax, lru_lambda[i][None], bsz=bsz, t=T_SCAN)

        merged = _merge(ya.reshape(n, bwid), yb, yc.reshape(n, bwid), w_branch[i].astype(BF16), gates,
                        tm=TM_MERGE, tn=TN_MERGE)
        x1, x1b = _wo_ln(merged, w_o[i].astype(BF16), x2, ln1_g[i][None], ln1_b[i][None], alpha=alpha, tm=TM_LN)

        ple = _ple(x1b, w_ple_gate[i].astype(BF16), b_ple_gate[i][None], p[i].reshape(n, -1).astype(BF16),
                   w_ple[i].astype(BF16), tm=TM_PLE, tn=TN_PLE)
        idx_t, g_t = _peer_route(x1b, peer_wq[i].astype(BF16), _peer_keys_layout(peer_keys[i]), tm=TM_ROUTE)
        wdense = _peer_gate_matrix(idx_t, g_t, tb=TB_GATE)
        yf = _peer_experts(x1b, peer_u[i].astype(BF16), peer_v[i].astype(BF16), wdense, tm=TM_EXPERT, ce=CE_EXPERT)
        x2, xb = _residual_ln(x1, yf, ple, ln2_g[i][None], ln2_b[i][None], alpha=alpha, tm=TM_LN)
    return x2.reshape(bsz, seq, dm)
```

```python
import functools
import math

import jax
import jax.numpy as jnp
import numpy as np
from jax import lax
from jax.experimental import pallas as pl
from jax.experimental.pallas import tpu as pltpu

F32, BF16, I32 = jnp.float32, jnp.bfloat16, jnp.int32

V7X_VMEM_BYTES = 64 * 2**20
V7X_LANES = 128
V7X_SUBLANES = 8
VMEM_HEADROOM_BYTES = 8 * 2**20

A_HEADS = 8
A_HALF = 64
A_VDIM = 2 * A_HALF
LRU_BLOCKS = 8
CONV_WIDTH = 4
LRU_C = 8.0
C_HEADS = 8
C_HDIM = 128
N_BRANCH = 3
REL_BUCKETS = 32
REL_MAX_DIST = 128
PEER_HEADS = 8
PEER_NKEYS = 128
PEER_HALF = 64
PEER_TOPK = 16
LN_EPS = 1e-5

LOG2E = 1.4426950408889634
MASKED = -1e30

T_ATTN = 512
TM_PROJ, TN_PROJ = 1024, 2048
TS_FGATE = 1024
T_SCAN = 256
TM_MERGE, TN_MERGE = 512, 1024
TM_LN = 512
TM_PLE, TN_PLE = 1024, 1024
TM_ROUTE = 1024
TB_GATE = 256
TM_EXPERT, CE_EXPERT = 1024, 1024


def _vmem_limit(*block_bytes, scratch_bytes=0, single_buffered_bytes=0):
    need = 2 * sum(block_bytes) + single_buffered_bytes + scratch_bytes + VMEM_HEADROOM_BYTES
    return int(min(max(need, 32 * 2**20), V7X_VMEM_BYTES - 4 * 2**20))


def _nbytes(shape, dtype):
    return int(np.prod(shape)) * jnp.dtype(dtype).itemsize


def _log_sigmoid(z):
    return jnp.minimum(z, 0.0) - jnp.log1p(jnp.exp(-jnp.abs(z)))


def _layer_norm_rows(y, g, b):
    mu = jnp.mean(y, axis=1, keepdims=True)
    yc = y - mu
    var = jnp.mean(yc * yc, axis=1, keepdims=True)
    return yc * lax.rsqrt(var + LN_EPS) * g + b


def _proj_kernel(x_ref, w_ref, b_ref, *rest, act, scaled):
    o_ref = rest[-1]
    y = jnp.dot(x_ref[...], w_ref[...], preferred_element_type=F32) + b_ref[...]
    if scaled:
        y = y * rest[0][...]
    if act == "sigmoid":
        y = jax.nn.sigmoid(y)
    o_ref[...] = y.astype(o_ref.dtype)


def _same_block(j):
    return j


def _proj(x, w, b, s=None, *, out_dtype, tm, tn, w_block=_same_block, act=None, name):
    m, k = x.shape
    n = b.shape[1]
    grid = (n // tn, m // tm)
    row = pl.BlockSpec((1, tn), lambda j, i: (0, j))
    operands = (x, w, b) if s is None else (x, w, b, s)
    return pl.pallas_call(
        functools.partial(_proj_kernel, act=act, scaled=s is not None),
        out_shape=jax.ShapeDtypeStruct((m, n), out_dtype),
        grid=grid,
        in_specs=[
            pl.BlockSpec((tm, k), lambda j, i: (i, 0)),
            pl.BlockSpec((k, tn), lambda j, i: (0, w_block(j))),
        ] + [row] * (len(operands) - 2),
        out_specs=pl.BlockSpec((tm, tn), lambda j, i: (i, j)),
        compiler_params=pltpu.CompilerParams(
            dimension_semantics=("parallel", "parallel"),
            vmem_limit_bytes=_vmem_limit(_nbytes((tm, k), x.dtype), _nbytes((k, tn), w.dtype),
                                         _nbytes((tm, tn), out_dtype)),
        ),
        name=name,
    )(*operands)


def _proj_t_kernel(w_ref, x_ref, b_ref, o_ref):
    y = lax.dot_general(w_ref[...], x_ref[...], (((1,), (1,)), ((), ())), preferred_element_type=F32)
    o_ref[...] = (y + b_ref[...]).astype(o_ref.dtype)


def _proj_t(x, w_t, b_col, *, tm, tn, name):
    m, k = x.shape
    n = w_t.shape[0]
    return pl.pallas_call(
        _proj_t_kernel,
        out_shape=jax.ShapeDtypeStruct((m // tm, n, tm), BF16),
        grid=(n // tn, m // tm),
        in_specs=[
            pl.BlockSpec((tn, k), lambda j, i: (j, 0)),
            pl.BlockSpec((tm, k), lambda j, i: (i, 0)),
            pl.BlockSpec((tn, 1), lambda j, i: (j, 0)),
        ],
        out_specs=pl.BlockSpec((None, tn, tm), lambda j, i: (i, j, 0)),
        compiler_params=pltpu.CompilerParams(
            dimension_semantics=("parallel", "parallel"),
            vmem_limit_bytes=_vmem_limit(_nbytes((tm, k), BF16), _nbytes((tn, k), BF16), _nbytes((tn, tm), BF16)),
        ),
        name=name,
    )(w_t, x, b_col)


def _fgate_kernel(x_ref, w_ref, b_ref, o_ref, carry_ref, *, ts):
    @pl.when(pl.program_id(1) == 0)
    def _():
        carry_ref[...] = jnp.zeros_like(carry_ref)

    z = lax.dot_general(w_ref[...], x_ref[...], (((1,), (1,)), ((), ())),
                        preferred_element_type=F32) + b_ref[...]
    c = _log_sigmoid(z)
    lane = lax.broadcasted_iota(I32, c.shape, 1)
    d = 1
    while d < ts:
        c = c + jnp.where(lane >= d, pltpu.roll(c, d, 1), 0.0)
        d *= 2
    c = c + carry_ref[:, 0:1]
    carry_ref[...] = jnp.broadcast_to(c[:, ts - 1:ts], carry_ref.shape)
    c = c * LOG2E
    for h in range(C_HEADS):
        o_ref[h] = jnp.broadcast_to(c[h:h + 1, :], (V7X_LANES, ts)).T


def _fgate_cumsum(x3, wf_t, bf, *, ts):
    bsz, seq, dm = x3.shape
    return pl.pallas_call(
        functools.partial(_fgate_kernel, ts=ts),
        out_shape=jax.ShapeDtypeStruct((bsz, C_HEADS, seq, V7X_LANES), F32),
        grid=(bsz, seq // ts),
        in_specs=[
            pl.BlockSpec((None, ts, dm), lambda b, i: (b, i, 0)),
            pl.BlockSpec((C_HEADS, dm), lambda b, i: (0, 0)),
            pl.BlockSpec((C_HEADS, 1), lambda b, i: (0, 0)),
        ],
        out_specs=pl.BlockSpec((None, C_HEADS, ts, V7X_LANES), lambda b, i: (b, 0, i, 0)),
        scratch_shapes=[pltpu.VMEM((C_HEADS, V7X_LANES), F32)],
        compiler_params=pltpu.CompilerParams(dimension_semantics=("parallel", "arbitrary")),
        name="fgate_cumsum",
    )(x3, wf_t, bf)


def _rel_bucket_tiles(t):
    q = np.arange(t)[None, :]
    k = np.arange(t)[:, None]
    out = []
    for off in (0, t):
        rel = q - k + off
        n = np.maximum(rel, 0)
        max_exact = REL_BUCKETS // 2
        nf = np.maximum(n, 1).astype(np.float32)
        large = max_exact + (np.log(nf / np.float32(max_exact)) / np.float32(math.log(REL_MAX_DIST / max_exact))
                             * np.float32(REL_BUCKETS - max_exact)).astype(np.int32)
        large = np.minimum(large, REL_BUCKETS - 1)
        bkt = np.where(n < max_exact, n, large)
        out.append(np.where(rel >= 0, bkt, -1))
    return np.stack(out).astype(np.int32)


def _bias_kernel(rel_ref, bkt_ref, o_ref):
    h = pl.program_id(0)
    bkt = bkt_ref[...]
    far = rel_ref[REL_BUCKETS - 1, h]
    acc = jnp.zeros(bkt.shape, F32)
    for b in range(REL_BUCKETS):
        acc = jnp.where(bkt == b, rel_ref[b, h] - far, acc)
    o_ref[...] = jnp.where(bkt < 0, MASKED, acc * LOG2E)


def _bias_tiles(rel_bias, t):
    bkt = jnp.asarray(_rel_bucket_tiles(t))
    return pl.pallas_call(
        _bias_kernel,
        out_shape=jax.ShapeDtypeStruct((A_HEADS, 2, t, t), F32),
        grid=(A_HEADS,),
        in_specs=[
            pl.BlockSpec(memory_space=pltpu.SMEM),
            pl.BlockSpec((2, t, t), lambda h: (0, 0, 0)),
        ],
        out_specs=pl.BlockSpec((None, 2, t, t), lambda h: (h, 0, 0, 0)),
        name="rel_bias_tiles",
    )(rel_bias, bkt)


HEADS_PER_STEP = 4
PER_GROUP_WINDOW = pl.Buffered(1)


def _sublane_allreduce(x, op):
    for shift in (1, 2, 4):
        x = op(x, pltpu.roll(x, shift, 0))
    return x


def _rows3(x):
    return x.reshape(x.shape[0] // V7X_SUBLANES, V7X_SUBLANES, x.shape[1])


def _attn_scratch(hp, dv, t, c):
    stat = pltpu.VMEM((hp, V7X_SUBLANES, c), F32)
    return [stat, stat, stat, pltpu.VMEM((hp, dv, c), F32), pltpu.VMEM((hp, t, c), BF16)]


def _attn_init(m_sc, l_sc, a_sc, acc_sc, p_sc):
    m_sc[...] = jnp.full_like(m_sc, MASKED)
    l_sc[...] = jnp.zeros_like(l_sc)
    a_sc[...] = jnp.ones_like(a_sc)
    acc_sc[...] = jnp.zeros_like(acc_sc)
    p_sc[...] = jnp.zeros_like(p_sc)


def _attn_fold(vt_prev, h, a_sc, acc_sc, p_sc):
    pv = jnp.dot(vt_prev, p_sc[h], preferred_element_type=F32)
    acc_sc[h] = (_rows3(acc_sc[h]) * a_sc[h][None]).reshape(pv.shape) + pv


def _attn_stage(s, vt_prev, h, m_sc, l_sc, a_sc, acc_sc, p_sc):
    _attn_fold(vt_prev, h, a_sc, acc_sc, p_sc)
    s3 = _rows3(s)
    m_prev = m_sc[h]
    m_new = jnp.maximum(m_prev, _sublane_allreduce(jnp.max(s3, axis=0), jnp.maximum))
    alpha = jnp.exp2(m_prev - m_new)
    p3 = jnp.exp2(s3 - m_new[None])
    l_sc[h] = alpha * l_sc[h] + jnp.sum(p3, axis=0)
    p_sc[h] = p3.reshape(s.shape).astype(BF16)
    a_sc[h] = alpha
    m_sc[h] = m_new


def _attn_stage_fixed(s, vt_prev, h, m_sc, l_sc, a_sc, acc_sc, p_sc):
    _attn_fold(vt_prev, h, a_sc, acc_sc, p_sc)
    p3 = jnp.exp2(_rows3(s) - m_sc[h][None])
    l_sc[h] = l_sc[h] + jnp.sum(p3, axis=0)
    p_sc[h] = p3.reshape(s.shape).astype(BF16)
    a_sc[h] = jnp.ones_like(a_sc[h])


OVERFLOW_GUARD_LOG2 = 100


def _attn_run_guarded(run, l_sc):
    run(_attn_stage_fixed)

    @pl.when(jnp.max(l_sc[...]) > 2.0 ** OVERFLOW_GUARD_LOG2)
    def _():
        run(_attn_stage)


def _diff_attn_kernel(lam_ref, subln_ref, q_ref, k_ref, vt_ref, bias_ref, o_ref,
                      m_sc, l_sc, a_sc, acc_sc, p_sc, *, t, lam_init):
    qi = pl.program_id(2)
    state = (m_sc, l_sc, a_sc, acc_sc, p_sc)

    qqs = []
    for h in range(HEADS_PER_STEP):
        q = q_ref[:, h * A_VDIM:(h + 1) * A_VDIM]
        lane = lax.broadcasted_iota(I32, q.shape, 1)
        zero = jnp.zeros_like(q)
        qqs.append(jnp.concatenate([jnp.where(lane < A_HALF, q, zero), jnp.where(lane >= A_HALF, q, zero)], axis=0))

    lv = lam_ref[...]
    lam = (jnp.exp(jnp.sum(lv[0:1] * lv[1:2], axis=1, keepdims=True))
           - jnp.exp(jnp.sum(lv[2:3] * lv[3:4], axis=1, keepdims=True)) + lam_init)

    def tile(kj, near, stage):
        start = pl.multiple_of(kj * t, t)
        prev = jnp.minimum(kj + 1, qi)
        for h in range(HEADS_PER_STEP):
            cols = slice(h * A_VDIM, (h + 1) * A_VDIM)
            s = lax.dot_general(k_ref[pl.ds(start, t), cols], qqs[h], (((1,), (1,)), ((), ())),
                                preferred_element_type=F32)
            if near is not None:
                bias = bias_ref[h, near]
                s = s + jnp.concatenate([bias, bias], axis=1)
            stage(s, vt_ref[prev, cols, :], h, *state)

    def run(later_stage):
        _attn_init(*state)
        tile(qi, 0, _attn_stage)

        @pl.when(qi >= 1)
        def _():
            tile(qi - 1, 1, later_stage)

        def far_body(i, carry):
            tile(qi - 2 - i, None, later_stage)
            return carry

        lax.fori_loop(0, jnp.maximum(qi - 1, 0), far_body, 0)
        for h in range(HEADS_PER_STEP):
            _attn_fold(vt_ref[0, h * A_VDIM:(h + 1) * A_VDIM, :], h, a_sc, acc_sc, p_sc)
            l = _sublane_allreduce(l_sc[h], jnp.add)
            r = (_rows3(acc_sc[h]) / l[None]).reshape(A_VDIM, 2 * t)
            o = r[:, :t] - lam * r[:, t:]
            ms = _sublane_allreduce(jnp.sum(_rows3(o * o), axis=0), jnp.add) * (1.0 / A_VDIM)
            y = (_rows3(o) * lax.rsqrt(ms + LN_EPS)[None]).reshape(A_VDIM, t) * (subln_ref[...] * (1.0 - lam_init))
            o_ref[:, h * A_VDIM:(h + 1) * A_VDIM] = y.T.astype(o_ref.dtype)

    _attn_run_guarded(run, l_sc)


def _diff_attention(qk3, vt4, lam4, subln_col, bias, *, t, lam_init):
    bsz, seq, _ = qk3.shape
    nq = seq // t
    hp = HEADS_PER_STEP
    wid = hp * A_VDIM
    kblk = A_HEADS // hp
    return pl.pallas_call(
        functools.partial(_diff_attn_kernel, t=t, lam_init=lam_init),
        out_shape=jax.ShapeDtypeStruct((bsz, seq, A_HEADS * A_VDIM), BF16),
        grid=(bsz, A_HEADS // hp, nq),
        in_specs=[
            pl.BlockSpec((4, A_HALF), lambda b, h, i: (0, 0)),
            pl.BlockSpec((A_VDIM, 1), lambda b, h, i: (0, 0)),
            pl.BlockSpec((None, t, wid), lambda b, h, i: (b, i, h)),
            pl.BlockSpec((None, seq, wid), lambda b, h, i: (b, 0, kblk + h), pipeline_mode=PER_GROUP_WINDOW),
            pl.BlockSpec((None, nq, wid, t), lambda b, h, i: (b, 0, h, 0), pipeline_mode=PER_GROUP_WINDOW),
            pl.BlockSpec((hp, 2, t, t), lambda b, h, i: (h, 0, 0, 0), pipeline_mode=PER_GROUP_WINDOW),
        ],
        out_specs=pl.BlockSpec((None, t, wid), lambda b, h, i: (b, i, h)),
        scratch_shapes=_attn_scratch(hp, A_VDIM, t, 2 * t),
        compiler_params=pltpu.CompilerParams(
            dimension_semantics=("parallel", "parallel", "arbitrary"),
            vmem_limit_bytes=_vmem_limit(
                2 * _nbytes((t, wid), BF16),
                single_buffered_bytes=2 * _nbytes((seq, wid), BF16) + _nbytes((hp, 2, t, t), F32),
                scratch_bytes=_nbytes((hp, A_VDIM + t // 2 + t, 2 * t), F32)),
        ),
        name="diff_attention",
    )(lam4, subln_col, qk3, qk3, vt4, bias)


def _fox_attn_kernel(q_ref, k_ref, vt_ref, ck_ref, o_ref, m_sc, l_sc, a_sc, acc_sc, p_sc, *, t):
    qi = pl.program_id(2)
    state = (m_sc, l_sc, a_sc, acc_sc, p_sc)
    qs = [q_ref[:, h * C_HDIM:(h + 1) * C_HDIM] for h in range(HEADS_PER_STEP)]

    def tile(kj, diagonal, stage):
        start = pl.multiple_of(kj * t, t)
        prev = jnp.minimum(kj + 1, qi)
        for h in range(HEADS_PER_STEP):
            cols = slice(h * C_HDIM, (h + 1) * C_HDIM)
            s = lax.dot_general(k_ref[pl.ds(start, t), cols], qs[h], (((1,), (1,)), ((), ())),
                                preferred_element_type=F32)
            ck = ck_ref[h, pl.ds(start, t), :]
            s = s - jnp.concatenate([ck] * (t // V7X_LANES), axis=1)
            if diagonal:
                key = lax.broadcasted_iota(I32, s.shape, 0)
                qry = lax.broadcasted_iota(I32, s.shape, 1)
                s = jnp.where(key <= qry, s, MASKED)
            stage(s, vt_ref[prev, cols, :], h, *state)

    def run(later_stage):
        _attn_init(*state)
        tile(qi, True, _attn_stage)

        def body(i, carry):
            tile(qi - 1 - i, False, later_stage)
            return carry

        lax.fori_loop(0, qi, body, 0)
        for h in range(HEADS_PER_STEP):
            _attn_fold(vt_ref[0, h * C_HDIM:(h + 1) * C_HDIM, :], h, a_sc, acc_sc, p_sc)
            l = _sublane_allreduce(l_sc[h], jnp.add)
            o = (_rows3(acc_sc[h]) / l[None]).reshape(C_HDIM, t)
            o_ref[:, h * C_HDIM:(h + 1) * C_HDIM] = o.T.astype(o_ref.dtype)

    _attn_run_guarded(run, l_sc)


def _fox_attention(qk3, vt4, ckb, *, t):
    bsz, seq, _ = qk3.shape
    nq = seq // t
    hp = HEADS_PER_STEP
    wid = hp * C_HDIM
    qblk = 2 * (A_HEADS // hp)
    kblk = qblk + C_HEADS // hp
    vblk = A_HEADS // hp
    return pl.pallas_call(
        functools.partial(_fox_attn_kernel, t=t),
        out_shape=jax.ShapeDtypeStruct((bsz, seq, C_HEADS * C_HDIM), BF16),
        grid=(bsz, C_HEADS // hp, nq),
        in_specs=[
            pl.BlockSpec((None, t, wid), lambda b, h, i: (b, i, qblk + h)),
            pl.BlockSpec((None, seq, wid), lambda b, h, i: (b, 0, kblk + h), pipeline_mode=PER_GROUP_WINDOW),
            pl.BlockSpec((None, nq, wid, t), lambda b, h, i: (b, 0, vblk + h, 0), pipeline_mode=PER_GROUP_WINDOW),
            pl.BlockSpec((None, hp, seq, V7X_LANES), lambda b, h, i: (b, h, 0, 0), pipeline_mode=PER_GROUP_WINDOW),
        ],
        out_specs=pl.BlockSpec((None, t, wid), lambda b, h, i: (b, i, h)),
        scratch_shapes=_attn_scratch(hp, C_HDIM, t, t),
        compiler_params=pltpu.CompilerParams(
            dimension_semantics=("parallel", "parallel", "arbitrary"),
            vmem_limit_bytes=_vmem_limit(
                2 * _nbytes((t, wid), BF16),
                single_buffered_bytes=2 * _nbytes((seq, wid), BF16) + _nbytes((hp, seq, V7X_LANES), F32),
                scratch_bytes=_nbytes((hp, C_HDIM + t // 2 + t, t), F32)),
        ),
        name="fox_attention",
    )(qk3, qk3, vt4, ckb)


def _lru_kernel(bx_ref, bg_ref, cw_ref, cb_ref, wax_ref, bax_ref, lam_ref, o_ref, xbuf, hprev, *, t):
    pad = V7X_SUBLANES

    @pl.when(pl.program_id(1) == 0)
    def _():
        xbuf[0:pad, :] = jnp.zeros((pad, xbuf.shape[1]), F32)
        hprev[...] = jnp.zeros_like(hprev)

    xbuf[pad:pad + t, :] = bx_ref[...]
    xc = cb_ref[...]
    for tap in range(CONV_WIDTH):
        xc = xc + xbuf[pl.ds(pad - (CONV_WIDTH - 1) + tap, t), :] * cw_ref[tap:tap + 1, :]
    xbuf[0:pad, :] = bx_ref[t - pad:t, :]

    bw = xc.shape[1] // LRU_BLOCKS
    row = lax.broadcasted_iota(I32, (t, bw), 0)
    for g in range(LRU_BLOCKS):
        cols = slice(g * bw, (g + 1) * bw)
        xg = xc[:, cols]
        z = jnp.dot(xg.astype(BF16), wax_ref[g], preferred_element_type=F32) + bax_ref[g]
        r = jax.nn.sigmoid(z[:, :bw])
        gi = jax.nn.sigmoid(z[:, bw:])
        log_a = (LRU_C * r) * _log_sigmoid(lam_ref[:, cols])
        a = jnp.exp(log_a)
        u = jnp.sqrt(1.0 - jnp.exp(2.0 * log_a)) * (gi * xg)
        d = 1
        while d < t:
            keep = row >= d
            a_sh = jnp.where(keep, pltpu.roll(a, d, 0), 1.0)
            u_sh = jnp.where(keep, pltpu.roll(u, d, 0), 0.0)
            u = a * u_sh + u
            a = a * a_sh
            d *= 2
        h = a * hprev[0:1, cols] + u
        hprev[0:1, cols] = h[t - 1:t, :]
        o_ref[:, cols] = (jax.nn.gelu(bg_ref[:, cols], approximate=True) * h).astype(o_ref.dtype)


def _lru_branch(bxg, conv_w, conv_b, wax, bax, lam, *, bsz, t):
    n, two_w = bxg.shape
    w = two_w // 2
    nt = n // bsz // t
    bw = w // LRU_BLOCKS
    return pl.pallas_call(
        functools.partial(_lru_kernel, t=t),
        out_shape=jax.ShapeDtypeStruct((n, w), BF16),
        grid=(bsz, nt),
        in_specs=[
            pl.BlockSpec((t, w), lambda b, i: (b * nt + i, 0)),
            pl.BlockSpec((t, w), lambda b, i: (b * nt + i, 1)),
            pl.BlockSpec((CONV_WIDTH, w), lambda b, i: (0, 0)),
            pl.BlockSpec((1, w), lambda b, i: (0, 0)),
            pl.BlockSpec((LRU_BLOCKS, bw, 2 * bw), lambda b, i: (0, 0, 0)),
            pl.BlockSpec((LRU_BLOCKS, 1, 2 * bw), lambda b, i: (0, 0, 0)),
            pl.BlockSpec((1, w), lambda b, i: (0, 0)),
        ],
        out_specs=pl.BlockSpec((t, w), lambda b, i: (b * nt + i, 0)),
        scratch_shapes=[
            pltpu.VMEM((t + V7X_SUBLANES, w), F32),
            pltpu.VMEM((V7X_SUBLANES, w), F32),
        ],
        compiler_params=pltpu.CompilerParams(dimension_semantics=("parallel", "arbitrary")),
        name="conv_rglru",
    )(bxg, bxg, conv_w, conv_b, wax, bax, lam)


def _merge_kernel(ya_ref, yb_ref, yc_ref, w_ref, g0_ref, g1_ref, g2_ref, o_ref):
    acc = g0_ref[...].astype(F32) * jnp.dot(ya_ref[...], w_ref[0], preferred_element_type=F32)
    acc = acc + g1_ref[...].astype(F32) * jnp.dot(yb_ref[...], w_ref[1], preferred_element_type=F32)
    acc = acc + g2_ref[...].astype(F32) * jnp.dot(yc_ref[...], w_ref[2], preferred_element_type=F32)
    o_ref[...] = acc.astype(o_ref.dtype)


def _merge(ya, yb, yc, wb, gates, *, tm, tn):
    n, bwid = ya.shape
    dm = wb.shape[2]
    nc = dm // tn
    y_spec = pl.BlockSpec((tm, bwid), lambda j, i: (i, 0))
    return pl.pallas_call(
        _merge_kernel,
        out_shape=jax.ShapeDtypeStruct((n, dm), BF16),
        grid=(nc, n // tm),
        in_specs=[
            y_spec, y_spec, y_spec,
            pl.BlockSpec((N_BRANCH, bwid, tn), lambda j, i: (0, 0, j)),
            pl.BlockSpec((tm, tn), lambda j, i: (i, j)),
            pl.BlockSpec((tm, tn), lambda j, i: (i, nc + j)),
            pl.BlockSpec((tm, tn), lambda j, i: (i, 2 * nc + j)),
        ],
        out_specs=pl.BlockSpec((tm, tn), lambda j, i: (i, j)),
        compiler_params=pltpu.CompilerParams(
            dimension_semantics=("parallel", "parallel"),
            vmem_limit_bytes=_vmem_limit(3 * _nbytes((tm, bwid), BF16), _nbytes((N_BRANCH, bwid, tn), BF16),
                                         4 * _nbytes((tm, tn), BF16)),
        ),
        name="gated_merge",
    )(ya, yb, yc, wb, gates, gates, gates)


def _wo_ln_kernel(m_ref, w_ref, x_ref, g_ref, b_ref, o32_ref, o16_ref, *, alpha):
    y = jnp.dot(m_ref[...], w_ref[...], preferred_element_type=F32) + alpha * x_ref[...]
    out = _layer_norm_rows(y, g_ref[...], b_ref[...])
    o32_ref[...] = out
    o16_ref[...] = out.astype(BF16)


def _wo_ln(merged, wo, x, g, b, *, alpha, tm):
    n, dm = x.shape
    row = pl.BlockSpec((tm, dm), lambda i: (i, 0))
    vec = pl.BlockSpec((1, dm), lambda i: (0, 0))
    return pl.pallas_call(
        functools.partial(_wo_ln_kernel, alpha=alpha),
        out_shape=(jax.ShapeDtypeStruct((n, dm), F32), jax.ShapeDtypeStruct((n, dm), BF16)),
        grid=(n // tm,),
        in_specs=[row, pl.BlockSpec((dm, dm), lambda i: (0, 0), pipeline_mode=pl.Buffered(1)), row, vec, vec],
        out_specs=(row, row),
        compiler_params=pltpu.CompilerParams(
            dimension_semantics=("parallel",),
            vmem_limit_bytes=_vmem_limit(2 * _nbytes((tm, dm), BF16), 2 * _nbytes((tm, dm), F32),
                                         single_buffered_bytes=_nbytes((dm, dm), BF16),
                                         scratch_bytes=_nbytes((tm, dm), F32)),
        ),
        name="wo_layernorm",
    )(merged, wo, x, g, b)


def _ple_kernel(x_ref, wg_ref, bg_ref, p_ref, we_ref, o_ref):
    gate = jax.nn.sigmoid(jnp.dot(x_ref[...], wg_ref[...], preferred_element_type=F32) + bg_ref[...])
    o_ref[...] = gate * jnp.dot(p_ref[...], we_ref[...], preferred_element_type=F32)


def _ple(x1b, wg, bg, pb, we, *, tm, tn):
    n, dm = x1b.shape
    pdim = pb.shape[1]
    return pl.pallas_call(
        _ple_kernel,
        out_shape=jax.ShapeDtypeStruct((n, dm), F32),
        grid=(dm // tn, n // tm),
        in_specs=[
            pl.BlockSpec((tm, dm), lambda j, i: (i, 0)),
            pl.BlockSpec((dm, tn), lambda j, i: (0, j)),
            pl.BlockSpec((1, tn), lambda j, i: (0, j)),
            pl.BlockSpec((tm, pdim), lambda j, i: (i, 0)),
            pl.BlockSpec((pdim, tn), lambda j, i: (0, j)),
        ],
        out_specs=pl.BlockSpec((tm, tn), lambda j, i: (i, j)),
        compiler_params=pltpu.CompilerParams(
            dimension_semantics=("parallel", "parallel"),
            vmem_limit_bytes=_vmem_limit(_nbytes((tm, dm), BF16), _nbytes((dm, tn), BF16),
                                         _nbytes((tm, tn), F32)),
        ),
        name="gated_ple",
    )(x1b, wg, bg, pb, we)


def _top_rows(s, payload, k):
    rows = s.shape[0]
    rid = lax.broadcasted_iota(I32, s.shape, 0)
    vals, pays = [], []
    for _ in range(k):
        m = jnp.max(s, axis=0, keepdims=True)
        pos = jnp.min(jnp.where(s == m, rid, rows), axis=0, keepdims=True)
        hit = rid == pos
        vals.append(m)
        pays.append(pos if payload is None else jnp.max(jnp.where(hit, payload, -1), axis=0, keepdims=True))
        s = jnp.where(hit, -jnp.inf, s)
    return jnp.concatenate(vals, axis=0), jnp.concatenate(pays, axis=0)


def _peer_route_kernel(x_ref, wq_ref, keys_ref, idx_ref, g_ref):
    q = jnp.dot(x_ref[...], wq_ref[...], preferred_element_type=F32).astype(BF16)
    st = lax.dot_general(keys_ref[...], q, (((1,), (1,)), ((), ())), preferred_element_type=F32)
    s1, i1 = _top_rows(st[:PEER_NKEYS], None, PEER_TOPK)
    s2, i2 = _top_rows(st[PEER_NKEYS:], None, PEER_TOPK)
    cand, cidx = [], []
    for a in range(PEER_TOPK):
        nb = PEER_TOPK // (a + 1)
        cand.append(s1[a:a + 1] + s2[:nb])
        cidx.append(i1[a:a + 1] * PEER_NKEYS + i2[:nb])
    npairs = sum(c.shape[0] for c in cand)
    npad = -npairs % V7X_SUBLANES
    cand.append(jnp.full((npad, st.shape[1]), -jnp.inf, F32))
    cidx.append(jnp.full((npad, st.shape[1]), -1, I32))
    sc, idx = _top_rows(jnp.concatenate(cand, axis=0), jnp.concatenate(cidx, axis=0), PEER_TOPK)
    e = jnp.exp(sc - jnp.max(sc, axis=0, keepdims=True))
    idx_ref[...] = idx
    g_ref[...] = e / jnp.sum(e, axis=0, keepdims=True)


def _peer_route(x1b, wq, keys_t, *, tm):
    n, dm = x1b.shape
    qd = 2 * PEER_HALF
    slots = PEER_HEADS * PEER_TOPK
    out_spec = pl.BlockSpec((PEER_TOPK, tm), lambda i, h: (h, i))
    return pl.pallas_call(
        _peer_route_kernel,
        out_shape=(jax.ShapeDtypeStruct((slots, n), I32), jax.ShapeDtypeStruct((slots, n), F32)),
        grid=(n // tm, PEER_HEADS),
        in_specs=[
            pl.BlockSpec((tm, dm), lambda i, h: (i, 0)),
            pl.BlockSpec((dm, qd), lambda i, h: (0, h)),
            pl.BlockSpec((None, 2 * PEER_NKEYS, qd), lambda i, h: (h, 0, 0)),
        ],
        out_specs=(out_spec, out_spec),
        compiler_params=pltpu.CompilerParams(dimension_semantics=("parallel", "parallel")),
        name="peer_route",
    )(x1b, wq, keys_t)


W_ROW_PITCH = PEER_NKEYS + V7X_SUBLANES


def _peer_gate_kernel(idx_ref, g_ref, o_ref, wbuf, idx_sc, g_sc, *, tb):
    idx_sc[...] = idx_ref[...].T
    g_sc[...] = g_ref[...].T
    rid = lax.broadcasted_iota(I32, (PEER_NKEYS, PEER_NKEYS), 0).astype(F32).astype(BF16)
    one = jnp.ones((PEER_NKEYS, PEER_NKEYS), BF16)
    zero = jnp.zeros((PEER_NKEYS, PEER_NKEYS), BF16)
    group = 2 * V7X_SUBLANES
    packed_rows = 2 * V7X_SUBLANES

    def body(j, carry):
        base = pl.multiple_of(j * group, group)
        ib = idx_sc[pl.ds(base, group), :]
        gb = g_sc[pl.ds(base, group), :]
        i1b = (ib >> (PEER_NKEYS.bit_length() - 1)).astype(F32)
        i2b = (ib & (PEER_NKEYS - 1)).astype(F32)
        for r in range(group):
            def rep(v):
                one_vreg = jnp.broadcast_to(v[r:r + 1, :], (packed_rows, PEER_NKEYS)).astype(BF16)
                return jnp.concatenate([one_vreg] * (PEER_NKEYS // packed_rows), axis=0)
            r1 = jnp.where(rid == rep(i1b), rep(gb), zero)
            r2t = jnp.where(rid == rep(i2b), one, zero)
            wt = lax.dot_general(r1, r2t, (((1,), (1,)), ((), ())), preferred_element_type=F32)
            wbuf[pl.ds(pl.multiple_of((base + r) * W_ROW_PITCH, V7X_SUBLANES), PEER_NKEYS), :] = wt
        return carry

    lax.fori_loop(0, tb // group, body, 0)
    for i1 in range(PEER_NKEYS):
        o_ref[:, i1 * PEER_NKEYS:(i1 + 1) * PEER_NKEYS] = (
            wbuf[pl.ds(i1, tb, stride=W_ROW_PITCH), :].astype(o_ref.dtype))


def _peer_gate_matrix(idx_t, g_t, *, tb):
    slots, n = idx_t.shape
    ne = PEER_NKEYS * PEER_NKEYS
    in_spec = pl.BlockSpec((slots, tb), lambda i: (0, i))
    return pl.pallas_call(
        functools.partial(_peer_gate_kernel, tb=tb),
        out_shape=jax.ShapeDtypeStruct((n, ne), BF16),
        grid=(n // tb,),
        in_specs=[in_spec, in_spec],
        out_specs=pl.BlockSpec((tb, ne), lambda i: (i, 0)),
        scratch_shapes=[
            pltpu.VMEM((tb * W_ROW_PITCH, PEER_NKEYS), F32),
            pltpu.VMEM((tb, slots), I32),
            pltpu.VMEM((tb, slots), F32),
        ],
        compiler_params=pltpu.CompilerParams(
            dimension_semantics=("parallel",),
            vmem_limit_bytes=_vmem_limit(_nbytes((tb, ne), BF16),
                                         scratch_bytes=_nbytes((tb * W_ROW_PITCH, PEER_NKEYS), F32)),
        ),
        name="peer_gate_matrix",
    )(idx_t, g_t)


def _peer_expert_kernel(x_ref, u_ref, v_ref, w_ref, o_ref):
    @pl.when(pl.program_id(1) == 0)
    def _():
        o_ref[...] = jnp.zeros_like(o_ref)

    a = lax.dot_general(x_ref[...], u_ref[...], (((1,), (1,)), ((), ())), preferred_element_type=F32)
    gelu = 0.5 * a * (1.0 + lax.erf(a * (0.5 ** 0.5)))
    g = (w_ref[...].astype(F32) * gelu).astype(BF16)
    o_ref[...] += jnp.dot(g, v_ref[...], preferred_element_type=F32)


def _peer_experts(x1b, u, v, w, *, tm, ce):
    n, dm = x1b.shape
    ne = v.shape[0]
    table_chunk = pl.BlockSpec((ce, dm), lambda i, j: (j, 0))
    return pl.pallas_call(
        _peer_expert_kernel,
        out_shape=jax.ShapeDtypeStruct((n, dm), F32),
        grid=(n // tm, ne // ce),
        in_specs=[
            pl.BlockSpec((tm, dm), lambda i, j: (i, 0)),
            table_chunk,
            table_chunk,
            pl.BlockSpec((tm, ce), lambda i, j: (i, j)),
        ],
        out_specs=pl.BlockSpec((tm, dm), lambda i, j: (i, 0)),
        compiler_params=pltpu.CompilerParams(
            dimension_semantics=("parallel", "arbitrary"),
            vmem_limit_bytes=_vmem_limit(_nbytes((tm, dm), BF16), 2 * _nbytes((dm, ce), BF16),
                                         _nbytes((tm, ce), BF16), _nbytes((tm, dm), F32)),
        ),
        name="peer_experts",
    )(x1b, u, v, w)


def _residual_ln_kernel(x_ref, y_ref, p_ref, g_ref, b_ref, o32_ref, o16_ref, *, alpha):
    out = _layer_norm_rows(alpha * x_ref[...] + y_ref[...] + p_ref[...], g_ref[...], b_ref[...])
    o32_ref[...] = out
    o16_ref[...] = out.astype(BF16)


def _residual_ln(x1, yf, ple, g, b, *, alpha, tm):
    n, dm = x1.shape
    row = pl.BlockSpec((tm, dm), lambda i: (i, 0))
    vec = pl.BlockSpec((1, dm), lambda i: (0, 0))
    return pl.pallas_call(
        functools.partial(_residual_ln_kernel, alpha=alpha),
        out_shape=(jax.ShapeDtypeStruct((n, dm), F32), jax.ShapeDtypeStruct((n, dm), BF16)),
        grid=(n // tm,),
        in_specs=[row, row, row, vec, vec],
        out_specs=(row, row),
        compiler_params=pltpu.CompilerParams(
            dimension_semantics=("parallel",),
            vmem_limit_bytes=_vmem_limit(4 * _nbytes((tm, dm), F32), _nbytes((tm, dm), BF16)),
        ),
        name="residual_layernorm",
    )(x1, yf, ple, g, b)


def _peer_keys_layout(keys):
    z = jnp.zeros_like(keys[:, 0])
    top = jnp.concatenate([keys[:, 0], z], axis=-1)
    bot = jnp.concatenate([z, keys[:, 1]], axis=-1)
    return jnp.concatenate([top, bot], axis=1).astype(BF16)


def _qk_weight_block(j):
    return j + 3 * (j // 2)


def _lru_weight_block(j):
    return j + 3


def kernel(x, p, w_in, b_in, diff_lambda, diff_subln, rel_bias, conv_w, conv_b, lru_wa, lru_ba, lru_wx,
           lru_bx, lru_lambda, w_branch, w_o, ln1_g, ln1_b, peer_wq, peer_keys, peer_u, peer_v, w_ple,
           w_ple_gate, b_ple_gate, ln2_g, ln2_b):
    bsz, seq, dm = x.shape
    depth = w_in.shape[0]
    n = bsz * seq
    alpha = (2 * depth) ** 0.25
    bwid = A_HEADS * A_VDIM
    t_attn = min(T_ATTN, seq)
    assert seq % t_attn == 0 and t_attn >= REL_MAX_DIST

    o_va, o_lru, o_c, o_vc, o_f = (j * bwid for j in (2, 3, 5, 7, 8))
    o_g = o_f + C_HEADS
    qk_scale = jnp.concatenate([
        jnp.full((bwid,), A_HALF ** -0.5 * LOG2E, F32), jnp.ones((bwid,), F32),
        jnp.full((bwid,), C_HDIM ** -0.5 * LOG2E, F32), jnp.ones((bwid,), F32)])[None]

    bias = _bias_tiles(rel_bias, t_attn)
    x2 = x.reshape(n, dm)
    xb = x2.astype(BF16)
    w_in_b = w_in.astype(BF16)
    for i in range(depth):
        w = w_in_b[i]
        b = b_in[i]
        b_qk = jnp.concatenate([b[:o_va], b[o_c:o_vc]])[None]
        qk = _proj(xb, w, b_qk, qk_scale, out_dtype=BF16, tm=TM_PROJ, tn=bwid, w_block=_qk_weight_block,
                   name="proj_qk")
        w_v = jnp.concatenate([w[:, o_va:o_lru], w[:, o_vc:o_f]], axis=1).T
        b_v = jnp.concatenate([b[o_va:o_lru], b[o_vc:o_f]])[:, None]
        vt = _proj_t(xb, w_v, b_v, tm=t_attn, tn=bwid, name="proj_v_t")
        bxg = _proj(xb, w, b[None, o_lru:o_c], out_dtype=F32, tm=TM_PROJ, tn=bwid, w_block=_lru_weight_block,
                    name="proj_lru")
        gates = _proj(xb, w[:, o_g:], b[None, o_g:],
                      out_dtype=BF16, tm=TM_PROJ, tn=TN_PROJ, act="sigmoid", name="proj_gates")
        ckb = _fgate_cumsum(xb.reshape(bsz, seq, dm), w[:, o_f:o_g].T, b[o_f:o_g, None],
                            ts=min(seq, TS_FGATE))

        qk3 = qk.reshape(bsz, seq, 4 * bwid)
        vt4 = vt.reshape(bsz, seq // t_attn, 2 * bwid, t_attn)
        lam_init = 0.8 - 0.6 * math.exp(-0.3 * i)
        ya = _diff_attention(qk3, vt4, diff_lambda[i], diff_subln[i][:, None], bias, t=t_attn, lam_init=lam_init)
        yc = _fox_attention(qk3, vt4, ckb, t=t_attn)

        bw = bwid // LRU_BLOCKS
        wax = jnp.concatenate([lru_wa[i], lru_wx[i]], axis=-1).astype(BF16)
        bax = jnp.concatenate([lru_ba[i].reshape(LRU_BLOCKS, 1, bw), lru_bx[i].reshape(LRU_BLOCKS, 1, bw)], axis=-1)
        yb = _lru_branch(bxg, conv_w[i], conv_b[i][None], wax, bax, lru_lambda[i][None], bsz=bsz, t=T_SCAN)

        merged = _merge(ya.reshape(n, bwid), yb, yc.reshape(n, bwid), w_branch[i].astype(BF16), gates,
                        tm=TM_MERGE, tn=TN_MERGE)
        x1, x1b = _wo_ln(merged, w_o[i].astype(BF16), x2, ln1_g[i][None], ln1_b[i][None], alpha=alpha, tm=TM_LN)

        ple = _ple(x1b, w_ple_gate[i].astype(BF16), b_ple_gate[i][None], p[i].reshape(n, -1).astype(BF16),
                   w_ple[i].astype(BF16), tm=TM_PLE, tn=TN_PLE)
        idx_t, g_t = _peer_route(x1b, peer_wq[i].astype(BF16), _peer_keys_layout(peer_keys[i]), tm=TM_ROUTE)
        wdense = _peer_gate_matrix(idx_t, g_t, tb=TB_GATE)
        yf = _peer_experts(x1b, peer_u[i].astype(BF16), peer_v[i].astype(BF16), wdense, tm=TM_EXPERT, ce=CE_EXPERT)
        x2, xb = _residual_ln(x1, yf, ple, ln2_g[i][None], ln2_b[i][None], alpha=alpha, tm=TM_LN)
    return x2.reshape(bsz, seq, dm)
```

```python
import functools
import math

import jax
import jax.numpy as jnp
import numpy as np
from jax import lax
from jax.experimental import pallas as pl
from jax.experimental.pallas import tpu as pltpu

F32, BF16, I32 = jnp.float32, jnp.bfloat16, jnp.int32

V7X_VMEM_BYTES = 64 * 2**20
V7X_LANES = 128
V7X_SUBLANES = 8
VMEM_HEADROOM_BYTES = 8 * 2**20

A_HEADS = 8
A_HALF = 64
A_VDIM = 2 * A_HALF
LRU_BLOCKS = 8
CONV_WIDTH = 4
LRU_C = 8.0
C_HEADS = 8
C_HDIM = 128
N_BRANCH = 3
REL_BUCKETS = 32
REL_MAX_DIST = 128
PEER_HEADS = 8
PEER_NKEYS = 128
PEER_HALF = 64
PEER_TOPK = 16
LN_EPS = 1e-5

LOG2E = 1.4426950408889634
MASKED = -1e30

T_ATTN = 512
TM_PROJ, TN_PROJ = 1024, 2048
TN_PROJ_F32 = 1024
TS_FGATE = 1024
T_SCAN = 256
TM_MERGE, TN_MERGE = 512, 1024
TM_LN = 512
TM_ROUTE = 1024
TB_GATE = 256
TM_EXPERT, CE_EXPERT = 1024, 1024


def _vmem_limit(*block_bytes, scratch_bytes=0, single_buffered_bytes=0):
    need = 2 * sum(block_bytes) + single_buffered_bytes + scratch_bytes + VMEM_HEADROOM_BYTES
    return int(min(max(need, 32 * 2**20), V7X_VMEM_BYTES - 4 * 2**20))


def _nbytes(shape, dtype):
    return int(np.prod(shape)) * jnp.dtype(dtype).itemsize


def _log_sigmoid(z):
    return jnp.minimum(z, 0.0) - jnp.log1p(jnp.exp(-jnp.abs(z)))


def _layer_norm_rows(y, g, b):
    mu = jnp.mean(y, axis=1, keepdims=True)
    yc = y - mu
    var = jnp.mean(yc * yc, axis=1, keepdims=True)
    return yc * lax.rsqrt(var + LN_EPS) * g + b


def _proj_kernel(x_ref, w_ref, b_ref, *rest, act, scaled):
    o_ref = rest[-1]
    y = jnp.dot(x_ref[...], w_ref[...], preferred_element_type=F32) + b_ref[...]
    if scaled:
        y = y * rest[0][...]
    if act == "sigmoid":
        y = jax.nn.sigmoid(y)
    o_ref[...] = y.astype(o_ref.dtype)


def _proj(x, w, b, s=None, *, out_dtype, tm, tn, act=None, name):
    m, k = x.shape
    n = w.shape[1]
    grid = (n // tn, m // tm)
    row = pl.BlockSpec((1, tn), lambda j, i: (0, j))
    operands = (x, w, b) if s is None else (x, w, b, s)
    return pl.pallas_call(
        functools.partial(_proj_kernel, act=act, scaled=s is not None),
        out_shape=jax.ShapeDtypeStruct((m, n), out_dtype),
        grid=grid,
        in_specs=[
            pl.BlockSpec((tm, k), lambda j, i: (i, 0)),
            pl.BlockSpec((k, tn), lambda j, i: (0, j)),
        ] + [row] * (len(operands) - 2),
        out_specs=pl.BlockSpec((tm, tn), lambda j, i: (i, j)),
        compiler_params=pltpu.CompilerParams(
            dimension_semantics=("parallel", "parallel"),
            vmem_limit_bytes=_vmem_limit(_nbytes((tm, k), x.dtype), _nbytes((k, tn), w.dtype),
                                         _nbytes((tm, tn), out_dtype)),
        ),
        name=name,
    )(*operands)


def _proj_t_kernel(w_ref, x_ref, b_ref, o_ref):
    y = lax.dot_general(w_ref[...], x_ref[...], (((1,), (1,)), ((), ())), preferred_element_type=F32)
    o_ref[...] = (y + b_ref[...]).astype(o_ref.dtype)


def _proj_t(x, w_t, b_col, *, tm, tn, name):
    m, k = x.shape
    n = w_t.shape[0]
    return pl.pallas_call(
        _proj_t_kernel,
        out_shape=jax.ShapeDtypeStruct((m // tm, n, tm), BF16),
        grid=(n // tn, m // tm),
        in_specs=[
            pl.BlockSpec((tn, k), lambda j, i: (j, 0)),
            pl.BlockSpec((tm, k), lambda j, i: (i, 0)),
            pl.BlockSpec((tn, 1), lambda j, i: (j, 0)),
        ],
        out_specs=pl.BlockSpec((None, tn, tm), lambda j, i: (i, j, 0)),
        compiler_params=pltpu.CompilerParams(
            dimension_semantics=("parallel", "parallel"),
            vmem_limit_bytes=_vmem_limit(_nbytes((tm, k), BF16), _nbytes((tn, k), BF16), _nbytes((tn, tm), BF16)),
        ),
        name=name,
    )(w_t, x, b_col)


def _fgate_kernel(x_ref, w_ref, b_ref, o_ref, carry_ref, *, ts):
    @pl.when(pl.program_id(1) == 0)
    def _():
        carry_ref[...] = jnp.zeros_like(carry_ref)

    z = lax.dot_general(w_ref[...], x_ref[...], (((1,), (1,)), ((), ())),
                        preferred_element_type=F32) + b_ref[...]
    c = _log_sigmoid(z)
    lane = lax.broadcasted_iota(I32, c.shape, 1)
    d = 1
    while d < ts:
        c = c + jnp.where(lane >= d, pltpu.roll(c, d, 1), 0.0)
        d *= 2
    c = c + carry_ref[:, 0:1]
    carry_ref[...] = jnp.broadcast_to(c[:, ts - 1:ts], carry_ref.shape)
    c = c * LOG2E
    for h in range(C_HEADS):
        o_ref[h] = jnp.broadcast_to(c[h:h + 1, :], (V7X_LANES, ts)).T


def _fgate_cumsum(x3, wf_t, bf, *, ts):
    bsz, seq, dm = x3.shape
    return pl.pallas_call(
        functools.partial(_fgate_kernel, ts=ts),
        out_shape=jax.ShapeDtypeStruct((bsz, C_HEADS, seq, V7X_LANES), F32),
        grid=(bsz, seq // ts),
        in_specs=[
            pl.BlockSpec((None, ts, dm), lambda b, i: (b, i, 0)),
            pl.BlockSpec((C_HEADS, dm), lambda b, i: (0, 0)),
            pl.BlockSpec((C_HEADS, 1), lambda b, i: (0, 0)),
        ],
        out_specs=pl.BlockSpec((None, C_HEADS, ts, V7X_LANES), lambda b, i: (b, 0, i, 0)),
        scratch_shapes=[pltpu.VMEM((C_HEADS, V7X_LANES), F32)],
        compiler_params=pltpu.CompilerParams(dimension_semantics=("parallel", "arbitrary")),
        name="fgate_cumsum",
    )(x3, wf_t, bf)


def _rel_bucket_tiles(t):
    q = np.arange(t)[None, :]
    k = np.arange(t)[:, None]
    out = []
    for off in (0, t):
        rel = q - k + off
        n = np.maximum(rel, 0)
        max_exact = REL_BUCKETS // 2
        nf = np.maximum(n, 1).astype(np.float32)
        large = max_exact + (np.log(nf / np.float32(max_exact)) / np.float32(math.log(REL_MAX_DIST / max_exact))
                             * np.float32(REL_BUCKETS - max_exact)).astype(np.int32)
        large = np.minimum(large, REL_BUCKETS - 1)
        bkt = np.where(n < max_exact, n, large)
        out.append(np.where(rel >= 0, bkt, -1))
    return np.stack(out).astype(np.int32)


def _bias_kernel(rel_ref, bkt_ref, o_ref):
    h = pl.program_id(0)
    bkt = bkt_ref[...]
    far = rel_ref[REL_BUCKETS - 1, h]
    acc = jnp.zeros(bkt.shape, F32)
    for b in range(REL_BUCKETS):
        acc = jnp.where(bkt == b, rel_ref[b, h] - far, acc)
    o_ref[...] = jnp.where(bkt < 0, MASKED, acc * LOG2E)


def _bias_tiles(rel_bias, t):
    bkt = jnp.asarray(_rel_bucket_tiles(t))
    return pl.pallas_call(
        _bias_kernel,
        out_shape=jax.ShapeDtypeStruct((A_HEADS, 2, t, t), F32),
        grid=(A_HEADS,),
        in_specs=[
            pl.BlockSpec(memory_space=pltpu.SMEM),
            pl.BlockSpec((2, t, t), lambda h: (0, 0, 0)),
        ],
        out_specs=pl.BlockSpec((None, 2, t, t), lambda h: (h, 0, 0, 0)),
        name="rel_bias_tiles",
    )(rel_bias, bkt)


HEADS_PER_STEP = 4
PER_GROUP_WINDOW = pl.Buffered(1)


def _sublane_allreduce(x, op):
    for shift in (1, 2, 4):
        x = op(x, pltpu.roll(x, shift, 0))
    return x


def _rows3(x):
    return x.reshape(x.shape[0] // V7X_SUBLANES, V7X_SUBLANES, x.shape[1])


def _attn_scratch(hp, dv, t, c):
    stat = pltpu.VMEM((hp, V7X_SUBLANES, c), F32)
    return [stat, stat, stat, pltpu.VMEM((hp, dv, c), F32), pltpu.VMEM((hp, t, c), BF16)]


def _attn_init(m_sc, l_sc, a_sc, acc_sc, p_sc):
    m_sc[...] = jnp.full_like(m_sc, MASKED)
    l_sc[...] = jnp.zeros_like(l_sc)
    a_sc[...] = jnp.ones_like(a_sc)
    acc_sc[...] = jnp.zeros_like(acc_sc)
    p_sc[...] = jnp.zeros_like(p_sc)


def _attn_fold(vt_prev, h, a_sc, acc_sc, p_sc):
    pv = jnp.dot(vt_prev, p_sc[h], preferred_element_type=F32)
    acc_sc[h] = (_rows3(acc_sc[h]) * a_sc[h][None]).reshape(pv.shape) + pv


def _attn_stage(s, vt_prev, h, m_sc, l_sc, a_sc, acc_sc, p_sc):
    _attn_fold(vt_prev, h, a_sc, acc_sc, p_sc)
    s3 = _rows3(s)
    m_prev = m_sc[h]
    m_new = jnp.maximum(m_prev, _sublane_allreduce(jnp.max(s3, axis=0), jnp.maximum))
    alpha = jnp.exp2(m_prev - m_new)
    p3 = jnp.exp2(s3 - m_new[None])
    l_sc[h] = alpha * l_sc[h] + jnp.sum(p3, axis=0)
    p_sc[h] = p3.reshape(s.shape).astype(BF16)
    a_sc[h] = alpha
    m_sc[h] = m_new


def _attn_stage_fixed(s, vt_prev, h, m_sc, l_sc, a_sc, acc_sc, p_sc):
    _attn_fold(vt_prev, h, a_sc, acc_sc, p_sc)
    p3 = jnp.exp2(_rows3(s) - m_sc[h][None])
    l_sc[h] = l_sc[h] + jnp.sum(p3, axis=0)
    p_sc[h] = p3.reshape(s.shape).astype(BF16)
    a_sc[h] = jnp.ones_like(a_sc[h])


OVERFLOW_GUARD_LOG2 = 100


def _attn_run_guarded(run, l_sc):
    run(_attn_stage_fixed)

    @pl.when(jnp.max(l_sc[...]) > 2.0 ** OVERFLOW_GUARD_LOG2)
    def _():
        run(_attn_stage)


def _diff_attn_kernel(lam_ref, subln_ref, q_ref, k_ref, vt_ref, bias_ref, o_ref,
                      m_sc, l_sc, a_sc, acc_sc, p_sc, *, t, lam_init):
    qi = pl.program_id(2)
    state = (m_sc, l_sc, a_sc, acc_sc, p_sc)

    qqs = []
    for h in range(HEADS_PER_STEP):
        q = q_ref[:, h * A_VDIM:(h + 1) * A_VDIM]
        lane = lax.broadcasted_iota(I32, q.shape, 1)
        zero = jnp.zeros_like(q)
        qqs.append(jnp.concatenate([jnp.where(lane < A_HALF, q, zero), jnp.where(lane >= A_HALF, q, zero)], axis=0))

    lv = lam_ref[...]
    lam = (jnp.exp(jnp.sum(lv[0:1] * lv[1:2], axis=1, keepdims=True))
           - jnp.exp(jnp.sum(lv[2:3] * lv[3:4], axis=1, keepdims=True)) + lam_init)

    def tile(kj, near, stage):
        start = pl.multiple_of(kj * t, t)
        prev = jnp.minimum(kj + 1, qi)
        for h in range(HEADS_PER_STEP):
            cols = slice(h * A_VDIM, (h + 1) * A_VDIM)
            s = lax.dot_general(k_ref[pl.ds(start, t), cols], qqs[h], (((1,), (1,)), ((), ())),
                                preferred_element_type=F32)
            if near is not None:
                bias = bias_ref[h, near]
                s = s + jnp.concatenate([bias, bias], axis=1)
            stage(s, vt_ref[prev, cols, :], h, *state)

    def run(later_stage):
        _attn_init(*state)
        tile(qi, 0, _attn_stage)

        @pl.when(qi >= 1)
        def _():
            tile(qi - 1, 1, later_stage)

        def far_body(i, carry):
            tile(qi - 2 - i, None, later_stage)
            return carry

        lax.fori_loop(0, jnp.maximum(qi - 1, 0), far_body, 0)
        for h in range(HEADS_PER_STEP):
            _attn_fold(vt_ref[0, h * A_VDIM:(h + 1) * A_VDIM, :], h, a_sc, acc_sc, p_sc)
            l = _sublane_allreduce(l_sc[h], jnp.add)
            r = (_rows3(acc_sc[h]) / l[None]).reshape(A_VDIM, 2 * t)
            o = r[:, :t] - lam * r[:, t:]
            ms = _sublane_allreduce(jnp.sum(_rows3(o * o), axis=0), jnp.add) * (1.0 / A_VDIM)
            y = (_rows3(o) * lax.rsqrt(ms + LN_EPS)[None]).reshape(A_VDIM, t) * (subln_ref[...] * (1.0 - lam_init))
            o_ref[:, h * A_VDIM:(h + 1) * A_VDIM] = y.T.astype(o_ref.dtype)

    _attn_run_guarded(run, l_sc)


def _diff_attention(qk3, vt4, lam4, subln_col, bias, *, t, lam_init):
    bsz, seq, _ = qk3.shape
    nq = seq // t
    hp = HEADS_PER_STEP
    wid = hp * A_VDIM
    kblk = A_HEADS // hp
    return pl.pallas_call(
        functools.partial(_diff_attn_kernel, t=t, lam_init=lam_init),
        out_shape=jax.ShapeDtypeStruct((bsz, seq, A_HEADS * A_VDIM), BF16),
        grid=(bsz, A_HEADS // hp, nq),
        in_specs=[
            pl.BlockSpec((4, A_HALF), lambda b, h, i: (0, 0)),
            pl.BlockSpec((A_VDIM, 1), lambda b, h, i: (0, 0)),
            pl.BlockSpec((None, t, wid), lambda b, h, i: (b, i, h)),
            pl.BlockSpec((None, seq, wid), lambda b, h, i: (b, 0, kblk + h), pipeline_mode=PER_GROUP_WINDOW),
            pl.BlockSpec((None, nq, wid, t), lambda b, h, i: (b, 0, h, 0), pipeline_mode=PER_GROUP_WINDOW),
            pl.BlockSpec((hp, 2, t, t), lambda b, h, i: (h, 0, 0, 0), pipeline_mode=PER_GROUP_WINDOW),
        ],
        out_specs=pl.BlockSpec((None, t, wid), lambda b, h, i: (b, i, h)),
        scratch_shapes=_attn_scratch(hp, A_VDIM, t, 2 * t),
        compiler_params=pltpu.CompilerParams(
            dimension_semantics=("parallel", "parallel", "arbitrary"),
            vmem_limit_bytes=_vmem_limit(
                2 * _nbytes((t, wid), BF16),
                single_buffered_bytes=2 * _nbytes((seq, wid), BF16) + _nbytes((hp, 2, t, t), F32),
                scratch_bytes=_nbytes((hp, A_VDIM + t // 2 + t, 2 * t), F32)),
        ),
        name="diff_attention",
    )(lam4, subln_col, qk3, qk3, vt4, bias)


def _fox_attn_kernel(q_ref, k_ref, vt_ref, ck_ref, o_ref, m_sc, l_sc, a_sc, acc_sc, p_sc, *, t):
    qi = pl.program_id(2)
    state = (m_sc, l_sc, a_sc, acc_sc, p_sc)
    qs = [q_ref[:, h * C_HDIM:(h + 1) * C_HDIM] for h in range(HEADS_PER_STEP)]

    def tile(kj, diagonal, stage):
        start = pl.multiple_of(kj * t, t)
        prev = jnp.minimum(kj + 1, qi)
        for h in range(HEADS_PER_STEP):
            cols = slice(h * C_HDIM, (h + 1) * C_HDIM)
            s = lax.dot_general(k_ref[pl.ds(start, t), cols], qs[h], (((1,), (1,)), ((), ())),
                                preferred_element_type=F32)
            ck = ck_ref[h, pl.ds(start, t), :]
            s = s - jnp.concatenate([ck] * (t // V7X_LANES), axis=1)
            if diagonal:
                key = lax.broadcasted_iota(I32, s.shape, 0)
                qry = lax.broadcasted_iota(I32, s.shape, 1)
                s = jnp.where(key <= qry, s, MASKED)
            stage(s, vt_ref[prev, cols, :], h, *state)

    def run(later_stage):
        _attn_init(*state)
        tile(qi, True, _attn_stage)

        def body(i, carry):
            tile(qi - 1 - i, False, later_stage)
            return carry

        lax.fori_loop(0, qi, body, 0)
        for h in range(HEADS_PER_STEP):
            _attn_fold(vt_ref[0, h * C_HDIM:(h + 1) * C_HDIM, :], h, a_sc, acc_sc, p_sc)
            l = _sublane_allreduce(l_sc[h], jnp.add)
            o = (_rows3(acc_sc[h]) / l[None]).reshape(C_HDIM, t)
            o_ref[:, h * C_HDIM:(h + 1) * C_HDIM] = o.T.astype(o_ref.dtype)

    _attn_run_guarded(run, l_sc)


def _fox_attention(qk3, vt4, ckb, *, t):
    bsz, seq, _ = qk3.shape
    nq = seq // t
    hp = HEADS_PER_STEP
    wid = hp * C_HDIM
    qblk = 2 * (A_HEADS // hp)
    kblk = qblk + C_HEADS // hp
    vblk = A_HEADS // hp
    return pl.pallas_call(
        functools.partial(_fox_attn_kernel, t=t),
        out_shape=jax.ShapeDtypeStruct((bsz, seq, C_HEADS * C_HDIM), BF16),
        grid=(bsz, C_HEADS // hp, nq),
        in_specs=[
            pl.BlockSpec((None, t, wid), lambda b, h, i: (b, i, qblk + h)),
            pl.BlockSpec((None, seq, wid), lambda b, h, i: (b, 0, kblk + h), pipeline_mode=PER_GROUP_WINDOW),
            pl.BlockSpec((None, nq, wid, t), lambda b, h, i: (b, 0, vblk + h, 0), pipeline_mode=PER_GROUP_WINDOW),
            pl.BlockSpec((None, hp, seq, V7X_LANES), lambda b, h, i: (b, h, 0, 0), pipeline_mode=PER_GROUP_WINDOW),
        ],
        out_specs=pl.BlockSpec((None, t, wid), lambda b, h, i: (b, i, h)),
        scratch_shapes=_attn_scratch(hp, C_HDIM, t, t),
        compiler_params=pltpu.CompilerParams(
            dimension_semantics=("parallel", "parallel", "arbitrary"),
            vmem_limit_bytes=_vmem_limit(
                2 * _nbytes((t, wid), BF16),
                single_buffered_bytes=2 * _nbytes((seq, wid), BF16) + _nbytes((hp, seq, V7X_LANES), F32),
                scratch_bytes=_nbytes((hp, C_HDIM + t // 2 + t, t), F32)),
        ),
        name="fox_attention",
    )(qk3, qk3, vt4, ckb)


def _lru_kernel(bx_ref, bg_ref, cw_ref, cb_ref, wax_ref, bax_ref, lam_ref, o_ref, xbuf, hprev, *, t):
    pad = V7X_SUBLANES

    @pl.when(pl.program_id(1) == 0)
    def _():
        xbuf[0:pad, :] = jnp.zeros((pad, xbuf.shape[1]), F32)
        hprev[...] = jnp.zeros_like(hprev)

    xbuf[pad:pad + t, :] = bx_ref[...]
    xc = cb_ref[...]
    for tap in range(CONV_WIDTH):
        xc = xc + xbuf[pl.ds(pad - (CONV_WIDTH - 1) + tap, t), :] * cw_ref[tap:tap + 1, :]
    xbuf[0:pad, :] = bx_ref[t - pad:t, :]

    bw = xc.shape[1] // LRU_BLOCKS
    row = lax.broadcasted_iota(I32, (t, bw), 0)
    for g in range(LRU_BLOCKS):
        cols = slice(g * bw, (g + 1) * bw)
        xg = xc[:, cols]
        z = jnp.dot(xg.astype(BF16), wax_ref[g], preferred_element_type=F32) + bax_ref[g]
        r = jax.nn.sigmoid(z[:, :bw])
        gi = jax.nn.sigmoid(z[:, bw:])
        log_a = (LRU_C * r) * _log_sigmoid(lam_ref[:, cols])
        a = jnp.exp(log_a)
        u = jnp.sqrt(1.0 - jnp.exp(2.0 * log_a)) * (gi * xg)
        d = 1
        while d < t:
            keep = row >= d
            a_sh = jnp.where(keep, pltpu.roll(a, d, 0), 1.0)
            u_sh = jnp.where(keep, pltpu.roll(u, d, 0), 0.0)
            u = a * u_sh + u
            a = a * a_sh
            d *= 2
        h = a * hprev[0:1, cols] + u
        hprev[0:1, cols] = h[t - 1:t, :]
        o_ref[:, cols] = (jax.nn.gelu(bg_ref[:, cols], approximate=True) * h).astype(o_ref.dtype)


def _lru_branch(bxg, conv_w, conv_b, wax, bax, lam, *, bsz, t):
    n, two_w = bxg.shape
    w = two_w // 2
    nt = n // bsz // t
    bw = w // LRU_BLOCKS
    return pl.pallas_call(
        functools.partial(_lru_kernel, t=t),
        out_shape=jax.ShapeDtypeStruct((n, w), BF16),
        grid=(bsz, nt),
        in_specs=[
            pl.BlockSpec((t, w), lambda b, i: (b * nt + i, 0)),
            pl.BlockSpec((t, w), lambda b, i: (b * nt + i, 1)),
            pl.BlockSpec((CONV_WIDTH, w), lambda b, i: (0, 0)),
            pl.BlockSpec((1, w), lambda b, i: (0, 0)),
            pl.BlockSpec((LRU_BLOCKS, bw, 2 * bw), lambda b, i: (0, 0, 0)),
            pl.BlockSpec((LRU_BLOCKS, 1, 2 * bw), lambda b, i: (0, 0, 0)),
            pl.BlockSpec((1, w), lambda b, i: (0, 0)),
        ],
        out_specs=pl.BlockSpec((t, w), lambda b, i: (b * nt + i, 0)),
        scratch_shapes=[
            pltpu.VMEM((t + V7X_SUBLANES, w), F32),
            pltpu.VMEM((V7X_SUBLANES, w), F32),
        ],
        compiler_params=pltpu.CompilerParams(dimension_semantics=("parallel", "arbitrary")),
        name="conv_rglru",
    )(bxg, bxg, conv_w, conv_b, wax, bax, lam)


def _merge_kernel(ya_ref, yb_ref, yc_ref, w_ref, g0_ref, g1_ref, g2_ref, o_ref):
    acc = g0_ref[...].astype(F32) * jnp.dot(ya_ref[...], w_ref[0], preferred_element_type=F32)
    acc = acc + g1_ref[...].astype(F32) * jnp.dot(yb_ref[...], w_ref[1], preferred_element_type=F32)
    acc = acc + g2_ref[...].astype(F32) * jnp.dot(yc_ref[...], w_ref[2], preferred_element_type=F32)
    o_ref[...] = acc.astype(o_ref.dtype)


def _merge(ya, yb, yc, wb, gates, *, tm, tn):
    n, bwid = ya.shape
    dm = wb.shape[2]
    nc = dm // tn
    y_spec = pl.BlockSpec((tm, bwid), lambda j, i: (i, 0))
    return pl.pallas_call(
        _merge_kernel,
        out_shape=jax.ShapeDtypeStruct((n, dm), BF16),
        grid=(nc, n // tm),
        in_specs=[
            y_spec, y_spec, y_spec,
            pl.BlockSpec((N_BRANCH, bwid, tn), lambda j, i: (0, 0, j)),
            pl.BlockSpec((tm, tn), lambda j, i: (i, j)),
            pl.BlockSpec((tm, tn), lambda j, i: (i, nc + j)),
            pl.BlockSpec((tm, tn), lambda j, i: (i, 2 * nc + j)),
        ],
        out_specs=pl.BlockSpec((tm, tn), lambda j, i: (i, j)),
        compiler_params=pltpu.CompilerParams(
            dimension_semantics=("parallel", "parallel"),
            vmem_limit_bytes=_vmem_limit(3 * _nbytes((tm, bwid), BF16), _nbytes((N_BRANCH, bwid, tn), BF16),
                                         4 * _nbytes((tm, tn), BF16)),
        ),
        name="gated_merge",
    )(ya, yb, yc, wb, gates, gates, gates)


def _wo_ln_kernel(m_ref, w_ref, x_ref, g_ref, b_ref, o32_ref, o16_ref, *, alpha):
    y = jnp.dot(m_ref[...], w_ref[...], preferred_element_type=F32) + alpha * x_ref[...]
    out = _layer_norm_rows(y, g_ref[...], b_ref[...])
    o32_ref[...] = out
    o16_ref[...] = out.astype(BF16)


def _wo_ln(merged, wo, x, g, b, *, alpha, tm):
    n, dm = x.shape
    row = pl.BlockSpec((tm, dm), lambda i: (i, 0))
    vec = pl.BlockSpec((1, dm), lambda i: (0, 0))
    return pl.pallas_call(
        functools.partial(_wo_ln_kernel, alpha=alpha),
        out_shape=(jax.ShapeDtypeStruct((n, dm), F32), jax.ShapeDtypeStruct((n, dm), BF16)),
        grid=(n // tm,),
        in_specs=[row, pl.BlockSpec((dm, dm), lambda i: (0, 0), pipeline_mode=pl.Buffered(1)), row, vec, vec],
        out_specs=(row, row),
        compiler_params=pltpu.CompilerParams(
            dimension_semantics=("parallel",),
            vmem_limit_bytes=_vmem_limit(2 * _nbytes((tm, dm), BF16), 2 * _nbytes((tm, dm), F32),
                                         single_buffered_bytes=_nbytes((dm, dm), BF16),
                                         scratch_bytes=_nbytes((tm, dm), F32)),
        ),
        name="wo_layernorm",
    )(merged, wo, x, g, b)


def _top_rows(s, payload, k):
    rows = s.shape[0]
    rid = lax.broadcasted_iota(I32, s.shape, 0)
    vals, pays = [], []
    for _ in range(k):
        m = jnp.max(s, axis=0, keepdims=True)
        pos = jnp.min(jnp.where(s == m, rid, rows), axis=0, keepdims=True)
        hit = rid == pos
        vals.append(m)
        pays.append(pos if payload is None else jnp.max(jnp.where(hit, payload, -1), axis=0, keepdims=True))
        s = jnp.where(hit, -jnp.inf, s)
    return jnp.concatenate(vals, axis=0), jnp.concatenate(pays, axis=0)


def _peer_route_kernel(x_ref, wq_ref, keys_ref, idx_ref, g_ref):
    q = jnp.dot(x_ref[...], wq_ref[...], preferred_element_type=F32).astype(BF16)
    st = lax.dot_general(keys_ref[...], q, (((1,), (1,)), ((), ())), preferred_element_type=F32)
    s1, i1 = _top_rows(st[:PEER_NKEYS], None, PEER_TOPK)
    s2, i2 = _top_rows(st[PEER_NKEYS:], None, PEER_TOPK)
    cand, cidx = [], []
    for a in range(PEER_TOPK):
        nb = PEER_TOPK // (a + 1)
        cand.append(s1[a:a + 1] + s2[:nb])
        cidx.append(i1[a:a + 1] * PEER_NKEYS + i2[:nb])
    npairs = sum(c.shape[0] for c in cand)
    npad = -npairs % V7X_SUBLANES
    cand.append(jnp.full((npad, st.shape[1]), -jnp.inf, F32))
    cidx.append(jnp.full((npad, st.shape[1]), -1, I32))
    sc, idx = _top_rows(jnp.concatenate(cand, axis=0), jnp.concatenate(cidx, axis=0), PEER_TOPK)
    e = jnp.exp(sc - jnp.max(sc, axis=0, keepdims=True))
    idx_ref[...] = idx
    g_ref[...] = e / jnp.sum(e, axis=0, keepdims=True)


def _peer_route(x1b, wq, keys_t, *, tm):
    n, dm = x1b.shape
    qd = 2 * PEER_HALF
    slots = PEER_HEADS * PEER_TOPK
    out_spec = pl.BlockSpec((PEER_TOPK, tm), lambda i, h: (h, i))
    return pl.pallas_call(
        _peer_route_kernel,
        out_shape=(jax.ShapeDtypeStruct((slots, n), I32), jax.ShapeDtypeStruct((slots, n), F32)),
        grid=(n // tm, PEER_HEADS),
        in_specs=[
            pl.BlockSpec((tm, dm), lambda i, h: (i, 0)),
            pl.BlockSpec((dm, qd), lambda i, h: (0, h)),
            pl.BlockSpec((None, 2 * PEER_NKEYS, qd), lambda i, h: (h, 0, 0)),
        ],
        out_specs=(out_spec, out_spec),
        compiler_params=pltpu.CompilerParams(dimension_semantics=("parallel", "parallel")),
        name="peer_route",
    )(x1b, wq, keys_t)


W_ROW_PITCH = PEER_NKEYS + V7X_SUBLANES


def _peer_gate_kernel(idx_ref, g_ref, o_ref, wbuf, idx_sc, g_sc, *, tb):
    idx_sc[...] = idx_ref[...].T
    g_sc[...] = g_ref[...].T
    rid = lax.broadcasted_iota(I32, (PEER_NKEYS, PEER_NKEYS), 0).astype(F32).astype(BF16)
    one = jnp.ones((PEER_NKEYS, PEER_NKEYS), BF16)
    zero = jnp.zeros((PEER_NKEYS, PEER_NKEYS), BF16)
    group = 2 * V7X_SUBLANES
    packed_rows = 2 * V7X_SUBLANES

    def body(j, carry):
        base = pl.multiple_of(j * group, group)
        ib = idx_sc[pl.ds(base, group), :]
        gb = g_sc[pl.ds(base, group), :]
        i1b = (ib >> (PEER_NKEYS.bit_length() - 1)).astype(F32)
        i2b = (ib & (PEER_NKEYS - 1)).astype(F32)
        for r in range(group):
            def rep(v):
                one_vreg = jnp.broadcast_to(v[r:r + 1, :], (packed_rows, PEER_NKEYS)).astype(BF16)
                return jnp.concatenate([one_vreg] * (PEER_NKEYS // packed_rows), axis=0)
            r1 = jnp.where(rid == rep(i1b), rep(gb), zero)
            r2t = jnp.where(rid == rep(i2b), one, zero)
            wt = lax.dot_general(r1, r2t, (((1,), (1,)), ((), ())), preferred_element_type=F32)
            wbuf[pl.ds(pl.multiple_of((base + r) * W_ROW_PITCH, V7X_SUBLANES), PEER_NKEYS), :] = wt
        return carry

    lax.fori_loop(0, tb // group, body, 0)
    for i1 in range(PEER_NKEYS):
        o_ref[:, i1 * PEER_NKEYS:(i1 + 1) * PEER_NKEYS] = (
            wbuf[pl.ds(i1, tb, stride=W_ROW_PITCH), :].astype(o_ref.dtype))


def _peer_gate_matrix(idx_t, g_t, *, tb):
    slots, n = idx_t.shape
    ne = PEER_NKEYS * PEER_NKEYS
    in_spec = pl.BlockSpec((slots, tb), lambda i: (0, i))
    return pl.pallas_call(
        functools.partial(_peer_gate_kernel, tb=tb),
        out_shape=jax.ShapeDtypeStruct((n, ne), BF16),
        grid=(n // tb,),
        in_specs=[in_spec, in_spec],
        out_specs=pl.BlockSpec((tb, ne), lambda i: (i, 0)),
        scratch_shapes=[
            pltpu.VMEM((tb * W_ROW_PITCH, PEER_NKEYS), F32),
            pltpu.VMEM((tb, slots), I32),
            pltpu.VMEM((tb, slots), F32),
        ],
        compiler_params=pltpu.CompilerParams(
            dimension_semantics=("parallel",),
            vmem_limit_bytes=_vmem_limit(_nbytes((tb, ne), BF16),
                                         scratch_bytes=_nbytes((tb * W_ROW_PITCH, PEER_NKEYS), F32)),
        ),
        name="peer_gate_matrix",
    )(idx_t, g_t)


def _peer_expert_kernel(x_ref, u_ref, v_ref, w_ref, o_ref):
    @pl.when(pl.program_id(1) == 0)
    def _():
        o_ref[...] = jnp.zeros_like(o_ref)

    a = lax.dot_general(x_ref[...], u_ref[...], (((1,), (1,)), ((), ())), preferred_element_type=F32)
    gelu = 0.5 * a * (1.0 + lax.erf(a * (0.5 ** 0.5)))
    g = (w_ref[...].astype(F32) * gelu).astype(BF16)
    o_ref[...] += jnp.dot(g, v_ref[...], preferred_element_type=F32)


def _peer_experts(x1b, u, v, w, *, tm, ce):
    n, dm = x1b.shape
    ne = v.shape[0]
    table_chunk = pl.BlockSpec((ce, dm), lambda i, j: (j, 0))
    return pl.pallas_call(
        _peer_expert_kernel,
        out_shape=jax.ShapeDtypeStruct((n, dm), F32),
        grid=(n // tm, ne // ce),
        in_specs=[
            pl.BlockSpec((tm, dm), lambda i, j: (i, 0)),
            table_chunk,
            table_chunk,
            pl.BlockSpec((tm, ce), lambda i, j: (i, j)),
        ],
        out_specs=pl.BlockSpec((tm, dm), lambda i, j: (i, 0)),
        compiler_params=pltpu.CompilerParams(
            dimension_semantics=("parallel", "arbitrary"),
            vmem_limit_bytes=_vmem_limit(_nbytes((tm, dm), BF16), 2 * _nbytes((dm, ce), BF16),
                                         _nbytes((tm, ce), BF16), _nbytes((tm, dm), F32)),
        ),
        name="peer_experts",
    )(x1b, u, v, w)


def _ple_ln_kernel(xb_ref, wg_ref, bg_ref, p_ref, we_ref, x_ref, y_ref, g_ref, b_ref, o32_ref, o16_ref, *, alpha):
    gate = jax.nn.sigmoid(jnp.dot(xb_ref[...], wg_ref[...], preferred_element_type=F32) + bg_ref[...])
    ple = gate * jnp.dot(p_ref[...], we_ref[...], preferred_element_type=F32)
    out = _layer_norm_rows(alpha * x_ref[...] + y_ref[...] + ple, g_ref[...], b_ref[...])
    o32_ref[...] = out
    o16_ref[...] = out.astype(BF16)


def _ple_ln(x1b, wg, bg, pb, we, x1, yf, g, b, *, alpha, tm):
    n, dm = x1.shape
    pdim = pb.shape[1]
    row = pl.BlockSpec((tm, dm), lambda i: (i, 0))
    vec = pl.BlockSpec((1, dm), lambda i: (0, 0))
    resident = pl.Buffered(1)
    return pl.pallas_call(
        functools.partial(_ple_ln_kernel, alpha=alpha),
        out_shape=(jax.ShapeDtypeStruct((n, dm), F32), jax.ShapeDtypeStruct((n, dm), BF16)),
        grid=(n // tm,),
        in_specs=[
            row,
            pl.BlockSpec((dm, dm), lambda i: (0, 0), pipeline_mode=resident),
            vec,
            pl.BlockSpec((tm, pdim), lambda i: (i, 0)),
            pl.BlockSpec((pdim, dm), lambda i: (0, 0), pipeline_mode=resident),
            row, row, vec, vec,
        ],
        out_specs=(row, row),
        compiler_params=pltpu.CompilerParams(
            dimension_semantics=("parallel",),
            vmem_limit_bytes=_vmem_limit(2 * _nbytes((tm, dm), BF16), 3 * _nbytes((tm, dm), F32),
                                         single_buffered_bytes=_nbytes((dm + pdim, dm), BF16),
                                         scratch_bytes=2 * _nbytes((tm, dm), F32)),
        ),
        name="ple_residual_layernorm",
    )(x1b, wg, bg, pb, we, x1, yf, g, b)


def _peer_keys_layout(keys):
    z = jnp.zeros_like(keys[:, 0])
    top = jnp.concatenate([keys[:, 0], z], axis=-1)
    bot = jnp.concatenate([z, keys[:, 1]], axis=-1)
    return jnp.concatenate([top, bot], axis=1).astype(BF16)


def kernel(x, p, w_in, b_in, diff_lambda, diff_subln, rel_bias, conv_w, conv_b, lru_wa, lru_ba, lru_wx,
           lru_bx, lru_lambda, w_branch, w_o, ln1_g, ln1_b, peer_wq, peer_keys, peer_u, peer_v, w_ple,
           w_ple_gate, b_ple_gate, ln2_g, ln2_b):
    bsz, seq, dm = x.shape
    depth = w_in.shape[0]
    n = bsz * seq
    alpha = (2 * depth) ** 0.25
    bwid = A_HEADS * A_VDIM
    t_attn = min(T_ATTN, seq)
    assert seq % t_attn == 0 and t_attn >= REL_MAX_DIST

    o_va, o_lru, o_c, o_vc, o_f = (j * bwid for j in (2, 3, 5, 7, 8))
    o_g = o_f + C_HEADS
    qk_scale = jnp.concatenate([
        jnp.full((bwid,), A_HALF ** -0.5 * LOG2E, F32), jnp.ones((bwid,), F32),
        jnp.full((bwid,), C_HDIM ** -0.5 * LOG2E, F32), jnp.ones((bwid,), F32)])[None]

    bias = _bias_tiles(rel_bias, t_attn)
    x2 = x.reshape(n, dm)
    xb = x2.astype(BF16)
    for i in range(depth):
        w = w_in[i]
        b = b_in[i]
        w_qk = jnp.concatenate([w[:, :o_va], w[:, o_c:o_vc]], axis=1).astype(BF16)
        b_qk = jnp.concatenate([b[:o_va], b[o_c:o_vc]])[None]
        qk = _proj(xb, w_qk, b_qk, qk_scale, out_dtype=BF16, tm=TM_PROJ, tn=TN_PROJ, name="proj_qk")
        w_v = jnp.concatenate([w[:, o_va:o_lru], w[:, o_vc:o_f]], axis=1).T.astype(BF16)
        b_v = jnp.concatenate([b[o_va:o_lru], b[o_vc:o_f]])[:, None]
        vt = _proj_t(xb, w_v, b_v, tm=t_attn, tn=bwid, name="proj_v_t")
        bxg = _proj(xb, w[:, o_lru:o_c].astype(BF16), b[None, o_lru:o_c],
                    out_dtype=F32, tm=TM_PROJ, tn=TN_PROJ_F32, name="proj_lru")
        gates = _proj(xb, w[:, o_g:].astype(BF16), b[None, o_g:],
                      out_dtype=BF16, tm=TM_PROJ, tn=TN_PROJ, act="sigmoid", name="proj_gates")
        ckb = _fgate_cumsum(xb.reshape(bsz, seq, dm), w[:, o_f:o_g].T.astype(BF16), b[o_f:o_g, None],
                            ts=min(seq, TS_FGATE))

        qk3 = qk.reshape(bsz, seq, 4 * bwid)
        vt4 = vt.reshape(bsz, seq // t_attn, 2 * bwid, t_attn)
        lam_init = 0.8 - 0.6 * math.exp(-0.3 * i)
        ya = _diff_attention(qk3, vt4, diff_lambda[i], diff_subln[i][:, None], bias, t=t_attn, lam_init=lam_init)
        yc = _fox_attention(qk3, vt4, ckb, t=t_attn)

        bw = bwid // LRU_BLOCKS
        wax = jnp.concatenate([lru_wa[i], lru_wx[i]], axis=-1).astype(BF16)
        bax = jnp.concatenate([lru_ba[i].reshape(LRU_BLOCKS, 1, bw), lru_bx[i].reshape(LRU_BLOCKS, 1, bw)], axis=-1)
        yb = _lru_branch(bxg, conv_w[i], conv_b[i][None], wax, bax, lru_lambda[i][None], bsz=bsz, t=T_SCAN)

        merged = _merge(ya.reshape(n, bwid), yb, yc.reshape(n, bwid), w_branch[i].astype(BF16), gates,
                        tm=TM_MERGE, tn=TN_MERGE)
        x1, x1b = _wo_ln(merged, w_o[i].astype(BF16), x2, ln1_g[i][None], ln1_b[i][None], alpha=alpha, tm=TM_LN)

        idx_t, g_t = _peer_route(x1b, peer_wq[i].astype(BF16), _peer_keys_layout(peer_keys[i]), tm=TM_ROUTE)
        wdense = _peer_gate_matrix(idx_t, g_t, tb=TB_GATE)
        yf = _peer_experts(x1b, peer_u[i].astype(BF16), peer_v[i].astype(BF16), wdense, tm=TM_EXPERT, ce=CE_EXPERT)
        x2, xb = _ple_ln(x1b, w_ple_gate[i].astype(BF16), b_ple_gate[i][None], p[i].reshape(n, -1).astype(BF16),
                         w_ple[i].astype(BF16), x1, yf, ln2_g[i][None], ln2_b[i][None], alpha=alpha, tm=TM_LN)
    return x2.reshape(bsz, seq, dm)
```

```python
import functools
import math

import jax
import jax.numpy as jnp
import numpy as np
from jax import lax
from jax.experimental import pallas as pl
from jax.experimental.pallas import tpu as pltpu

F32, BF16, I32 = jnp.float32, jnp.bfloat16, jnp.int32

V7X_VMEM_BYTES = 64 * 2**20
V7X_LANES = 128
V7X_SUBLANES = 8
VMEM_HEADROOM_BYTES = 8 * 2**20

A_HEADS = 8
A_HALF = 64
A_VDIM = 2 * A_HALF
LRU_BLOCKS = 8
CONV_WIDTH = 4
LRU_C = 8.0
C_HEADS = 8
C_HDIM = 128
N_BRANCH = 3
REL_BUCKETS = 32
REL_MAX_DIST = 128
PEER_HEADS = 8
PEER_NKEYS = 128
PEER_HALF = 64
PEER_TOPK = 16
LN_EPS = 1e-5

LOG2E = 1.4426950408889634
MASKED = -1e30

T_ATTN = 512
TM_PROJ, TN_PROJ = 1024, 2048
TN_PROJ_F32 = 1024
TS_FGATE = 1024
T_SCAN = 256
TM_MERGE, TN_MERGE = 512, 1024
TM_LN = 512
TM_ROUTE = 1024
TB_GATE = 256
TM_EXPERT, CE_EXPERT = 1024, 1024


def _vmem_limit(*block_bytes, scratch_bytes=0, single_buffered_bytes=0):
    need = 2 * sum(block_bytes) + single_buffered_bytes + scratch_bytes + VMEM_HEADROOM_BYTES
    return int(min(max(need, 32 * 2**20), V7X_VMEM_BYTES - 4 * 2**20))


def _nbytes(shape, dtype):
    return int(np.prod(shape)) * jnp.dtype(dtype).itemsize


def _log_sigmoid(z):
    return jnp.minimum(z, 0.0) - jnp.log1p(jnp.exp(-jnp.abs(z)))


def _layer_norm_rows(y, g, b):
    mu = jnp.mean(y, axis=1, keepdims=True)
    yc = y - mu
    var = jnp.mean(yc * yc, axis=1, keepdims=True)
    return yc * lax.rsqrt(var + LN_EPS) * g + b


def _proj_kernel(x_ref, w_ref, b_ref, *rest, act, scaled):
    o_ref = rest[-1]
    y = jnp.dot(x_ref[...], w_ref[...], preferred_element_type=F32) + b_ref[...]
    if scaled:
        y = y * rest[0][...]
    if act == "sigmoid":
        y = jax.nn.sigmoid(y)
    o_ref[...] = y.astype(o_ref.dtype)


def _proj(x, w, b, s=None, *, out_dtype, tm, tn, act=None, name):
    m, k = x.shape
    n = w.shape[1]
    grid = (n // tn, m // tm)
    row = pl.BlockSpec((1, tn), lambda j, i: (0, j))
    operands = (x, w, b) if s is None else (x, w, b, s)
    return pl.pallas_call(
        functools.partial(_proj_kernel, act=act, scaled=s is not None),
        out_shape=jax.ShapeDtypeStruct((m, n), out_dtype),
        grid=grid,
        in_specs=[
            pl.BlockSpec((tm, k), lambda j, i: (i, 0)),
            pl.BlockSpec((k, tn), lambda j, i: (0, j)),
        ] + [row] * (len(operands) - 2),
        out_specs=pl.BlockSpec((tm, tn), lambda j, i: (i, j)),
        compiler_params=pltpu.CompilerParams(
            dimension_semantics=("parallel", "parallel"),
            vmem_limit_bytes=_vmem_limit(_nbytes((tm, k), x.dtype), _nbytes((k, tn), w.dtype),
                                         _nbytes((tm, tn), out_dtype)),
        ),
        name=name,
    )(*operands)


def _proj_t_kernel(w_ref, x_ref, b_ref, o_ref):
    y = lax.dot_general(w_ref[...], x_ref[...], (((1,), (1,)), ((), ())), preferred_element_type=F32)
    o_ref[...] = (y + b_ref[...]).astype(o_ref.dtype)


def _proj_t(x, w_t, b_col, *, tm, tn, name):
    m, k = x.shape
    n = w_t.shape[0]
    return pl.pallas_call(
        _proj_t_kernel,
        out_shape=jax.ShapeDtypeStruct((m // tm, n, tm), BF16),
        grid=(n // tn, m // tm),
        in_specs=[
            pl.BlockSpec((tn, k), lambda j, i: (j, 0)),
            pl.BlockSpec((tm, k), lambda j, i: (i, 0)),
            pl.BlockSpec((tn, 1), lambda j, i: (j, 0)),
        ],
        out_specs=pl.BlockSpec((None, tn, tm), lambda j, i: (i, j, 0)),
        compiler_params=pltpu.CompilerParams(
            dimension_semantics=("parallel", "parallel"),
            vmem_limit_bytes=_vmem_limit(_nbytes((tm, k), BF16), _nbytes((tn, k), BF16), _nbytes((tn, tm), BF16)),
        ),
        name=name,
    )(w_t, x, b_col)


def _fgate_kernel(x_ref, w_ref, b_ref, o_ref, carry_ref, *, ts):
    @pl.when(pl.program_id(1) == 0)
    def _():
        carry_ref[...] = jnp.zeros_like(carry_ref)

    z = lax.dot_general(w_ref[...], x_ref[...], (((1,), (1,)), ((), ())),
                        preferred_element_type=F32) + b_ref[...]
    c = _log_sigmoid(z)
    lane = lax.broadcasted_iota(I32, c.shape, 1)
    d = 1
    while d < ts:
        c = c + jnp.where(lane >= d, pltpu.roll(c, d, 1), 0.0)
        d *= 2
    c = c + carry_ref[:, 0:1]
    carry_ref[...] = jnp.broadcast_to(c[:, ts - 1:ts], carry_ref.shape)
    c = c * LOG2E
    for h in range(C_HEADS):
        o_ref[h] = jnp.broadcast_to(c[h:h + 1, :], (V7X_LANES, ts)).T


def _fgate_cumsum(x3, wf_t, bf, *, ts):
    bsz, seq, dm = x3.shape
    return pl.pallas_call(
        functools.partial(_fgate_kernel, ts=ts),
        out_shape=jax.ShapeDtypeStruct((bsz, C_HEADS, seq, V7X_LANES), F32),
        grid=(bsz, seq // ts),
        in_specs=[
            pl.BlockSpec((None, ts, dm), lambda b, i: (b, i, 0)),
            pl.BlockSpec((C_HEADS, dm), lambda b, i: (0, 0)),
            pl.BlockSpec((C_HEADS, 1), lambda b, i: (0, 0)),
        ],
        out_specs=pl.BlockSpec((None, C_HEADS, ts, V7X_LANES), lambda b, i: (b, 0, i, 0)),
        scratch_shapes=[pltpu.VMEM((C_HEADS, V7X_LANES), F32)],
        compiler_params=pltpu.CompilerParams(dimension_semantics=("parallel", "arbitrary")),
        name="fgate_cumsum",
    )(x3, wf_t, bf)


def _rel_bucket_tiles(t):
    q = np.arange(t)[None, :]
    k = np.arange(t)[:, None]
    out = []
    for off in (0, t):
        rel = q - k + off
        n = np.maximum(rel, 0)
        max_exact = REL_BUCKETS // 2
        nf = np.maximum(n, 1).astype(np.float32)
        large = max_exact + (np.log(nf / np.float32(max_exact)) / np.float32(math.log(REL_MAX_DIST / max_exact))
                             * np.float32(REL_BUCKETS - max_exact)).astype(np.int32)
        large = np.minimum(large, REL_BUCKETS - 1)
        bkt = np.where(n < max_exact, n, large)
        out.append(np.where(rel >= 0, bkt, -1))
    return np.stack(out).astype(np.int32)


def _bias_kernel(rel_ref, bkt_ref, o_ref):
    h = pl.program_id(0)
    bkt = bkt_ref[...]
    far = rel_ref[REL_BUCKETS - 1, h]
    acc = jnp.zeros(bkt.shape, F32)
    for b in range(REL_BUCKETS):
        acc = jnp.where(bkt == b, rel_ref[b, h] - far, acc)
    o_ref[...] = jnp.where(bkt < 0, MASKED, acc * LOG2E)


def _bias_tiles(rel_bias, t):
    bkt = jnp.asarray(_rel_bucket_tiles(t))
    return pl.pallas_call(
        _bias_kernel,
        out_shape=jax.ShapeDtypeStruct((A_HEADS, 2, t, t), F32),
        grid=(A_HEADS,),
        in_specs=[
            pl.BlockSpec(memory_space=pltpu.SMEM),
            pl.BlockSpec((2, t, t), lambda h: (0, 0, 0)),
        ],
        out_specs=pl.BlockSpec((None, 2, t, t), lambda h: (h, 0, 0, 0)),
        name="rel_bias_tiles",
    )(rel_bias, bkt)


HEADS_PER_STEP = 4
PER_GROUP_WINDOW = pl.Buffered(1)


def _sublane_allreduce(x, op):
    for shift in (1, 2, 4):
        x = op(x, pltpu.roll(x, shift, 0))
    return x


def _rows3(x):
    return x.reshape(x.shape[0] // V7X_SUBLANES, V7X_SUBLANES, x.shape[1])


def _attn_scratch(hp, dv, t, c):
    stat = pltpu.VMEM((hp, V7X_SUBLANES, c), F32)
    return [stat, stat, stat, pltpu.VMEM((hp, dv, c), F32), pltpu.VMEM((hp, t, c), BF16)]


def _attn_init(m_sc, l_sc, a_sc, acc_sc, p_sc):
    m_sc[...] = jnp.full_like(m_sc, MASKED)
    l_sc[...] = jnp.zeros_like(l_sc)
    a_sc[...] = jnp.ones_like(a_sc)
    acc_sc[...] = jnp.zeros_like(acc_sc)
    p_sc[...] = jnp.zeros_like(p_sc)


def _attn_fold(vt_prev, h, a_sc, acc_sc, p_sc):
    pv = jnp.dot(vt_prev, p_sc[h], preferred_element_type=F32)
    acc_sc[h] = (_rows3(acc_sc[h]) * a_sc[h][None]).reshape(pv.shape) + pv


def _attn_stage(s, vt_prev, h, m_sc, l_sc, a_sc, acc_sc, p_sc):
    _attn_fold(vt_prev, h, a_sc, acc_sc, p_sc)
    s3 = _rows3(s)
    m_prev = m_sc[h]
    m_new = jnp.maximum(m_prev, _sublane_allreduce(jnp.max(s3, axis=0), jnp.maximum))
    alpha = jnp.exp2(m_prev - m_new)
    p3 = jnp.exp2(s3 - m_new[None])
    l_sc[h] = alpha * l_sc[h] + jnp.sum(p3, axis=0)
    p_sc[h] = p3.reshape(s.shape).astype(BF16)
    a_sc[h] = alpha
    m_sc[h] = m_new


def _attn_stage_fixed(s, vt_prev, h, m_sc, l_sc, a_sc, acc_sc, p_sc):
    _attn_fold(vt_prev, h, a_sc, acc_sc, p_sc)
    p3 = jnp.exp2(_rows3(s) - m_sc[h][None])
    l_sc[h] = l_sc[h] + jnp.sum(p3, axis=0)
    p_sc[h] = p3.reshape(s.shape).astype(BF16)
    a_sc[h] = jnp.ones_like(a_sc[h])


OVERFLOW_GUARD_LOG2 = 100


def _attn_run_guarded(run, l_sc):
    run(_attn_stage_fixed)

    @pl.when(jnp.max(l_sc[...]) > 2.0 ** OVERFLOW_GUARD_LOG2)
    def _():
        run(_attn_stage)


def _diff_attn_kernel(lam_ref, subln_ref, q_ref, k_ref, vt_ref, bias_ref, o_ref,
                      m_sc, l_sc, a_sc, acc_sc, p_sc, *, t, lam_init):
    qi = pl.program_id(2)
    state = (m_sc, l_sc, a_sc, acc_sc, p_sc)

    qqs = []
    for h in range(HEADS_PER_STEP):
        q = q_ref[:, h * A_VDIM:(h + 1) * A_VDIM]
        lane = lax.broadcasted_iota(I32, q.shape, 1)
        zero = jnp.zeros_like(q)
        qqs.append(jnp.concatenate([jnp.where(lane < A_HALF, q, zero), jnp.where(lane >= A_HALF, q, zero)], axis=0))

    lv = lam_ref[...]
    lam = (jnp.exp(jnp.sum(lv[0:1] * lv[1:2], axis=1, keepdims=True))
           - jnp.exp(jnp.sum(lv[2:3] * lv[3:4], axis=1, keepdims=True)) + lam_init)

    def tile(kj, near, stage):
        start = pl.multiple_of(kj * t, t)
        prev = jnp.minimum(kj + 1, qi)
        for h in range(HEADS_PER_STEP):
            cols = slice(h * A_VDIM, (h + 1) * A_VDIM)
            s = lax.dot_general(k_ref[pl.ds(start, t), cols], qqs[h], (((1,), (1,)), ((), ())),
                                preferred_element_type=F32)
            if near is not None:
                bias = bias_ref[h, near]
                s = s + jnp.concatenate([bias, bias], axis=1)
            stage(s, vt_ref[prev, cols, :], h, *state)

    def run(later_stage):
        _attn_init(*state)
        tile(qi, 0, _attn_stage)

        @pl.when(qi >= 1)
        def _():
            tile(qi - 1, 1, later_stage)

        def far_body(i, carry):
            tile(qi - 2 - i, None, later_stage)
            return carry

        lax.fori_loop(0, jnp.maximum(qi - 1, 0), far_body, 0)
        for h in range(HEADS_PER_STEP):
            _attn_fold(vt_ref[0, h * A_VDIM:(h + 1) * A_VDIM, :], h, a_sc, acc_sc, p_sc)
            l = _sublane_allreduce(l_sc[h], jnp.add)
            r = (_rows3(acc_sc[h]) / l[None]).reshape(A_VDIM, 2 * t)
            o = r[:, :t] - lam * r[:, t:]
            ms = _sublane_allreduce(jnp.sum(_rows3(o * o), axis=0), jnp.add) * (1.0 / A_VDIM)
            y = (_rows3(o) * lax.rsqrt(ms + LN_EPS)[None]).reshape(A_VDIM, t) * (subln_ref[...] * (1.0 - lam_init))
            o_ref[:, h * A_VDIM:(h + 1) * A_VDIM] = y.T.astype(o_ref.dtype)

    _attn_run_guarded(run, l_sc)


def _diff_attention(qk3, vt4, lam4, subln_col, bias, *, t, lam_init):
    bsz, seq, _ = qk3.shape
    nq = seq // t
    hp = HEADS_PER_STEP
    wid = hp * A_VDIM
    kblk = A_HEADS // hp
    return pl.pallas_call(
        functools.partial(_diff_attn_kernel, t=t, lam_init=lam_init),
        out_shape=jax.ShapeDtypeStruct((bsz, seq, A_HEADS * A_VDIM), BF16),
        grid=(bsz, A_HEADS // hp, nq),
        in_specs=[
            pl.BlockSpec((4, A_HALF), lambda b, h, i: (0, 0)),
            pl.BlockSpec((A_VDIM, 1), lambda b, h, i: (0, 0)),
            pl.BlockSpec((None, t, wid), lambda b, h, i: (b, i, h)),
            pl.BlockSpec((None, seq, wid), lambda b, h, i: (b, 0, kblk + h), pipeline_mode=PER_GROUP_WINDOW),
            pl.BlockSpec((None, nq, wid, t), lambda b, h, i: (b, 0, h, 0), pipeline_mode=PER_GROUP_WINDOW),
            pl.BlockSpec((hp, 2, t, t), lambda b, h, i: (h, 0, 0, 0), pipeline_mode=PER_GROUP_WINDOW),
        ],
        out_specs=pl.BlockSpec((None, t, wid), lambda b, h, i: (b, i, h)),
        scratch_shapes=_attn_scratch(hp, A_VDIM, t, 2 * t),
        compiler_params=pltpu.CompilerParams(
            dimension_semantics=("parallel", "parallel", "arbitrary"),
            vmem_limit_bytes=_vmem_limit(
                2 * _nbytes((t, wid), BF16),
                single_buffered_bytes=2 * _nbytes((seq, wid), BF16) + _nbytes((hp, 2, t, t), F32),
                scratch_bytes=_nbytes((hp, A_VDIM + t // 2 + t, 2 * t), F32)),
        ),
        name="diff_attention",
    )(lam4, subln_col, qk3, qk3, vt4, bias)


def _fox_attn_kernel(q_ref, k_ref, vt_ref, ck_ref, o_ref, m_sc, l_sc, a_sc, acc_sc, p_sc, *, t):
    qi = pl.program_id(2)
    state = (m_sc, l_sc, a_sc, acc_sc, p_sc)
    qs = [q_ref[:, h * C_HDIM:(h + 1) * C_HDIM] for h in range(HEADS_PER_STEP)]

    def tile(kj, diagonal, stage):
        start = pl.multiple_of(kj * t, t)
        prev = jnp.minimum(kj + 1, qi)
        for h in range(HEADS_PER_STEP):
            cols = slice(h * C_HDIM, (h + 1) * C_HDIM)
            s = lax.dot_general(k_ref[pl.ds(start, t), cols], qs[h], (((1,), (1,)), ((), ())),
                                preferred_element_type=F32)
            ck = ck_ref[h, pl.ds(start, t), :]
            s = s - jnp.concatenate([ck] * (t // V7X_LANES), axis=1)
            if diagonal:
                key = lax.broadcasted_iota(I32, s.shape, 0)
                qry = lax.broadcasted_iota(I32, s.shape, 1)
                s = jnp.where(key <= qry, s, MASKED)
            stage(s, vt_ref[prev, cols, :], h, *state)

    def run(later_stage):
        _attn_init(*state)
        tile(qi, True, _attn_stage)

        def body(i, carry):
            tile(qi - 1 - i, False, later_stage)
            return carry

        lax.fori_loop(0, qi, body, 0)
        for h in range(HEADS_PER_STEP):
            _attn_fold(vt_ref[0, h * C_HDIM:(h + 1) * C_HDIM, :], h, a_sc, acc_sc, p_sc)
            l = _sublane_allreduce(l_sc[h], jnp.add)
            o = (_rows3(acc_sc[h]) / l[None]).reshape(C_HDIM, t)
            o_ref[:, h * C_HDIM:(h + 1) * C_HDIM] = o.T.astype(o_ref.dtype)

    _attn_run_guarded(run, l_sc)


def _fox_attention(qk3, vt4, ckb, *, t):
    bsz, seq, _ = qk3.shape
    nq = seq // t
    hp = HEADS_PER_STEP
    wid = hp * C_HDIM
    qblk = 2 * (A_HEADS // hp)
    kblk = qblk + C_HEADS // hp
    vblk = A_HEADS // hp
    return pl.pallas_call(
        functools.partial(_fox_attn_kernel, t=t),
        out_shape=jax.ShapeDtypeStruct((bsz, seq, C_HEADS * C_HDIM), BF16),
        grid=(bsz, C_HEADS // hp, nq),
        in_specs=[
            pl.BlockSpec((None, t, wid), lambda b, h, i: (b, i, qblk + h)),
            pl.BlockSpec((None, seq, wid), lambda b, h, i: (b, 0, kblk + h), pipeline_mode=PER_GROUP_WINDOW),
            pl.BlockSpec((None, nq, wid, t), lambda b, h, i: (b, 0, vblk + h, 0), pipeline_mode=PER_GROUP_WINDOW),
            pl.BlockSpec((None, hp, seq, V7X_LANES), lambda b, h, i: (b, h, 0, 0), pipeline_mode=PER_GROUP_WINDOW),
        ],
        out_specs=pl.BlockSpec((None, t, wid), lambda b, h, i: (b, i, h)),
        scratch_shapes=_attn_scratch(hp, C_HDIM, t, t),
        compiler_params=pltpu.CompilerParams(
            dimension_semantics=("parallel", "parallel", "arbitrary"),
            vmem_limit_bytes=_vmem_limit(
                2 * _nbytes((t, wid), BF16),
                single_buffered_bytes=2 * _nbytes((seq, wid), BF16) + _nbytes((hp, seq, V7X_LANES), F32),
                scratch_bytes=_nbytes((hp, C_HDIM + t // 2 + t, t), F32)),
        ),
        name="fox_attention",
    )(qk3, qk3, vt4, ckb)


def _lru_kernel(bx_ref, bg_ref, cw_ref, cb_ref, wax_ref, bax_ref, lam_ref, o_ref, xbuf, hprev, *, t):
    pad = V7X_SUBLANES

    @pl.when(pl.program_id(1) == 0)
    def _():
        xbuf[0:pad, :] = jnp.zeros((pad, xbuf.shape[1]), F32)
        hprev[...] = jnp.zeros_like(hprev)

    xbuf[pad:pad + t, :] = bx_ref[...]
    xc = cb_ref[...]
    for tap in range(CONV_WIDTH):
        xc = xc + xbuf[pl.ds(pad - (CONV_WIDTH - 1) + tap, t), :] * cw_ref[tap:tap + 1, :]
    xbuf[0:pad, :] = bx_ref[t - pad:t, :]

    bw = xc.shape[1] // LRU_BLOCKS
    row = lax.broadcasted_iota(I32, (t, bw), 0)
    for g in range(LRU_BLOCKS):
        cols = slice(g * bw, (g + 1) * bw)
        xg = xc[:, cols]
        z = jnp.dot(xg.astype(BF16), wax_ref[g], preferred_element_type=F32) + bax_ref[g]
        r = jax.nn.sigmoid(z[:, :bw])
        gi = jax.nn.sigmoid(z[:, bw:])
        log_a = (LRU_C * r) * _log_sigmoid(lam_ref[:, cols])
        a = jnp.exp(log_a)
        u = jnp.sqrt(1.0 - jnp.exp(2.0 * log_a)) * (gi * xg)
        d = 1
        while d < t:
            keep = row >= d
            a_sh = jnp.where(keep, pltpu.roll(a, d, 0), 1.0)
            u_sh = jnp.where(keep, pltpu.roll(u, d, 0), 0.0)
            u = a * u_sh + u
            a = a * a_sh
            d *= 2
        h = a * hprev[0:1, cols] + u
        hprev[0:1, cols] = h[t - 1:t, :]
        o_ref[:, cols] = (jax.nn.gelu(bg_ref[:, cols], approximate=True) * h).astype(o_ref.dtype)


def _lru_branch(bxg, conv_w, conv_b, wax, bax, lam, *, bsz, t):
    n, two_w = bxg.shape
    w = two_w // 2
    nt = n // bsz // t
    bw = w // LRU_BLOCKS
    return pl.pallas_call(
        functools.partial(_lru_kernel, t=t),
        out_shape=jax.ShapeDtypeStruct((n, w), BF16),
        grid=(bsz, nt),
        in_specs=[
            pl.BlockSpec((t, w), lambda b, i: (b * nt + i, 0)),
            pl.BlockSpec((t, w), lambda b, i: (b * nt + i, 1)),
            pl.BlockSpec((CONV_WIDTH, w), lambda b, i: (0, 0)),
            pl.BlockSpec((1, w), lambda b, i: (0, 0)),
            pl.BlockSpec((LRU_BLOCKS, bw, 2 * bw), lambda b, i: (0, 0, 0)),
            pl.BlockSpec((LRU_BLOCKS, 1, 2 * bw), lambda b, i: (0, 0, 0)),
            pl.BlockSpec((1, w), lambda b, i: (0, 0)),
        ],
        out_specs=pl.BlockSpec((t, w), lambda b, i: (b * nt + i, 0)),
        scratch_shapes=[
            pltpu.VMEM((t + V7X_SUBLANES, w), F32),
            pltpu.VMEM((V7X_SUBLANES, w), F32),
        ],
        compiler_params=pltpu.CompilerParams(dimension_semantics=("parallel", "arbitrary")),
        name="conv_rglru",
    )(bxg, bxg, conv_w, conv_b, wax, bax, lam)


def _merge_kernel(ya_ref, yb_ref, yc_ref, w_ref, g0_ref, g1_ref, g2_ref, o_ref):
    acc = g0_ref[...].astype(F32) * jnp.dot(ya_ref[...], w_ref[0], preferred_element_type=F32)
    acc = acc + g1_ref[...].astype(F32) * jnp.dot(yb_ref[...], w_ref[1], preferred_element_type=F32)
    acc = acc + g2_ref[...].astype(F32) * jnp.dot(yc_ref[...], w_ref[2], preferred_element_type=F32)
    o_ref[...] = acc.astype(o_ref.dtype)


def _merge(ya, yb, yc, wb, gates, *, tm, tn):
    n, bwid = ya.shape
    dm = wb.shape[2]
    nc = dm // tn
    y_spec = pl.BlockSpec((tm, bwid), lambda j, i: (i, 0))
    return pl.pallas_call(
        _merge_kernel,
        out_shape=jax.ShapeDtypeStruct((n, dm), BF16),
        grid=(nc, n // tm),
        in_specs=[
            y_spec, y_spec, y_spec,
            pl.BlockSpec((N_BRANCH, bwid, tn), lambda j, i: (0, 0, j)),
            pl.BlockSpec((tm, tn), lambda j, i: (i, j)),
            pl.BlockSpec((tm, tn), lambda j, i: (i, nc + j)),
            pl.BlockSpec((tm, tn), lambda j, i: (i, 2 * nc + j)),
        ],
        out_specs=pl.BlockSpec((tm, tn), lambda j, i: (i, j)),
        compiler_params=pltpu.CompilerParams(
            dimension_semantics=("parallel", "parallel"),
            vmem_limit_bytes=_vmem_limit(3 * _nbytes((tm, bwid), BF16), _nbytes((N_BRANCH, bwid, tn), BF16),
                                         4 * _nbytes((tm, tn), BF16)),
        ),
        name="gated_merge",
    )(ya, yb, yc, wb, gates, gates, gates)


def _wo_ln_kernel(m_ref, w_ref, x_ref, g_ref, b_ref, o32_ref, o16_ref, *, alpha):
    y = jnp.dot(m_ref[...], w_ref[...], preferred_element_type=F32) + alpha * x_ref[...]
    out = _layer_norm_rows(y, g_ref[...], b_ref[...])
    o32_ref[...] = out
    o16_ref[...] = out.astype(BF16)


def _wo_ln(merged, wo, x, g, b, *, alpha, tm):
    n, dm = x.shape
    row = pl.BlockSpec((tm, dm), lambda i: (i, 0))
    vec = pl.BlockSpec((1, dm), lambda i: (0, 0))
    return pl.pallas_call(
        functools.partial(_wo_ln_kernel, alpha=alpha),
        out_shape=(jax.ShapeDtypeStruct((n, dm), F32), jax.ShapeDtypeStruct((n, dm), BF16)),
        grid=(n // tm,),
        in_specs=[row, pl.BlockSpec((dm, dm), lambda i: (0, 0), pipeline_mode=pl.Buffered(1)), row, vec, vec],
        out_specs=(row, row),
        compiler_params=pltpu.CompilerParams(
            dimension_semantics=("parallel",),
            vmem_limit_bytes=_vmem_limit(2 * _nbytes((tm, dm), BF16), 2 * _nbytes((tm, dm), F32),
                                         single_buffered_bytes=_nbytes((dm, dm), BF16),
                                         scratch_bytes=_nbytes((tm, dm), F32)),
        ),
        name="wo_layernorm",
    )(merged, wo, x, g, b)


def _top_rows(s, payload, k):
    rows = s.shape[0]
    rid = lax.broadcasted_iota(I32, s.shape, 0)
    vals, pays = [], []
    for _ in range(k):
        m = jnp.max(s, axis=0, keepdims=True)
        pos = jnp.min(jnp.where(s == m, rid, rows), axis=0, keepdims=True)
        hit = rid == pos
        vals.append(m)
        pays.append(pos if payload is None else jnp.max(jnp.where(hit, payload, -1), axis=0, keepdims=True))
        s = jnp.where(hit, -jnp.inf, s)
    return jnp.concatenate(vals, axis=0), jnp.concatenate(pays, axis=0)


def _peer_route_kernel(x_ref, wq_ref, keys_ref, idx_ref, g_ref):
    q = jnp.dot(x_ref[...], wq_ref[...], preferred_element_type=F32).astype(BF16)
    st = lax.dot_general(keys_ref[...], q, (((1,), (1,)), ((), ())), preferred_element_type=F32)
    s1, i1 = _top_rows(st[:PEER_NKEYS], None, PEER_TOPK)
    s2, i2 = _top_rows(st[PEER_NKEYS:], None, PEER_TOPK)
    cand, cidx = [], []
    for a in range(PEER_TOPK):
        nb = PEER_TOPK // (a + 1)
        cand.append(s1[a:a + 1] + s2[:nb])
        cidx.append(i1[a:a + 1] * PEER_NKEYS + i2[:nb])
    npairs = sum(c.shape[0] for c in cand)
    npad = -npairs % V7X_SUBLANES
    cand.append(jnp.full((npad, st.shape[1]), -jnp.inf, F32))
    cidx.append(jnp.full((npad, st.shape[1]), -1, I32))
    sc, idx = _top_rows(jnp.concatenate(cand, axis=0), jnp.concatenate(cidx, axis=0), PEER_TOPK)
    e = jnp.exp(sc - jnp.max(sc, axis=0, keepdims=True))
    idx_ref[...] = idx
    g_ref[...] = e / jnp.sum(e, axis=0, keepdims=True)


def _peer_route(x1b, wq, keys_t, *, tm):
    n, dm = x1b.shape
    qd = 2 * PEER_HALF
    slots = PEER_HEADS * PEER_TOPK
    out_spec = pl.BlockSpec((PEER_TOPK, tm), lambda i, h: (h, i))
    return pl.pallas_call(
        _peer_route_kernel,
        out_shape=(jax.ShapeDtypeStruct((slots, n), I32), jax.ShapeDtypeStruct((slots, n), F32)),
        grid=(n // tm, PEER_HEADS),
        in_specs=[
            pl.BlockSpec((tm, dm), lambda i, h: (i, 0)),
            pl.BlockSpec((dm, qd), lambda i, h: (0, h)),
            pl.BlockSpec((None, 2 * PEER_NKEYS, qd), lambda i, h: (h, 0, 0)),
        ],
        out_specs=(out_spec, out_spec),
        compiler_params=pltpu.CompilerParams(dimension_semantics=("parallel", "parallel")),
        name="peer_route",
    )(x1b, wq, keys_t)


W_ROW_PITCH = PEER_NKEYS + V7X_SUBLANES


def _peer_gate_kernel(idx_ref, g_ref, o_ref, wbuf, idx_sc, g_sc, *, tb):
    idx_sc[...] = idx_ref[...].T
    g_sc[...] = g_ref[...].T
    rid = lax.broadcasted_iota(I32, (PEER_NKEYS, PEER_NKEYS), 0).astype(F32).astype(BF16)
    one = jnp.ones((PEER_NKEYS, PEER_NKEYS), BF16)
    zero = jnp.zeros((PEER_NKEYS, PEER_NKEYS), BF16)
    group = 8 * V7X_SUBLANES
    packed_rows = 2 * V7X_SUBLANES

    def body(j, carry):
        base = pl.multiple_of(j * group, group)
        ib = idx_sc[pl.ds(base, group), :]
        gb = g_sc[pl.ds(base, group), :]
        i1b = (ib >> (PEER_NKEYS.bit_length() - 1)).astype(F32)
        i2b = (ib & (PEER_NKEYS - 1)).astype(F32)
        for r in range(group):
            def rep(v):
                one_vreg = jnp.broadcast_to(v[r:r + 1, :], (packed_rows, PEER_NKEYS)).astype(BF16)
                return jnp.concatenate([one_vreg] * (PEER_NKEYS // packed_rows), axis=0)
            r1 = jnp.where(rid == rep(i1b), rep(gb), zero)
            r2t = jnp.where(rid == rep(i2b), one, zero)
            wt = lax.dot_general(r1, r2t, (((1,), (1,)), ((), ())), preferred_element_type=F32)
            wbuf[pl.ds(pl.multiple_of((base + r) * W_ROW_PITCH, V7X_SUBLANES), PEER_NKEYS), :] = wt
        return carry

    lax.fori_loop(0, tb // group, body, 0)
    for i1 in range(PEER_NKEYS):
        o_ref[:, i1 * PEER_NKEYS:(i1 + 1) * PEER_NKEYS] = (
            wbuf[pl.ds(i1, tb, stride=W_ROW_PITCH), :].astype(o_ref.dtype))


def _peer_gate_matrix(idx_t, g_t, *, tb):
    slots, n = idx_t.shape
    ne = PEER_NKEYS * PEER_NKEYS
    in_spec = pl.BlockSpec((slots, tb), lambda i: (0, i))
    return pl.pallas_call(
        functools.partial(_peer_gate_kernel, tb=tb),
        out_shape=jax.ShapeDtypeStruct((n, ne), BF16),
        grid=(n // tb,),
        in_specs=[in_spec, in_spec],
        out_specs=pl.BlockSpec((tb, ne), lambda i: (i, 0)),
        scratch_shapes=[
            pltpu.VMEM((tb * W_ROW_PITCH, PEER_NKEYS), F32),
            pltpu.VMEM((tb, slots), I32),
            pltpu.VMEM((tb, slots), F32),
        ],
        compiler_params=pltpu.CompilerParams(
            dimension_semantics=("parallel",),
            vmem_limit_bytes=_vmem_limit(_nbytes((tb, ne), BF16),
                                         scratch_bytes=_nbytes((tb * W_ROW_PITCH, PEER_NKEYS), F32)),
        ),
        name="peer_gate_matrix",
    )(idx_t, g_t)


def _peer_expert_kernel(x_ref, u_ref, v_ref, w_ref, o_ref):
    @pl.when(pl.program_id(1) == 0)
    def _():
        o_ref[...] = jnp.zeros_like(o_ref)

    a = lax.dot_general(x_ref[...], u_ref[...], (((1,), (1,)), ((), ())), preferred_element_type=F32)
    gelu = 0.5 * a * (1.0 + lax.erf(a * (0.5 ** 0.5)))
    g = (w_ref[...].astype(F32) * gelu).astype(BF16)
    o_ref[...] += jnp.dot(g, v_ref[...], preferred_element_type=F32)


def _peer_experts(x1b, u, v, w, *, tm, ce):
    n, dm = x1b.shape
    ne = v.shape[0]
    table_chunk = pl.BlockSpec((ce, dm), lambda i, j: (j, 0))
    return pl.pallas_call(
        _peer_expert_kernel,
        out_shape=jax.ShapeDtypeStruct((n, dm), F32),
        grid=(n // tm, ne // ce),
        in_specs=[
            pl.BlockSpec((tm, dm), lambda i, j: (i, 0)),
            table_chunk,
            table_chunk,
            pl.BlockSpec((tm, ce), lambda i, j: (i, j)),
        ],
        out_specs=pl.BlockSpec((tm, dm), lambda i, j: (i, 0)),
        compiler_params=pltpu.CompilerParams(
            dimension_semantics=("parallel", "arbitrary"),
            vmem_limit_bytes=_vmem_limit(_nbytes((tm, dm), BF16), 2 * _nbytes((dm, ce), BF16),
                                         _nbytes((tm, ce), BF16), _nbytes((tm, dm), F32)),
        ),
        name="peer_experts",
    )(x1b, u, v, w)


def _ple_ln_kernel(xb_ref, wg_ref, bg_ref, p_ref, we_ref, x_ref, y_ref, g_ref, b_ref, o32_ref, o16_ref, *, alpha):
    gate = jax.nn.sigmoid(jnp.dot(xb_ref[...], wg_ref[...], preferred_element_type=F32) + bg_ref[...])
    ple = gate * jnp.dot(p_ref[...], we_ref[...], preferred_element_type=F32)
    out = _layer_norm_rows(alpha * x_ref[...] + y_ref[...] + ple, g_ref[...], b_ref[...])
    o32_ref[...] = out
    o16_ref[...] = out.astype(BF16)


def _ple_ln(x1b, wg, bg, pb, we, x1, yf, g, b, *, alpha, tm):
    n, dm = x1.shape
    pdim = pb.shape[1]
    row = pl.BlockSpec((tm, dm), lambda i: (i, 0))
    vec = pl.BlockSpec((1, dm), lambda i: (0, 0))
    resident = pl.Buffered(1)
    return pl.pallas_call(
        functools.partial(_ple_ln_kernel, alpha=alpha),
        out_shape=(jax.ShapeDtypeStruct((n, dm), F32), jax.ShapeDtypeStruct((n, dm), BF16)),
        grid=(n // tm,),
        in_specs=[
            row,
            pl.BlockSpec((dm, dm), lambda i: (0, 0), pipeline_mode=resident),
            vec,
            pl.BlockSpec((tm, pdim), lambda i: (i, 0)),
            pl.BlockSpec((pdim, dm), lambda i: (0, 0), pipeline_mode=resident),
            row, row, vec, vec,
        ],
        out_specs=(row, row),
        compiler_params=pltpu.CompilerParams(
            dimension_semantics=("parallel",),
            vmem_limit_bytes=_vmem_limit(2 * _nbytes((tm, dm), BF16), 3 * _nbytes((tm, dm), F32),
                                         single_buffered_bytes=_nbytes((dm + pdim, dm), BF16),
                                         scratch_bytes=2 * _nbytes((tm, dm), F32)),
        ),
        name="ple_residual_layernorm",
    )(x1b, wg, bg, pb, we, x1, yf, g, b)


def _peer_keys_layout(keys):
    z = jnp.zeros_like(keys[:, 0])
    top = jnp.concatenate([keys[:, 0], z], axis=-1)
    bot = jnp.concatenate([z, keys[:, 1]], axis=-1)
    return jnp.concatenate([top, bot], axis=1).astype(BF16)


def kernel(x, p, w_in, b_in, diff_lambda, diff_subln, rel_bias, conv_w, conv_b, lru_wa, lru_ba, lru_wx,
           lru_bx, lru_lambda, w_branch, w_o, ln1_g, ln1_b, peer_wq, peer_keys, peer_u, peer_v, w_ple,
           w_ple_gate, b_ple_gate, ln2_g, ln2_b):
    bsz, seq, dm = x.shape
    depth = w_in.shape[0]
    n = bsz * seq
    alpha = (2 * depth) ** 0.25
    bwid = A_HEADS * A_VDIM
    t_attn = min(T_ATTN, seq)
    assert seq % t_attn == 0 and t_attn >= REL_MAX_DIST

    o_va, o_lru, o_c, o_vc, o_f = (j * bwid for j in (2, 3, 5, 7, 8))
    o_g = o_f + C_HEADS
    qk_scale = jnp.concatenate([
        jnp.full((bwid,), A_HALF ** -0.5 * LOG2E, F32), jnp.ones((bwid,), F32),
        jnp.full((bwid,), C_HDIM ** -0.5 * LOG2E, F32), jnp.ones((bwid,), F32)])[None]

    bias = _bias_tiles(rel_bias, t_attn)
    x2 = x.reshape(n, dm)
    xb = x2.astype(BF16)
    for i in range(depth):
        w = w_in[i]
        b = b_in[i]
        w_qk = jnp.concatenate([w[:, :o_va], w[:, o_c:o_vc]], axis=1).astype(BF16)
        b_qk = jnp.concatenate([b[:o_va], b[o_c:o_vc]])[None]
        qk = _proj(xb, w_qk, b_qk, qk_scale, out_dtype=BF16, tm=TM_PROJ, tn=TN_PROJ, name="proj_qk")
        w_v = jnp.concatenate([w[:, o_va:o_lru], w[:, o_vc:o_f]], axis=1).T.astype(BF16)
        b_v = jnp.concatenate([b[o_va:o_lru], b[o_vc:o_f]])[:, None]
        vt = _proj_t(xb, w_v, b_v, tm=t_attn, tn=bwid, name="proj_v_t")
        bxg = _proj(xb, w[:, o_lru:o_c].astype(BF16), b[None, o_lru:o_c],
                    out_dtype=F32, tm=TM_PROJ, tn=TN_PROJ_F32, name="proj_lru")
        gates = _proj(xb, w[:, o_g:].astype(BF16), b[None, o_g:],
                      out_dtype=BF16, tm=TM_PROJ, tn=TN_PROJ, act="sigmoid", name="proj_gates")
        ckb = _fgate_cumsum(xb.reshape(bsz, seq, dm), w[:, o_f:o_g].T.astype(BF16), b[o_f:o_g, None],
                            ts=min(seq, TS_FGATE))

        qk3 = qk.reshape(bsz, seq, 4 * bwid)
        vt4 = vt.reshape(bsz, seq // t_attn, 2 * bwid, t_attn)
        lam_init = 0.8 - 0.6 * math.exp(-0.3 * i)
        ya = _diff_attention(qk3, vt4, diff_lambda[i], diff_subln[i][:, None], bias, t=t_attn, lam_init=lam_init)
        yc = _fox_attention(qk3, vt4, ckb, t=t_attn)

        bw = bwid // LRU_BLOCKS
        wax = jnp.concatenate([lru_wa[i], lru_wx[i]], axis=-1).astype(BF16)
        bax = jnp.concatenate([lru_ba[i].reshape(LRU_BLOCKS, 1, bw), lru_bx[i].reshape(LRU_BLOCKS, 1, bw)], axis=-1)
        yb = _lru_branch(bxg, conv_w[i], conv_b[i][None], wax, bax, lru_lambda[i][None], bsz=bsz, t=T_SCAN)

        merged = _merge(ya.reshape(n, bwid), yb, yc.reshape(n, bwid), w_branch[i].astype(BF16), gates,
                        tm=TM_MERGE, tn=TN_MERGE)
        x1, x1b = _wo_ln(merged, w_o[i].astype(BF16), x2, ln1_g[i][None], ln1_b[i][None], alpha=alpha, tm=TM_LN)

        idx_t, g_t = _peer_route(x1b, peer_wq[i].astype(BF16), _peer_keys_layout(peer_keys[i]), tm=TM_ROUTE)
        wdense = _peer_gate_matrix(idx_t, g_t, tb=TB_GATE)
        yf = _peer_experts(x1b, peer_u[i].astype(BF16), peer_v[i].astype(BF16), wdense, tm=TM_EXPERT, ce=CE_EXPERT)
        x2, xb = _ple_ln(x1b, w_ple_gate[i].astype(BF16), b_ple_gate[i][None], p[i].reshape(n, -1).astype(BF16),
                         w_ple[i].astype(BF16), x1, yf, ln2_g[i][None], ln2_b[i][None], alpha=alpha, tm=TM_LN)
    return x2.reshape(bsz, seq, dm)
```
